```python
import math
import jax
import jax.numpy as jnp
from jax import lax
import numpy as np

D_MODEL = 1024
BATCH = 8
SEQ = 2048
DEPTH = 2

GRID_W = 64
CTX_LEN = 256

NA_HEADS = 8
NA_DIM = 64
NA_W = NA_HEADS * NA_DIM
WIN_R = 8
WIN_C = 16
NA_QCB = 16
NA_KCS = 32

DN_HEADS = 8
DN_DK = 64
DN_DV = 64
DN_KW = DN_HEADS * DN_DK
DN_VW = DN_HEADS * DN_DV
DN_CHUNK = 64

SSD_HEADS = 16
SSD_P = 64
SSD_GROUPS = 2
SSD_N = 128
SSD_DI = SSD_HEADS * SSD_P
SSD_BC = SSD_GROUPS * SSD_N
SSD_XBC = SSD_DI + 2 * SSD_BC
SSD_CHUNK = 64

CONV_W = 5
D_FF = 4 * D_MODEL
ROPE_THETA = 10000.0
EPS = 1e-6

IN_SPLITS = (3 * NA_W, 2 * DN_KW + DN_VW, DN_VW, 2 * DN_HEADS, 2 * DN_HEADS,
             SSD_DI, SSD_XBC, 2 * SSD_HEADS, 3 * D_MODEL)
D_IN = sum(IN_SPLITS)

F32 = jnp.float32

kernel_name = 'hybrid_na_gdn_ssd_dit_block'


def rms_norm(x, gain):
    xf = x.astype(F32)
    y = xf * lax.rsqrt(jnp.mean(xf * xf, axis=-1, keepdims=True) + EPS)
    return (y * gain.astype(F32)).astype(x.dtype)


def l2_norm(x):
    xf = x.astype(F32)
    return xf * lax.rsqrt(jnp.sum(xf * xf, axis=-1, keepdims=True) + EPS)


def modulate(x, shift, scale):
    return x * (1.0 + scale) + shift


def _flip(t):
    return None if t is None else jnp.flip(t, axis=1)


def dwconv_centred(x, w, b=None):
    y = lax.conv_general_dilated(
        x, w[:, None, :].astype(x.dtype), window_strides=(1,),
        padding=[(CONV_W // 2, CONV_W // 2)],
        dimension_numbers=('NWC', 'WIO', 'NWC'), feature_group_count=x.shape[-1])
    return y if b is None else y + b.astype(y.dtype)


def axial_rope(x):
    seq_len, dh = x.shape[1], x.shape[-1]
    half, quarter = dh // 2, dh // 4
    pos = jnp.arange(seq_len)
    inv_freq = ROPE_THETA ** (-jnp.arange(quarter, dtype=F32) / quarter)

    def rotate(xa, p):
        ang = p.astype(F32)[:, None] * inv_freq
        cos, sin = jnp.cos(ang)[None, :, None, :], jnp.sin(ang)[None, :, None, :]
        x1, x2 = xa[..., :quarter], xa[..., quarter:]
        return jnp.concatenate([x1 * cos - x2 * sin, x1 * sin + x2 * cos], axis=-1)

    return jnp.concatenate([rotate(x[..., :half], pos // GRID_W),
                            rotate(x[..., half:], pos % GRID_W)], axis=-1)


def split_cols(u):
    return jnp.split(u, [int(i) for i in np.cumsum(IN_SPLITS)[:-1]], axis=-1)


def neighbourhood_attention(qkv, qkv_c, q_gain, k_gain, rpb, need_ctx):
    bsz, seq_len, _ = qkv.shape
    ctx_len = qkv_c.shape[1]
    t = qkv.reshape(bsz, seq_len, 3, NA_HEADS, NA_DIM)
    tc = qkv_c.reshape(bsz, ctx_len, 3, NA_HEADS, NA_DIM)
    scale = NA_DIM ** -0.5
    q = rms_norm(t[:, :, 0], q_gain) * scale
    k = rms_norm(t[:, :, 1], k_gain)
    v = t[:, :, 2]
    kc = rms_norm(tc[:, :, 1], k_gain)
    vc = tc[:, :, 2]

    rows = seq_len // GRID_W
    wr = min(WIN_R, rows)
    ncb = GRID_W // NA_QCB
    r = np.arange(rows)
    row_idx = np.clip(r - wr // 2, 0, rows - wr)[:, None] + np.arange(wr)
    qcol = np.arange(ncb)[:, None] * NA_QCB + np.arange(NA_QCB)
    kcol = (np.clip(np.arange(ncb) * NA_QCB - WIN_C // 2, 0, GRID_W - NA_KCS)[:, None]
            + np.arange(NA_KCS))
    wstart = np.clip(qcol - WIN_C // 2, 0, GRID_W - WIN_C)
    col_ok = (kcol[:, None, :] >= wstart[..., None]) & (kcol[:, None, :] < wstart[..., None] + WIN_C)
    dr = row_idx - r[:, None] + (WIN_R - 1)
    dc = np.clip(kcol[:, None, :] - qcol[..., None], 1 - WIN_C, WIN_C - 1) + (WIN_C - 1)
    bias = rpb.astype(F32)[:, dr[None, :, None, :, None], dc[:, None, :, None, :]]
    bias = jnp.where(col_ok[:, None, :, None, :], bias, -jnp.inf)
    bias = jnp.moveaxis(bias, 0, 1)

    q_blocks = jnp.moveaxis(q.reshape(bsz, rows, ncb, NA_QCB, NA_HEADS, NA_DIM), 2, 0)
    k_grid = k.reshape(bsz, rows, GRID_W, NA_HEADS, NA_DIM)
    v_grid = v.reshape(bsz, rows, GRID_W, NA_HEADS, NA_DIM)
    row_idx_j = jnp.asarray(row_idx)
    n_win = wr * NA_KCS

    def query_block(args):
        qb, kcol_b, bias_b = args
        kb = jnp.take(k_grid, kcol_b, axis=2)[:, row_idx_j]
        vb = jnp.take(v_grid, kcol_b, axis=2)[:, row_idx_j].reshape(bsz, rows, n_win, NA_HEADS, NA_DIM)
        s_win = jnp.einsum('brqhd,brwkhd->bhrqwk', qb, kb).astype(F32) + bias_b[None]
        s_ctx = jnp.einsum('brqhd,bmhd->bhrqm', qb, kc).astype(F32)
        s = jnp.concatenate([s_win.reshape(bsz, NA_HEADS, rows, NA_QCB, n_win), s_ctx], axis=-1)
        p = jax.nn.softmax(s, axis=-1).astype(v.dtype)
        return (jnp.einsum('bhrqn,brnhd->brqhd', p[..., :n_win], vb)
                + jnp.einsum('bhrqm,bmhd->brqhd', p[..., n_win:], vc))

    o = lax.map(query_block, (q_blocks, jnp.asarray(kcol), bias))
    y = jnp.moveaxis(o, 0, 2).reshape(bsz, seq_len, NA_W)

    yc = None
    if need_ctx:
        qc = rms_norm(tc[:, :, 0], q_gain) * scale
        pc = jax.nn.softmax(jnp.einsum('bmhd,bnhd->bhmn', qc, kc).astype(F32), axis=-1).astype(vc.dtype)
        yc = jnp.einsum('bhmn,bnhd->bmhd', pc, vc).reshape(bsz, ctx_len, NA_W)
    return y, yc


def gdn_chunked(q, k, v, g, beta, s0, with_output):
    bsz, seq_len, n_h, _ = k.shape
    dv = v.shape[-1]
    cl = DN_CHUNK
    nc = seq_len // cl

    def chunks(t):
        return jnp.swapaxes(t.astype(F32).reshape(bsz, nc, cl, n_h, *t.shape[3:]), 2, 3)

    k, v, g, beta = chunks(k), chunks(v), chunks(g), chunks(beta)
    incl = np.tril(np.ones((cl, cl), dtype=bool))
    strict = np.tril(np.ones((cl, cl), dtype=bool), -1)
    gam = jnp.cumsum(g, axis=-1)
    dec = jnp.exp(jnp.where(incl, gam[..., :, None] - gam[..., None, :], -jnp.inf))
    kb = k * beta[..., None]
    a = jnp.where(strict, jnp.einsum('bnhid,bnhjd->bnhij', kb, k) * dec, 0.0)
    eye = jnp.eye(cl, dtype=F32)
    tmat = lax.linalg.triangular_solve(eye + a, jnp.broadcast_to(eye, a.shape),
                                       left_side=True, lower=True, unit_diagonal=True)
    u = tmat @ (v * beta[..., None])
    w = tmat @ (kb * jnp.exp(gam)[..., None])
    gam_last = gam[..., -1]
    kdec = k * jnp.exp(gam_last[..., None] - gam)[..., None]
    xs = [u, w, kdec, gam_last]
    if with_output:
        q = chunks(q)
        xs += [q * jnp.exp(gam)[..., None], jnp.einsum('bnhid,bnhjd->bnhij', q, k) * dec]

    def step(s, inp):
        v_new = inp[0] - inp[1] @ s
        s_next = s * jnp.exp(inp[3])[..., None, None] + jnp.swapaxes(inp[2], -1, -2) @ v_new
        if with_output:
            return s_next, inp[4] @ s + inp[5] @ v_new
        return s_next, None

    s_fin, o = lax.scan(step, s0.astype(F32), [jnp.moveaxis(t, 1, 0) for t in xs])
    if with_output:
        o = jnp.swapaxes(jnp.moveaxis(o, 0, 1), 2, 3).reshape(bsz, seq_len, n_h, dv)
    return o, s_fin


def _gdn_inputs(qkv_s, b_s, a_s, conv_w, a_log, dt_bias, use_rope, need_q):
    bsz, n, _ = qkv_s.shape
    t = jax.nn.silu(dwconv_centred(qkv_s, conv_w)).astype(F32)
    q, k, v = jnp.split(t, [DN_KW, 2 * DN_KW], axis=-1)
    k = l2_norm(k.reshape(bsz, n, DN_HEADS, DN_DK))
    if use_rope:
        k = axial_rope(k)
    if need_q:
        q = l2_norm(q.reshape(bsz, n, DN_HEADS, DN_DK))
        if use_rope:
            q = axial_rope(q)
        q = q * DN_DK ** -0.5
    else:
        q = None
    v = v.reshape(bsz, n, DN_HEADS, DN_DV)
    beta = jax.nn.sigmoid(b_s.astype(F32)).reshape(bsz, n, 2, DN_HEADS)
    g = -jnp.exp(a_log.astype(F32)) * jax.nn.softplus(
        a_s.astype(F32).reshape(bsz, n, 2, DN_HEADS) + dt_bias.astype(F32))
    return q, k, v, beta, g


def gated_deltanet(qkv, z, b_raw, a_raw, qkv_c, z_c, b_raw_c, a_raw_c,
                   conv_w, a_log, dt_bias, o_gain, need_ctx):
    bsz = qkv.shape[0]
    q, k, v, beta, g = _gdn_inputs(qkv, b_raw, a_raw, conv_w, a_log, dt_bias, True, True)
    qc, kc, vc, betac, gc = _gdn_inputs(qkv_c, b_raw_c, a_raw_c, conv_w, a_log, dt_bias, False, need_ctx)
    s0 = jnp.zeros((bsz, DN_HEADS, DN_DK, DN_DV), F32)
    oc_f, sc_f = gdn_chunked(qc, kc, vc, gc[:, :, 0], betac[:, :, 0], s0, need_ctx)
    oc_b, sc_b = gdn_chunked(_flip(qc), _flip(kc), _flip(vc), _flip(gc[:, :, 1]), _flip(betac[:, :, 1]), s0, need_ctx)
    o_f, _ = gdn_chunked(q, k, v, g[:, :, 0], beta[:, :, 0], sc_f, True)
    o_b, _ = gdn_chunked(_flip(q), _flip(k), _flip(v), _flip(g[:, :, 1]), _flip(beta[:, :, 1]), sc_b, True)

    def gated_out(o, zs):
        n = zs.shape[1]
        zh = zs.astype(F32).reshape(bsz, n, DN_HEADS, DN_DV)
        return (rms_norm(o, o_gain) * jax.nn.silu(zh)).reshape(bsz, n, DN_VW)

    y = gated_out(o_f + _flip(o_b), z)
    yc = gated_out(oc_f + _flip(oc_b), z_c) if need_ctx else None
    return y, yc


def ssd_chunked(xs, dt, a, bm, cm, h0, with_output):
    bsz, seq_len, n_h, hp = xs.shape
    n_g, n_s = bm.shape[2], bm.shape[3]
    n_r = n_h // n_g
    cl = SSD_CHUNK
    nc = seq_len // cl
    x = xs.reshape(bsz, nc, cl, n_g, n_r, hp)
    dtc = dt.reshape(bsz, nc, cl, n_g, n_r)
    bc = bm.reshape(bsz, nc, cl, n_g, n_s)
    lam = jnp.cumsum(dtc * a.reshape(n_g, n_r), axis=2)
    lam_last = lam[:, :, -1]
    st = jnp.einsum('bcjgn,bcjgr,bcjgrp->bcgrpn', bc, jnp.exp(lam_last[:, :, None] - lam) * dtc, x)

    def step(h, inp):
        st_c, dec_c = inp
        h_next = h * jnp.exp(dec_c)[..., None, None] + st_c
        return h_next, (h if with_output else None)

    h_fin, h_prev = lax.scan(step, h0.reshape(bsz, n_g, n_r, hp, n_s),
                             (jnp.moveaxis(st, 1, 0), jnp.moveaxis(lam_last, 1, 0)))
    h_fin = h_fin.reshape(bsz, n_h, hp, n_s)
    if not with_output:
        return None, h_fin
    cc = cm.reshape(bsz, nc, cl, n_g, n_s)
    y_inter = jnp.einsum('bcign,bcigr,bcgrpn->bcigrp', cc, jnp.exp(lam), jnp.moveaxis(h_prev, 0, 1))
    incl = np.tril(np.ones((cl, cl), dtype=bool))
    lam_t = jnp.moveaxis(lam, 2, -1)
    dec = jnp.exp(jnp.where(incl, lam_t[..., :, None] - lam_t[..., None, :], -jnp.inf))
    cb = jnp.einsum('bcign,bcjgn->bcgij', cc, bc)
    wgt = cb[:, :, :, None] * dec * jnp.moveaxis(dtc, 2, -1)[..., None, :]
    y_intra = jnp.einsum('bcgrij,bcjgrp->bcigrp', wgt, x)
    return (y_inter + y_intra).reshape(bsz, seq_len, n_h, hp), h_fin


def _ssd_inputs(xbc_s, dt_s, conv_w, conv_b, dt_bias):
    bsz, n, _ = xbc_s.shape
    t = jax.nn.silu(dwconv_centred(xbc_s, conv_w, conv_b)).astype(F32)
    xs, bm, cm = jnp.split(t, [SSD_DI, SSD_DI + SSD_BC], axis=-1)
    dt = jax.nn.softplus(dt_s.astype(F32).reshape(bsz, n, 2, SSD_HEADS) + dt_bias.astype(F32))
    return (xs.reshape(bsz, n, SSD_HEADS, SSD_P), bm.reshape(bsz, n, SSD_GROUPS, SSD_N),
            cm.reshape(bsz, n, SSD_GROUPS, SSD_N), dt)


def mamba2_ssd(z, xbc, dt_raw, z_c, xbc_c, dt_raw_c, conv_w, conv_b, a_log, dt_bias,
               d_skip, o_gain, need_ctx):
    bsz = z.shape[0]
    a = -jnp.exp(a_log.astype(F32))
    xs, bm, cm, dt = _ssd_inputs(xbc, dt_raw, conv_w, conv_b, dt_bias)
    xsc, bmc, cmc, dtc = _ssd_inputs(xbc_c, dt_raw_c, conv_w, conv_b, dt_bias)
    h0 = jnp.zeros((bsz, SSD_HEADS, SSD_P, SSD_N), F32)
    yc_f, hc_f = ssd_chunked(xsc, dtc[:, :, 0], a[0], bmc, cmc, h0, need_ctx)
    yc_b, hc_b = ssd_chunked(_flip(xsc), _flip(dtc[:, :, 1]), a[1], _flip(bmc), _flip(cmc), h0, need_ctx)
    y_f, _ = ssd_chunked(xs, dt[:, :, 0], a[0], bm, cm, hc_f, True)
    y_b, _ = ssd_chunked(_flip(xs), _flip(dt[:, :, 1]), a[1], _flip(bm), _flip(cm), hc_b, True)

    def gated_out(ys, xss, zs):
        n = zs.shape[1]
        yy = (ys + d_skip.astype(F32)[:, None] * xss).reshape(bsz, n, SSD_DI) * jax.nn.silu(zs.astype(F32))
        grp = yy.reshape(bsz, n, SSD_GROUPS, SSD_DI // SSD_GROUPS)
        return rms_norm(grp, o_gain.reshape(SSD_GROUPS, SSD_DI // SSD_GROUPS)).reshape(bsz, n, SSD_DI)

    y = gated_out(y_f + _flip(y_b), xs, z)
    yc = gated_out(yc_f + _flip(yc_b), xsc, z_c) if need_ctx else None
    return y, yc


def token_mixer(h, hc, w_in, na_q_gain, na_k_gain, na_rpb, dn_conv_w, dn_a_log, dn_dt_bias, dn_o_gain,
                ssd_conv_w, ssd_conv_b, ssd_a_log, ssd_dt_bias, ssd_d, ssd_o_gain,
                w_pa, w_pb, w_pc, w_out, need_ctx):
    (na_qkv, dn_qkv, dn_z, dn_b, dn_a, ssd_z, ssd_xbc, ssd_dt, gate_raw) = split_cols(h @ w_in)
    (na_qkv_c, dn_qkv_c, dn_z_c, dn_b_c, dn_a_c, ssd_z_c, ssd_xbc_c, ssd_dt_c, gate_raw_c) = split_cols(hc @ w_in)
    y_a, yc_a = neighbourhood_attention(na_qkv, na_qkv_c, na_q_gain, na_k_gain, na_rpb, need_ctx)
    y_b, yc_b = gated_deltanet(dn_qkv, dn_z, dn_b, dn_a, dn_qkv_c, dn_z_c, dn_b_c, dn_a_c,
                               dn_conv_w, dn_a_log, dn_dt_bias, dn_o_gain, need_ctx)
    y_c, yc_c = mamba2_ssd(ssd_z, ssd_xbc, ssd_dt, ssd_z_c, ssd_xbc_c, ssd_dt_c, ssd_conv_w, ssd_conv_b,
                           ssd_a_log, ssd_dt_bias, ssd_d, ssd_o_gain, need_ctx)

    def merge(ya, yb, ycc, g_raw):
        dtype = g_raw.dtype
        gts = jax.nn.sigmoid(g_raw.astype(F32)).astype(dtype).reshape(*g_raw.shape[:2], 3, D_MODEL)
        m = (gts[:, :, 0] * (ya.astype(dtype) @ w_pa)
             + gts[:, :, 1] * (yb.astype(dtype) @ w_pb)
             + gts[:, :, 2] * (ycc.astype(dtype) @ w_pc))
        return m @ w_out

    y = merge(y_a, y_b, y_c, gate_raw)
    yc = merge(yc_a, yc_b, yc_c, gate_raw_c) if need_ctx else None
    return y, yc


def squared_relu_mlp(h, w1, w2):
    return jnp.square(jax.nn.relu(h @ w1)) @ w2


def setup_inputs(seed: int = 0) -> dict:
    key = jax.random.key(seed)
    keys = iter(jax.random.split(key, 32))

    def normal(shape, scale):
        return scale * jax.random.normal(next(keys), shape, F32)

    def gain(shape):
        return 1.0 + 0.02 * jax.random.normal(next(keys), shape, F32)

    def a_log(shape):
        return jnp.log(jax.random.uniform(next(keys), shape, F32, 1.0, 16.0))

    def dt_bias(shape):
        dt = jnp.exp(jax.random.uniform(next(keys), shape, F32, math.log(1e-3), math.log(1e-1)))
        return dt + jnp.log(-jnp.expm1(-dt))

    L, D = DEPTH, D_MODEL
    return {
        'x': normal((BATCH, SEQ, D), 1.0),
        'c': normal((BATCH, D), 1.0),
        'ctx': normal((BATCH, CTX_LEN, D), 1.0),
        'c_ctx': normal((D,), 1.0),
        'w_ada': normal((L, D, 6 * D), D ** -0.5),
        'b_ada': normal((L, 6 * D), 0.02),
        'norm1_g': gain((L, D)),
        'norm2_g': gain((L, D)),
        'w_in': normal((L, D, D_IN), D ** -0.5),
        'na_q_gain': gain((L, NA_DIM)),
        'na_k_gain': gain((L, NA_DIM)),
        'na_rpb': normal((L, NA_HEADS, 2 * WIN_R - 1, 2 * WIN_C - 1), 0.1),
        'dn_conv_w': normal((L, CONV_W, 2 * DN_KW + DN_VW), CONV_W ** -0.5),
        'dn_a_log': a_log((L, 2, DN_HEADS)),
        'dn_dt_bias': dt_bias((L, 2, DN_HEADS)),
        'dn_o_gain': gain((L, DN_DV)),
        'ssd_conv_w': normal((L, CONV_W, SSD_XBC), CONV_W ** -0.5),
        'ssd_conv_b': normal((L, SSD_XBC), 0.02),
        'ssd_a_log': a_log((L, 2, SSD_HEADS)),
        'ssd_dt_bias': dt_bias((L, 2, SSD_HEADS)),
        'ssd_d': gain((L, SSD_HEADS)),
        'ssd_o_gain': gain((L, SSD_DI)),
        'w_pa': normal((L, NA_W, D), NA_W ** -0.5),
        'w_pb': normal((L, DN_VW, D), DN_VW ** -0.5),
        'w_pc': normal((L, SSD_DI, D), SSD_DI ** -0.5),
        'w_out': normal((L, D, D), D ** -0.5),
        'w_ff1': normal((L, D, D_FF), D ** -0.5),
        'w_ff2': normal((L, D_FF, D), D_FF ** -0.5),
    }


def reference(x, c, ctx, c_ctx, w_ada, b_ada, norm1_g, norm2_g, w_in, na_q_gain, na_k_gain, na_rpb,
              dn_conv_w, dn_a_log, dn_dt_bias, dn_o_gain, ssd_conv_w, ssd_conv_b, ssd_a_log,
              ssd_dt_bias, ssd_d, ssd_o_gain, w_pa, w_pb, w_pc, w_out, w_ff1, w_ff2):
    xc = ctx
    for l in range(DEPTH):
        need_ctx = l < DEPTH - 1
        mod = jnp.split((jax.nn.silu(c) @ w_ada[l] + b_ada[l])[:, None, :], 6, axis=-1)
        mod_c = jnp.split(jax.nn.silu(c_ctx) @ w_ada[l] + b_ada[l], 6, axis=-1)
        h = modulate(rms_norm(x, norm1_g[l]), mod[0], mod[1])
        hc = modulate(rms_norm(xc, norm1_g[l]), mod_c[0], mod_c[1])
        y, yc = token_mixer(h, hc, w_in[l], na_q_gain[l], na_k_gain[l], na_rpb[l],
                            dn_conv_w[l], dn_a_log[l], dn_dt_bias[l], dn_o_gain[l],
                            ssd_conv_w[l], ssd_conv_b[l], ssd_a_log[l], ssd_dt_bias[l], ssd_d[l], ssd_o_gain[l],
                            w_pa[l], w_pb[l], w_pc[l], w_out[l], need_ctx)
        x = x + mod[2] * y
        x = x + mod[5] * squared_relu_mlp(modulate(rms_norm(x, norm2_g[l]), mod[3], mod[4]), w_ff1[l], w_ff2[l])
        if need_ctx:
            xc = xc + mod_c[2] * yc
            xc = xc + mod_c[5] * squared_relu_mlp(
                modulate(rms_norm(xc, norm2_g[l]), mod_c[3], mod_c[4]), w_ff1[l], w_ff2[l])
    return x
```

```python
import functools
import math

import numpy as np
import jax
import jax.numpy as jnp
from jax import lax
from jax.experimental import pallas as pl
from jax.experimental.pallas import tpu as pltpu

F32 = jnp.float32
BF16 = jnp.bfloat16

D_MODEL = 1024
DEPTH = 2
GRID_W = 64
CTX_LEN = 256
NA_HEADS = 8
NA_DIM = 64
NA_W = NA_HEADS * NA_DIM
WIN_R = 8
WIN_C = 16
DN_HEADS = 8
DN_DK = 64
DN_KW = DN_HEADS * DN_DK
DN_VW = DN_HEADS * DN_DK
SSD_HEADS = 16
SSD_P = 64
SSD_GROUPS = 2
SSD_N = 128
SSD_DI = SSD_HEADS * SSD_P
SSD_BC = SSD_GROUPS * SSD_N
SSD_XBC = SSD_DI + 2 * SSD_BC
CONV_W = 5
D_FF = 4 * D_MODEL
ROPE_THETA = 10000.0
EPS = 1e-6
NEG = -1e30

CHUNK = 64
TM = 256
HALO = 16
MOD_ROWS = 16
SMALL_W = 128

_O_NA = 0
_O_DNQKV = _O_NA + 3 * NA_W
_O_DNZ = _O_DNQKV + 2 * DN_KW + DN_VW
_O_DNB = _O_DNZ + DN_VW
_O_DNA = _O_DNB + 2 * DN_HEADS
_O_SSDZ = _O_DNA + 2 * DN_HEADS
_O_XBC = _O_SSDZ + SSD_DI
_O_SSDDT = _O_XBC + SSD_XBC
_O_GATE = _O_SSDDT + 2 * SSD_HEADS
_O_END = _O_GATE + 3 * D_MODEL

U_NA, U_DN, U_XBC, U_DNZ, U_SSDZ, U_GATE = 0, 1536, 3072, 4608, 5120, 6144
U_W = 9216
S_BETA, S_DECAY, S_DT = 0, 16, 32

VMEM_LIMIT = 56 * 1024 * 1024


def _cparams(n_axes):
    return pltpu.CompilerParams(dimension_semantics=("arbitrary",) * n_axes,
                                vmem_limit_bytes=VMEM_LIMIT)


def _dot(a, b):
    return jnp.dot(a, b, preferred_element_type=F32)


def _dot_nt(a, b):
    return lax.dot_general(a, b, (((1,), (1,)), ((), ())), preferred_element_type=F32)


def _dot_exact(a, b):
    return jnp.dot(a, b, preferred_element_type=F32, precision=lax.Precision.HIGHEST)


def _transpose_bf16(a):
    m = a.shape[1]
    eye = (lax.broadcasted_iota(jnp.int32, (m, m), 0) == lax.broadcasted_iota(jnp.int32, (m, m), 1))
    return _dot_nt(eye.astype(BF16), a).astype(BF16)


def _softplus(x):
    return jnp.maximum(x, 0.0) + jnp.log(1.0 + jnp.exp(-jnp.abs(x)))


def _sigmoid(x):
    return 1.0 / (1.0 + jnp.exp(-x))


def _silu(x):
    return x * _sigmoid(x)


def _ada_kernel(c_ref, w_ref, b_ref, o_ref):
    c = c_ref[...]
    a = _silu(c).astype(BF16)
    o_ref[0] = _dot(a, w_ref[0].astype(BF16)) + b_ref[0]


def _ada_call(cs, w_ada, b_ada):
    n_l = w_ada.shape[0]
    tn = 1536
    return pl.pallas_call(
        _ada_kernel,
        out_shape=jax.ShapeDtypeStruct((n_l, MOD_ROWS, 6 * D_MODEL), F32),
        grid=(n_l, 6 * D_MODEL // tn),
        in_specs=[pl.BlockSpec((MOD_ROWS, D_MODEL), lambda l, n: (0, 0)),
                  pl.BlockSpec((1, D_MODEL, tn), lambda l, n: (l, 0, n)),
                  pl.BlockSpec((1, 1, tn), lambda l, n: (l, 0, n))],
        out_specs=pl.BlockSpec((1, MOD_ROWS, tn), lambda l, n: (l, 0, n)),
        compiler_params=_cparams(2),
        name="ada",
    )(cs, w_ada, b_ada.reshape(n_l, 1, 6 * D_MODEL))


def _mod_row(b, t, bsz):
    return jnp.where(t == 0, bsz, b)


def _norm_mod(x, g, shift, scale):
    ms = jnp.mean(x * x, axis=-1, keepdims=True)
    y = x * lax.rsqrt(ms + EPS) * g
    return y * (1.0 + scale) + shift


def _inproj_kernel(x_ref, shift_ref, scale_ref, g_ref, wb_ref, ws_ref, u_ref, us_ref):
    h = _norm_mod(x_ref[0], g_ref[...], shift_ref[0], scale_ref[0]).astype(BF16)
    cw = 1536
    for c in range(U_W // cw):
        u_ref[0, :, c * cw:(c + 1) * cw] = _dot(h, wb_ref[:, c * cw:(c + 1) * cw]).astype(BF16)
    us_ref[0] = _dot(h, ws_ref[...])


def _inproj_call(xs, mod3, g, wb, ws):
    bsz, lt, _ = xs.shape
    nt = lt // TM
    return pl.pallas_call(
        _inproj_kernel,
        out_shape=(jax.ShapeDtypeStruct((bsz, lt, U_W), BF16),
                   jax.ShapeDtypeStruct((bsz, lt, SMALL_W), F32)),
        grid=(bsz, nt),
        in_specs=[pl.BlockSpec((1, TM, D_MODEL), lambda b, t: (b, t, 0)),
                  pl.BlockSpec((1, 1, D_MODEL), lambda b, t: (_mod_row(b, t, bsz), 0, 0)),
                  pl.BlockSpec((1, 1, D_MODEL), lambda b, t: (_mod_row(b, t, bsz), 0, 1)),
                  pl.BlockSpec((1, D_MODEL), lambda b, t: (0, 0)),
                  pl.BlockSpec((D_MODEL, U_W), lambda b, t: (0, 0)),
                  pl.BlockSpec((D_MODEL, SMALL_W), lambda b, t: (0, 0))],
        out_specs=(pl.BlockSpec((1, TM, U_W), lambda b, t: (b, t, 0)),
                   pl.BlockSpec((1, TM, SMALL_W), lambda b, t: (b, t, 0))),
        compiler_params=_cparams(2),
        name="inproj",
    )(xs, mod3, mod3, g, wb, ws)


def _head_rms(x, gain, bd):
    xx = x * x
    hi = xx.astype(BF16)
    lo = (xx - hi.astype(F32)).astype(BF16)
    ss = _dot(hi, bd) + _dot(lo, bd)
    return x * lax.rsqrt(ss * (1.0 / NA_DIM) + EPS) * gain


def _na_kernel(q_ref, k_ref, v_ref, qg_ref, kg_ref, bias_ref, y_ref, qn_s, kn_s, *, need_ctx):
    lt = q_ref.shape[1]
    rows = (lt - CTX_LEN) // GRID_W
    wr = min(WIN_R, rows)
    off = CTX_LEN if need_ctx else 0
    li = lax.broadcasted_iota(jnp.int32, (128, 128), 0) // NA_DIM
    lj = lax.broadcasted_iota(jnp.int32, (128, 128), 1) // NA_DIM
    bd = (li == lj).astype(BF16)
    scale = NA_DIM ** -0.5

    def norm_tile(i, _):
        r0 = pl.multiple_of(i * TM, TM)
        kn_s[pl.ds(r0, TM), :] = _head_rms(k_ref[0, pl.ds(r0, TM), :].astype(F32), kg_ref[...], bd).astype(BF16)
        qn_s[pl.ds(r0, TM), :] = (_head_rms(q_ref[0, pl.ds(r0, TM), :].astype(F32), qg_ref[...], bd)
                                  * scale).astype(BF16)
        return 0

    lax.fori_loop(0, lt // TM, norm_tile, 0)

    lane = lax.broadcasted_iota(jnp.int32, (1, 128), 1)
    head_mask = [lane < NA_DIM, lane >= NA_DIM]
    def attend(qm, keys, vals, biases):
        ss = [_dot_nt(qm, kk) for kk in keys]
        ss = [s if b is None else s + b for s, b in zip(ss, biases)]
        m = ss[0].max(axis=-1, keepdims=True)
        for s in ss[1:]:
            m = jnp.maximum(m, s.max(axis=-1, keepdims=True))
        ps = [jnp.exp(s - m) for s in ss]
        den = ps[0].sum(axis=-1, keepdims=True)
        for p in ps[1:]:
            den = den + p.sum(axis=-1, keepdims=True)
        acc = _dot(ps[0].astype(BF16), vals[0])
        for p, vv in zip(ps[1:], vals[1:]):
            acc = acc + _dot(p.astype(BF16), vv)
        return acc * (1.0 / den)

    def row_body(r, _):
        ws = jnp.clip(r - wr // 2, 0, rows - wr)
        cls = ws - r + (WIN_R - 1)
        q_r = qn_s[pl.ds(pl.multiple_of(CTX_LEN + r * GRID_W, GRID_W), GRID_W), :]
        k0 = pl.multiple_of(CTX_LEN + ws * GRID_W, GRID_W)
        kw = kn_s[pl.ds(k0, wr * GRID_W), :]
        vw = v_ref[0, pl.ds(k0, wr * GRID_W), :]
        kc = kn_s[0:CTX_LEN, :]
        vc = v_ref[0, 0:CTX_LEN, :]
        outs = []
        for hh in range(2):
            qm = jnp.where(head_mask[hh], q_r, jnp.zeros_like(q_r))
            outs.append(attend(qm, [kw, kc], [vw, vc], [bias_ref[0, hh, cls], None]))
        y = jnp.where(head_mask[0], outs[0], outs[1])
        y_ref[0, pl.ds(pl.multiple_of(off + r * GRID_W, GRID_W), GRID_W), :] = y.astype(BF16)
        return 0

    lax.fori_loop(0, rows, row_body, 0)

    if need_ctx:
        qc = qn_s[0:CTX_LEN, :]
        kc = kn_s[0:CTX_LEN, :]
        vc = v_ref[0, 0:CTX_LEN, :]
        outs = []
        for hh in range(2):
            qm = jnp.where(head_mask[hh], qc, jnp.zeros_like(qc))
            outs.append(attend(qm, [kc], [vc], [None]))
        y_ref[0, 0:CTX_LEN, :] = jnp.where(head_mask[0], outs[0], outs[1]).astype(BF16)


def _na_call(u, q_gain, k_gain, bias_tab, need_ctx):
    bsz, lt, _ = u.shape
    lo = lt if need_ctx else lt - CTX_LEN
    npair = NA_HEADS // 2
    c0 = U_NA // 128
    qg = jnp.tile(q_gain.astype(F32), 2).reshape(1, 128)
    kg = jnp.tile(k_gain.astype(F32), 2).reshape(1, 128)
    return pl.pallas_call(
        functools.partial(_na_kernel, need_ctx=need_ctx),
        out_shape=jax.ShapeDtypeStruct((bsz, lo, NA_W), BF16),
        grid=(npair, bsz),
        in_specs=[pl.BlockSpec((1, lt, 128), lambda p, b: (b, 0, c0 + p)),
                  pl.BlockSpec((1, lt, 128), lambda p, b: (b, 0, c0 + npair + p)),
                  pl.BlockSpec((1, lt, 128), lambda p, b: (b, 0, c0 + 2 * npair + p)),
                  pl.BlockSpec((1, 128), lambda p, b: (0, 0)),
                  pl.BlockSpec((1, 128), lambda p, b: (0, 0)),
                  pl.BlockSpec((1, 2, WIN_R, GRID_W, WIN_R * GRID_W), lambda p, b: (p, 0, 0, 0, 0))],
        out_specs=pl.BlockSpec((1, lo, 128), lambda p, b: (b, 0, p)),
        scratch_shapes=[pltpu.VMEM((lt, 128), BF16), pltpu.VMEM((lt, 128), BF16)],
        compiler_params=_cparams(2),
        name="na",
    )(u, u, u, qg, kg, bias_tab)


def _na_bias_table(rpb):
    q = np.arange(GRID_W)[:, None]
    kc = np.arange(GRID_W)[None, :]
    wstart = np.clip(q - WIN_C // 2, 0, GRID_W - WIN_C)
    ok = (kc >= wstart) & (kc < wstart + WIN_C)
    dc = np.clip(kc - q, 1 - WIN_C, WIN_C - 1) + (WIN_C - 1)
    dr = np.arange(WIN_R)[:, None] + np.arange(WIN_R)[None, :]
    t = rpb.astype(F32)[:, dr[:, :, None, None], dc[None, None, :, :]]
    t = jnp.where(ok[None, None, None], t, NEG)
    t = jnp.transpose(t, (0, 1, 3, 2, 4)).reshape(NA_HEADS, WIN_R, GRID_W, WIN_R * GRID_W)
    return t.reshape(NA_HEADS // 2, 2, WIN_R, GRID_W, WIN_R * GRID_W)


def _rope_tables(n_lat):
    pos = np.arange(n_lat)
    quarter = DN_DK // 4
    inv = ROPE_THETA ** (-np.arange(quarter, dtype=np.float64) / quarter)
    lane = np.arange(128)
    d = lane % DN_DK
    p = np.where((d < DN_DK // 2)[None, :], (pos // GRID_W)[:, None], (pos % GRID_W)[:, None])
    ang = p * inv[d % quarter][None, :]
    first = ((d % (DN_DK // 2)) < quarter)[None, :]
    cos = np.cos(ang)
    sin_up = np.where(first, -np.sin(ang), 0.0)
    sin_dn = np.where(first, 0.0, np.sin(ang))
    return (jnp.asarray(cos, F32), jnp.asarray(sin_up, F32), jnp.asarray(sin_dn, F32))


def _conv_kernel(prev_ref, cur_ref, next_ref, w_ref, b_ref, cos_ref, sup_ref, sdn_ref, o_ref, win_s, *, gdn):
    t = pl.program_id(1)
    nt = pl.num_programs(1)
    n_c = cur_ref.shape[2]
    pad = 8
    prev_ok = t >= 2
    next_ok = jnp.logical_and(t >= 1, t < nt - 1)
    pv = prev_ref[0].astype(F32)[HALO - pad:HALO, :]
    nx = next_ref[0].astype(F32)[0:pad, :]
    win_s[0:pad, :] = jnp.where(prev_ok, pv, jnp.zeros_like(pv))
    win_s[pad + TM:pad + TM + pad, :] = jnp.where(next_ok, nx, jnp.zeros_like(nx))
    win_s[pad:pad + TM, :] = cur_ref[0].astype(F32)

    sub = 64
    cb = 512
    wrows = sub + 2 * pad
    li = lax.broadcasted_iota(jnp.int32, (128, 128), 0) // DN_DK
    lj = lax.broadcasted_iota(jnp.int32, (128, 128), 1) // DN_DK
    bd = (li == lj).astype(BF16)
    is_lat = t >= 1
    for s in range(TM // sub):
        for c in range(n_c // cb):
            win = win_s[s * sub:s * sub + wrows, c * cb:(c + 1) * cb]
            acc = None
            for k in range(CONV_W):
                sh = (CONV_W // 2 - k) % wrows
                rolled = win if sh == 0 else pltpu.roll(win, shift=sh, axis=0)
                term = rolled[pad:pad + sub, :] * w_ref[k:k + 1, c * cb:(c + 1) * cb]
                acc = term if acc is None else acc + term
            acc = acc + b_ref[:, c * cb:(c + 1) * cb]
            acc = _silu(acc)
            if gdn and c < 2:
                pieces = []
                for j in range(cb // 128):
                    x = acc[:, j * 128:(j + 1) * 128]
                    xx = x * x
                    hi = xx.astype(BF16)
                    lo = (xx - hi.astype(F32)).astype(BF16)
                    ss = _dot(hi, bd) + _dot(lo, bd)
                    x = x * lax.rsqrt(ss + EPS)
                    cos = cos_ref[s * sub:(s + 1) * sub, :]
                    sup = sup_ref[s * sub:(s + 1) * sub, :]
                    sdn = sdn_ref[s * sub:(s + 1) * sub, :]
                    quarter = DN_DK // 4
                    xr = (x * cos + pltpu.roll(x, shift=128 - quarter, axis=1) * sup
                          + pltpu.roll(x, shift=quarter, axis=1) * sdn)
                    x = jnp.where(is_lat, xr, x)
                    if c == 0:
                        x = x * (DN_DK ** -0.5)
                    pieces.append(x)
                acc = jnp.concatenate(pieces, axis=1)
            o_ref[0, s * sub:(s + 1) * sub, c * cb:(c + 1) * cb] = acc.astype(BF16)


def _conv_call(u, col0, conv_w, conv_b, rope, gdn):
    bsz, lt, _ = u.shape
    nt = lt // TM
    width = 1536
    cblk = col0 // width
    hb = TM // HALO
    nhb = lt // HALO
    cos, sup, sdn = rope
    tab_spec = pl.BlockSpec((TM, 128), lambda b, t: (jnp.maximum(t - 1, 0), 0))
    return pl.pallas_call(
        functools.partial(_conv_kernel, gdn=gdn),
        out_shape=jax.ShapeDtypeStruct((bsz, lt, width), BF16),
        grid=(bsz, nt),
        in_specs=[pl.BlockSpec((1, HALO, width), lambda b, t: (b, jnp.maximum(t * hb - 1, 0), cblk)),
                  pl.BlockSpec((1, TM, width), lambda b, t: (b, t, cblk)),
                  pl.BlockSpec((1, HALO, width), lambda b, t: (b, jnp.minimum((t + 1) * hb, nhb - 1), cblk)),
                  pl.BlockSpec((CONV_W, width), lambda b, t: (0, 0)),
                  pl.BlockSpec((1, width), lambda b, t: (0, 0)),
                  tab_spec, tab_spec, tab_spec],
        out_specs=pl.BlockSpec((1, TM, width), lambda b, t: (b, t, 0)),
        scratch_shapes=[pltpu.VMEM((TM + 16, width), F32)],
        compiler_params=_cparams(2),
        name="conv_gdn" if gdn else "conv_ssd",
    )(u, u, u, conv_w.astype(F32), conv_b.astype(F32).reshape(1, width), cos, sup, sdn)


def _tri_masks(rev):
    ii = lax.broadcasted_iota(jnp.int32, (CHUNK, CHUNK), 0)
    jj = lax.broadcasted_iota(jnp.int32, (CHUNK, CHUNK), 1)
    if rev:
        return ii <= jj, ii < jj, ii == jj
    return ii >= jj, ii > jj, ii == jj


def _col(x, lane):
    return jnp.broadcast_to(x[:, lane:lane + 1], (x.shape[0], CHUNK))


def _row_form(gc, eye):
    return jnp.sum(jnp.where(eye, gc, 0.0), axis=0, keepdims=True)


def _unit_tri_inverse(a, eye_f):
    ii = lax.broadcasted_iota(jnp.int32, (CHUNK, CHUNK), 0)
    jj = lax.broadcasted_iota(jnp.int32, (CHUNK, CHUNK), 1)
    t = eye_f - jnp.where((ii >> 1) == (jj >> 1), a, 0.0)
    lev = 1
    while (1 << lev) < CHUNK:
        pair = jnp.logical_and((ii >> lev) != (jj >> lev), (ii >> (lev + 1)) == (jj >> (lev + 1)))
        lo = jnp.where(pair, a, 0.0).astype(BF16)
        tb = t.astype(BF16)
        t = t - _dot(_dot(tb, lo).astype(BF16), tb)
        lev += 1
    return t


def _chunk_order(c_lo, c_hi, rev, body):
    n = c_hi - c_lo

    def step(i, carry):
        c = (c_hi - 1 - i) if rev else (c_lo + i)
        body(c)
        return carry

    lax.fori_loop(0, n, step, 0)


def _gdn_scan_kernel(qkv_ref, z_ref, us_ref, alog_ref, dtb_ref, og_ref, y_ref, feat_s, o_s, st_s, *, need_ctx):
    lt = qkv_ref.shape[1]
    n_chunks = lt // CHUNK
    n_ctx = CTX_LEN // CHUNK
    off = 0 if need_ctx else CTX_LEN

    u = us_ref[0]
    lane = lax.broadcasted_iota(jnp.int32, (1, SMALL_W), 1)
    beta = _sigmoid(u)
    g = -jnp.exp(alog_ref[...]) * _softplus(u + dtb_ref[...])
    feat_s[...] = jnp.where(lane < S_DECAY, beta, g)

    def run(rev, c_lo, c_hi, with_out, final):
        incl, strict, eye = _tri_masks(rev)
        tri = incl.astype(F32)
        eye_f = eye.astype(F32)
        d = 1 if rev else 0
        last = 0 if rev else CHUNK - 1

        def body(c):
            r0 = pl.multiple_of(c * CHUNK, CHUNK)
            f = feat_s[pl.ds(r0, CHUNK), :]
            gam = _dot_exact(tri, f)
            outs = []
            q_all = qkv_ref[0, pl.ds(r0, CHUNK), 0:DN_KW]
            k_all = qkv_ref[0, pl.ds(r0, CHUNK), DN_KW:2 * DN_KW].astype(F32)
            v_all = qkv_ref[0, pl.ds(r0, CHUNK), 2 * DN_KW:3 * DN_KW].astype(F32)
            for h in range(DN_HEADS):
                k = k_all[:, h * DN_DK:(h + 1) * DN_DK]
                v = v_all[:, h * DN_DK:(h + 1) * DN_DK]
                bcol = _col(f, S_BETA + d * DN_HEADS + h)
                gc = _col(gam, S_DECAY + d * DN_HEADS + h)
                gr = _row_form(gc, eye)
                dec = jnp.exp(jnp.where(incl, gc - gr, NEG))
                kb = k * bcol
                kbf = k.astype(BF16)
                a = jnp.where(strict, _dot_nt(kb.astype(BF16), kbf) * dec, 0.0)
                tb = _unit_tri_inverse(a, eye_f).astype(BF16)
                egam = jnp.exp(gc)
                uu = _dot(tb, (v * bcol).astype(BF16))
                ww = _dot(tb, (kb * egam).astype(BF16))
                glast = gc[last:last + 1, :]
                kdec = k * jnp.exp(glast - gc)
                s = st_s[d, h]
                sb = s.astype(BF16)
                v_new = uu - _dot(ww.astype(BF16), sb)
                vnb = v_new.astype(BF16)
                st_s[d, h] = s * jnp.exp(glast) + _dot(_transpose_bf16(kdec.astype(BF16)), vnb)
                if with_out:
                    q = q_all[:, h * DN_DK:(h + 1) * DN_DK]
                    qk = _dot_nt(q, kbf) * dec
                    o = _dot((q.astype(F32) * egam).astype(BF16), sb) + _dot(qk.astype(BF16), vnb)
                    outs.append(o)
            if with_out:
                ro = pl.multiple_of(r0 - off, CHUNK)
                o_all = jnp.concatenate(outs, axis=1)
                if not final:
                    o_s[pl.ds(ro, CHUNK), :] = o_all
                else:
                    o_all = o_all + o_s[pl.ds(ro, CHUNK), :]
                    z = z_ref[0, pl.ds(r0, CHUNK), :].astype(F32)
                    ys = []
                    for h in range(DN_HEADS):
                        oh = o_all[:, h * DN_DK:(h + 1) * DN_DK]
                        ms = jnp.mean(oh * oh, axis=-1, keepdims=True)
                        ys.append(oh * lax.rsqrt(ms + EPS) * og_ref[...])
                    y = jnp.concatenate(ys, axis=1) * _silu(z)
                    y_ref[0, pl.ds(ro, CHUNK), :] = y.astype(BF16)

        _chunk_order(c_lo, c_hi, rev, body)

    for d, rev in ((0, False), (1, True)):
        st_s[d] = jnp.zeros((DN_HEADS, DN_DK, DN_DK), F32)
        run(rev, 0, n_ctx, need_ctx, rev)
        run(rev, n_ctx, n_chunks, True, rev)


def _gdn_scan_call(qkvn, u, us, alog_pad, dtb_pad, o_gain, need_ctx):
    bsz, lt, _ = qkvn.shape
    lo = lt if need_ctx else lt - CTX_LEN
    return pl.pallas_call(
        functools.partial(_gdn_scan_kernel, need_ctx=need_ctx),
        out_shape=jax.ShapeDtypeStruct((bsz, lo, DN_VW), BF16),
        grid=(bsz,),
        in_specs=[pl.BlockSpec((1, lt, 3 * DN_KW), lambda b: (b, 0, 0)),
                  pl.BlockSpec((1, lt, DN_VW), lambda b: (b, 0, U_DNZ // DN_VW)),
                  pl.BlockSpec((1, lt, SMALL_W), lambda b: (b, 0, 0)),
                  pl.BlockSpec((1, SMALL_W), lambda b: (0, 0)),
                  pl.BlockSpec((1, SMALL_W), lambda b: (0, 0)),
                  pl.BlockSpec((1, DN_DK), lambda b: (0, 0))],
        out_specs=pl.BlockSpec((1, lo, DN_VW), lambda b: (b, 0, 0)),
        scratch_shapes=[pltpu.VMEM((lt, SMALL_W), F32),
                        pltpu.VMEM((lo, DN_VW), F32),
                        pltpu.VMEM((2, DN_HEADS, DN_DK, DN_DK), F32)],
        compiler_params=_cparams(1),
        name="gdn_scan",
    )(qkvn, u, us, alog_pad, dtb_pad, o_gain.astype(F32).reshape(1, DN_DK))


def _ssd_scan_kernel(xbc_ref, z_ref, us_ref, alog_ref, dtb_ref, dskip_ref, og_ref, y_ref,
                     dt_s, la_s, y_s, st_s, *, need_ctx):
    lt = xbc_ref.shape[1]
    n_chunks = lt // CHUNK
    n_ctx = CTX_LEN // CHUNK
    off = 0 if need_ctx else CTX_LEN
    hpg = SSD_HEADS // SSD_GROUPS
    gw = SSD_DI // SSD_GROUPS

    dt = _softplus(us_ref[0] + dtb_ref[...])
    dt_s[...] = dt
    la_s[...] = dt * (-jnp.exp(alog_ref[...]))

    def run(rev, c_lo, c_hi, with_out, final):
        incl, _, eye = _tri_masks(rev)
        tri = incl.astype(F32)
        d = 1 if rev else 0
        last = 0 if rev else CHUNK - 1

        def body(c):
            r0 = pl.multiple_of(c * CHUNK, CHUNK)
            lam = _dot_exact(tri, la_s[pl.ds(r0, CHUNK), :])
            dtc = dt_s[pl.ds(r0, CHUNK), :]
            x_all = xbc_ref[0, pl.ds(r0, CHUNK), 0:SSD_DI].astype(F32)
            outs = []
            for gi in range(SSD_GROUPS):
                bg = xbc_ref[0, pl.ds(r0, CHUNK), SSD_DI + gi * SSD_N:SSD_DI + (gi + 1) * SSD_N]
                cg = xbc_ref[0, pl.ds(r0, CHUNK),
                             SSD_DI + SSD_BC + gi * SSD_N:SSD_DI + SSD_BC + (gi + 1) * SSD_N]
                bgt = _transpose_bf16(bg)
                if with_out:
                    cb = _dot_nt(cg, bg)
                for r in range(hpg):
                    h = gi * hpg + r
                    ln = S_DT + d * SSD_HEADS + h
                    xh = x_all[:, h * SSD_P:(h + 1) * SSD_P]
                    gc = _col(lam, ln)
                    xdt = xh * _col(dtc, ln)
                    glast = gc[last:last + 1, :]
                    ht = st_s[d, h]
                    xdec = xdt * jnp.exp(glast - gc)
                    st_s[d, h] = ht * jnp.exp(glast) + _dot(bgt, xdec.astype(BF16))
                    if with_out:
                        gr = _row_form(gc, eye)
                        dec = jnp.exp(jnp.where(incl, gc - gr, NEG))
                        y = (_dot((cb * dec).astype(BF16), xdt.astype(BF16))
                             + _dot(cg, ht.astype(BF16)) * jnp.exp(gc))
                        if final:
                            y = y + xh * dskip_ref[...][:, h * SSD_P:(h + 1) * SSD_P]
                        outs.append(y)
            if with_out:
                ro = pl.multiple_of(r0 - off, CHUNK)
                y_all = jnp.concatenate(outs, axis=1)
                if not final:
                    y_s[pl.ds(ro, CHUNK), :] = y_all
                else:
                    y_all = (y_all + y_s[pl.ds(ro, CHUNK), :]) * _silu(z_ref[0, pl.ds(r0, CHUNK), :].astype(F32))
                    ys = []
                    for gi in range(SSD_GROUPS):
                        yg = y_all[:, gi * gw:(gi + 1) * gw]
                        ms = jnp.mean(yg * yg, axis=-1, keepdims=True)
                        ys.append(yg * lax.rsqrt(ms + EPS) * og_ref[:, gi * gw:(gi + 1) * gw])
                    y_ref[0, pl.ds(ro, CHUNK), :] = jnp.concatenate(ys, axis=1).astype(BF16)

        _chunk_order(c_lo, c_hi, rev, body)

    for d, rev in ((0, False), (1, True)):
        st_s[d] = jnp.zeros((SSD_HEADS, SSD_N, SSD_P), F32)
        run(rev, 0, n_ctx, need_ctx, rev)
        run(rev, n_ctx, n_chunks, True, rev)


def _ssd_scan_call(xbcn, u, us, alog_pad, dtb_pad, d_skip, o_gain, need_ctx):
    bsz, lt, _ = xbcn.shape
    lo = lt if need_ctx else lt - CTX_LEN
    dskip = jnp.repeat(d_skip.astype(F32), SSD_P).reshape(1, SSD_DI)
    return pl.pallas_call(
        functools.partial(_ssd_scan_kernel, need_ctx=need_ctx),
        out_shape=jax.ShapeDtypeStruct((bsz, lo, SSD_DI), BF16),
        grid=(bsz,),
        in_specs=[pl.BlockSpec((1, lt, SSD_XBC), lambda b: (b, 0, 0)),
                  pl.BlockSpec((1, lt, SSD_DI), lambda b: (b, 0, U_SSDZ // SSD_DI)),
                  pl.BlockSpec((1, lt, SMALL_W), lambda b: (b, 0, 0)),
                  pl.BlockSpec((1, SMALL_W), lambda b: (0, 0)),
                  pl.BlockSpec((1, SMALL_W), lambda b: (0, 0)),
                  pl.BlockSpec((1, SSD_DI), lambda b: (0, 0)),
                  pl.BlockSpec((1, SSD_DI), lambda b: (0, 0))],
        out_specs=pl.BlockSpec((1, lo, SSD_DI), lambda b: (b, 0, 0)),
        scratch_shapes=[pltpu.VMEM((lt, SMALL_W), F32),
                        pltpu.VMEM((lt, SMALL_W), F32),
                        pltpu.VMEM((lo, SSD_DI), F32),
                        pltpu.VMEM((2, SSD_HEADS, SSD_N, SSD_P), F32)],
        compiler_params=_cparams(1),
        name="ssd_scan",
    )(xbcn, u, us, alog_pad, dtb_pad, dskip, o_gain.astype(F32).reshape(1, SSD_DI))


def _merge_kernel(ya_ref, yb_ref, yc_ref, gt_ref, x_ref, gate_ref, wpa_ref, wpb_ref, wpc_ref, wout_ref, o_ref):
    g = _sigmoid(gt_ref[0].astype(F32))
    m = (g[:, 0:D_MODEL] * _dot(ya_ref[0], wpa_ref[...])
         + g[:, D_MODEL:2 * D_MODEL] * _dot(yb_ref[0], wpb_ref[...])
         + g[:, 2 * D_MODEL:3 * D_MODEL] * _dot(yc_ref[0], wpc_ref[...]))
    y = _dot(m.astype(BF16), wout_ref[...])
    o_ref[0] = x_ref[0] + gate_ref[0] * y


def _merge_call(ya, yb, yc, u, xs, mod3, wpa, wpb, wpc, wout, need_ctx):
    bsz, lt, _ = xs.shape
    nt = lt // TM
    ntl = nt if need_ctx else nt - 1
    t0 = 0 if need_ctx else 1

    def row(b, t):
        return _mod_row(b, t + t0, bsz)

    full = lambda b, t: (0, 0)
    return pl.pallas_call(
        _merge_kernel,
        out_shape=jax.ShapeDtypeStruct((bsz, ntl * TM, D_MODEL), F32),
        grid=(bsz, ntl),
        in_specs=[pl.BlockSpec((1, TM, NA_W), lambda b, t: (b, t, 0)),
                  pl.BlockSpec((1, TM, DN_VW), lambda b, t: (b, t, 0)),
                  pl.BlockSpec((1, TM, SSD_DI), lambda b, t: (b, t, 0)),
                  pl.BlockSpec((1, TM, 3 * D_MODEL), lambda b, t: (b, t + t0, U_GATE // (3 * D_MODEL))),
                  pl.BlockSpec((1, TM, D_MODEL), lambda b, t: (b, t + t0, 0)),
                  pl.BlockSpec((1, 1, D_MODEL), lambda b, t: (row(b, t), 0, 2)),
                  pl.BlockSpec((NA_W, D_MODEL), full),
                  pl.BlockSpec((DN_VW, D_MODEL), full),
                  pl.BlockSpec((SSD_DI, D_MODEL), full),
                  pl.BlockSpec((D_MODEL, D_MODEL), full)],
        out_specs=pl.BlockSpec((1, TM, D_MODEL), lambda b, t: (b, t, 0)),
        compiler_params=_cparams(2),
        name="merge",
    )(ya, yb, yc, u, xs, mod3, wpa, wpb, wpc, wout)


def _mlp_kernel(x_ref, shift_ref, scale_ref, gate_ref, g_ref, w1_ref, w2_ref, o_ref):
    x = x_ref[0]
    h = _norm_mod(x, g_ref[...], shift_ref[0], scale_ref[0]).astype(BF16)
    fc = 1024
    acc = None
    for c in range(D_FF // fc):
        a = jnp.maximum(_dot(h, w1_ref[:, c * fc:(c + 1) * fc]), 0.0)
        part = _dot((a * a).astype(BF16), w2_ref[c * fc:(c + 1) * fc, :])
        acc = part if acc is None else acc + part
    o_ref[0] = x + gate_ref[0] * acc


def _mlp_call(x1, mod3, g, w1, w2, need_ctx):
    bsz, lt, _ = x1.shape
    nt = lt // TM
    t0 = 0 if need_ctx else 1

    def row(b, t):
        return _mod_row(b, t + t0, bsz)

    full = lambda b, t: (0, 0)
    return pl.pallas_call(
        _mlp_kernel,
        out_shape=jax.ShapeDtypeStruct((bsz, lt, D_MODEL), F32),
        grid=(bsz, nt),
        in_specs=[pl.BlockSpec((1, TM, D_MODEL), lambda b, t: (b, t, 0)),
                  pl.BlockSpec((1, 1, D_MODEL), lambda b, t: (row(b, t), 0, 3)),
                  pl.BlockSpec((1, 1, D_MODEL), lambda b, t: (row(b, t), 0, 4)),
                  pl.BlockSpec((1, 1, D_MODEL), lambda b, t: (row(b, t), 0, 5)),
                  pl.BlockSpec((1, D_MODEL), full),
                  pl.BlockSpec((D_MODEL, D_FF), full),
                  pl.BlockSpec((D_FF, D_MODEL), full)],
        out_specs=pl.BlockSpec((1, TM, D_MODEL), lambda b, t: (b, t, 0)),
        compiler_params=_cparams(2),
        name="mlp",
    )(x1, mod3, mod3, mod3, g, w1, w2)


def _pad_lanes(pieces):
    row = jnp.zeros((SMALL_W,), F32)
    for o, v in pieces:
        row = lax.dynamic_update_slice(row, v.astype(F32).reshape(-1), (o,))
    return row.reshape(1, SMALL_W)


def _split_w_in(w):
    big = jnp.concatenate([w[:, _O_NA:_O_DNQKV], w[:, _O_DNQKV:_O_DNZ], w[:, _O_XBC:_O_SSDDT],
                           w[:, _O_DNZ:_O_DNB], w[:, _O_SSDZ:_O_XBC], w[:, _O_GATE:_O_END]], axis=1)
    small = jnp.concatenate([w[:, _O_DNB:_O_SSDZ], w[:, _O_SSDDT:_O_GATE],
                             jnp.zeros((w.shape[0], SMALL_W - 4 * DN_HEADS - 2 * SSD_HEADS), w.dtype)], axis=1)
    return big.astype(BF16), small.astype(BF16)


def kernel(x, c, ctx, c_ctx, w_ada, b_ada, norm1_g, norm2_g, w_in, na_q_gain, na_k_gain, na_rpb,
           dn_conv_w, dn_a_log, dn_dt_bias, dn_o_gain, ssd_conv_w, ssd_conv_b, ssd_a_log,
           ssd_dt_bias, ssd_d, ssd_o_gain, w_pa, w_pb, w_pc, w_out, w_ff1, w_ff2):
    bsz, seq, _ = x.shape
    assert bsz < MOD_ROWS and seq % TM == 0 and ctx.shape[1] == CTX_LEN
    n_l = w_ada.shape[0]
    cs = jnp.concatenate([c, c_ctx[None, :], jnp.zeros((MOD_ROWS - bsz - 1, D_MODEL), F32)], axis=0)
    mod = _ada_call(cs, w_ada, b_ada)
    xs = jnp.concatenate([ctx, x], axis=1)
    rope = _rope_tables(seq)
    zeros_w = jnp.zeros((1, 1536), F32)

    for l in range(n_l):
        need_ctx = l < n_l - 1
        mod3 = mod[l].reshape(MOD_ROWS, 1, 6 * D_MODEL)
        wb, ws = _split_w_in(w_in[l])
        u, us = _inproj_call(xs, mod3, norm1_g[l].reshape(1, D_MODEL), wb, ws)

        ya = _na_call(u, na_q_gain[l], na_k_gain[l], _na_bias_table(na_rpb[l]), need_ctx)

        qkvn = _conv_call(u, U_DN, dn_conv_w[l], zeros_w, rope, True)
        alog_dn = _pad_lanes([(S_DECAY, dn_a_log[l])])
        dtb_dn = _pad_lanes([(S_DECAY, dn_dt_bias[l])])
        yb = _gdn_scan_call(qkvn, u, us, alog_dn, dtb_dn, dn_o_gain[l], need_ctx)

        xbcn = _conv_call(u, U_XBC, ssd_conv_w[l], ssd_conv_b[l], rope, False)
        alog_ssd = _pad_lanes([(S_DT, ssd_a_log[l])])
        dtb_ssd = _pad_lanes([(S_DT, ssd_dt_bias[l])])
        yc = _ssd_scan_call(xbcn, u, us, alog_ssd, dtb_ssd, ssd_d[l], ssd_o_gain[l], need_ctx)

        x1 = _merge_call(ya, yb, yc, u, xs, mod3, w_pa[l].astype(BF16), w_pb[l].astype(BF16),
                         w_pc[l].astype(BF16), w_out[l].astype(BF16), need_ctx)
        xs = _mlp_call(x1, mod3, norm2_g[l].reshape(1, D_MODEL), w_ff1[l].astype(BF16),
                       w_ff2[l].astype(BF16), need_ctx)
    return xs
```

```python
import functools
import math

import numpy as np
import jax
import jax.numpy as jnp
from jax import lax
from jax.experimental import pallas as pl
from jax.experimental.pallas import tpu as pltpu

F32 = jnp.float32
BF16 = jnp.bfloat16

D_MODEL = 1024
DEPTH = 2
GRID_W = 64
CTX_LEN = 256
NA_HEADS = 8
NA_DIM = 64
NA_W = NA_HEADS * NA_DIM
WIN_R = 8
WIN_C = 16
DN_HEADS = 8
DN_DK = 64
DN_KW = DN_HEADS * DN_DK
DN_VW = DN_HEADS * DN_DK
SSD_HEADS = 16
SSD_P = 64
SSD_GROUPS = 2
SSD_N = 128
SSD_DI = SSD_HEADS * SSD_P
SSD_BC = SSD_GROUPS * SSD_N
SSD_XBC = SSD_DI + 2 * SSD_BC
CONV_W = 5
D_FF = 4 * D_MODEL
ROPE_THETA = 10000.0
EPS = 1e-6
NEG = -1e30

CHUNK = 64
TM = 256
HALO = 16
MOD_ROWS = 16
SMALL_W = 128

_O_NA = 0
_O_DNQKV = _O_NA + 3 * NA_W
_O_DNZ = _O_DNQKV + 2 * DN_KW + DN_VW
_O_DNB = _O_DNZ + DN_VW
_O_DNA = _O_DNB + 2 * DN_HEADS
_O_SSDZ = _O_DNA + 2 * DN_HEADS
_O_XBC = _O_SSDZ + SSD_DI
_O_SSDDT = _O_XBC + SSD_XBC
_O_GATE = _O_SSDDT + 2 * SSD_HEADS
_O_END = _O_GATE + 3 * D_MODEL

U_NA, U_DN, U_XBC, U_DNZ, U_SSDZ, U_GATE = 0, 1536, 3072, 4608, 5120, 6144
U_W = 9216
S_BETA, S_DECAY, S_DT = 0, 16, 32

VMEM_LIMIT = 56 * 1024 * 1024


def _cparams(n_axes):
    return pltpu.CompilerParams(dimension_semantics=("arbitrary",) * n_axes,
                                vmem_limit_bytes=VMEM_LIMIT)


def _dot(a, b):
    return jnp.dot(a, b, preferred_element_type=F32)


def _dot_nt(a, b):
    return lax.dot_general(a, b, (((1,), (1,)), ((), ())), preferred_element_type=F32)


def _dot_exact(a, b):
    return jnp.dot(a, b, preferred_element_type=F32, precision=lax.Precision.HIGHEST)


def _transpose_bf16(a):
    m = a.shape[1]
    eye = (lax.broadcasted_iota(jnp.int32, (m, m), 0) == lax.broadcasted_iota(jnp.int32, (m, m), 1))
    return _dot_nt(eye.astype(BF16), a).astype(BF16)


def _softplus(x):
    return jnp.maximum(x, 0.0) + jnp.log(1.0 + jnp.exp(-jnp.abs(x)))


def _sigmoid(x):
    return 1.0 / (1.0 + jnp.exp(-x))


def _silu(x):
    return x * _sigmoid(x)


def _ada_kernel(c_ref, w_ref, b_ref, o_ref):
    c = c_ref[...]
    a = _silu(c).astype(BF16)
    o_ref[0] = _dot(a, w_ref[0].astype(BF16)) + b_ref[0]


def _ada_call(cs, w_ada, b_ada):
    n_l = w_ada.shape[0]
    tn = 1536
    return pl.pallas_call(
        _ada_kernel,
        out_shape=jax.ShapeDtypeStruct((n_l, MOD_ROWS, 6 * D_MODEL), F32),
        grid=(n_l, 6 * D_MODEL // tn),
        in_specs=[pl.BlockSpec((MOD_ROWS, D_MODEL), lambda l, n: (0, 0)),
                  pl.BlockSpec((1, D_MODEL, tn), lambda l, n: (l, 0, n)),
                  pl.BlockSpec((1, 1, tn), lambda l, n: (l, 0, n))],
        out_specs=pl.BlockSpec((1, MOD_ROWS, tn), lambda l, n: (l, 0, n)),
        compiler_params=_cparams(2),
        name="ada",
    )(cs, w_ada, b_ada.reshape(n_l, 1, 6 * D_MODEL))


def _mod_row(b, t, bsz):
    return jnp.where(t == 0, bsz, b)


def _norm_mod(x, g, shift, scale):
    ms = jnp.mean(x * x, axis=-1, keepdims=True)
    y = x * lax.rsqrt(ms + EPS) * g
    return y * (1.0 + scale) + shift


def _inproj_kernel(x_ref, shift_ref, scale_ref, g_ref, wb_ref, ws_ref, u_ref, us_ref):
    h = _norm_mod(x_ref[0], g_ref[...], shift_ref[0], scale_ref[0]).astype(BF16)
    cw = 1536
    for c in range(U_W // cw):
        u_ref[0, :, c * cw:(c + 1) * cw] = _dot(h, wb_ref[:, c * cw:(c + 1) * cw]).astype(BF16)
    us_ref[0] = _dot(h, ws_ref[...])


def _inproj_call(xs, mod3, g, wb, ws):
    bsz, lt, _ = xs.shape
    nt = lt // TM
    return pl.pallas_call(
        _inproj_kernel,
        out_shape=(jax.ShapeDtypeStruct((bsz, lt, U_W), BF16),
                   jax.ShapeDtypeStruct((bsz, lt, SMALL_W), F32)),
        grid=(bsz, nt),
        in_specs=[pl.BlockSpec((1, TM, D_MODEL), lambda b, t: (b, t, 0)),
                  pl.BlockSpec((1, 1, D_MODEL), lambda b, t: (_mod_row(b, t, bsz), 0, 0)),
                  pl.BlockSpec((1, 1, D_MODEL), lambda b, t: (_mod_row(b, t, bsz), 0, 1)),
                  pl.BlockSpec((1, D_MODEL), lambda b, t: (0, 0)),
                  pl.BlockSpec((D_MODEL, U_W), lambda b, t: (0, 0)),
                  pl.BlockSpec((D_MODEL, SMALL_W), lambda b, t: (0, 0))],
        out_specs=(pl.BlockSpec((1, TM, U_W), lambda b, t: (b, t, 0)),
                   pl.BlockSpec((1, TM, SMALL_W), lambda b, t: (b, t, 0))),
        compiler_params=_cparams(2),
        name="inproj",
    )(xs, mod3, mod3, g, wb, ws)


def _head_rms(x, gain, bd):
    xx = x * x
    hi = xx.astype(BF16)
    lo = (xx - hi.astype(F32)).astype(BF16)
    ss = _dot(hi, bd) + _dot(lo, bd)
    return x * lax.rsqrt(ss * (1.0 / NA_DIM) + EPS) * gain


def _na_kernel(q_ref, k_ref, v_ref, qg_ref, kg_ref, bias_ref, y_ref, qn_s, kn_s, *, need_ctx):
    lt = q_ref.shape[1]
    rows = (lt - CTX_LEN) // GRID_W
    wr = min(WIN_R, rows)
    off = CTX_LEN if need_ctx else 0
    li = lax.broadcasted_iota(jnp.int32, (128, 128), 0) // NA_DIM
    lj = lax.broadcasted_iota(jnp.int32, (128, 128), 1) // NA_DIM
    bd = (li == lj).astype(BF16)
    scale = NA_DIM ** -0.5

    def norm_tile(i, _):
        r0 = pl.multiple_of(i * TM, TM)
        kn_s[pl.ds(r0, TM), :] = _head_rms(k_ref[0, pl.ds(r0, TM), :].astype(F32), kg_ref[...], bd).astype(BF16)
        qn_s[pl.ds(r0, TM), :] = (_head_rms(q_ref[0, pl.ds(r0, TM), :].astype(F32), qg_ref[...], bd)
                                  * scale).astype(BF16)
        return 0

    lax.fori_loop(0, lt // TM, norm_tile, 0)

    lane = lax.broadcasted_iota(jnp.int32, (1, 128), 1)
    head_mask = [lane < NA_DIM, lane >= NA_DIM]
    def stack_heads(q):
        z = jnp.zeros_like(q)
        return jnp.concatenate([jnp.where(head_mask[0], q, z), jnp.where(head_mask[1], q, z)], axis=0)

    def unstack_heads(o):
        r = o.shape[0] // 2
        return jnp.where(head_mask[0], o[0:r], o[r:])

    def attend(probs):
        ss = [[_dot_nt(p[0], kk) for kk in p[1]] for p in probs]
        ss = [[s if b is None else s + b for s, b in zip(sl, p[3])] for sl, p in zip(ss, probs)]
        ms = [functools.reduce(jnp.maximum, [s.max(axis=-1, keepdims=True) for s in sl]) for sl in ss]
        es = [[jnp.exp(s - m) for s in sl] for sl, m in zip(ss, ms)]
        dens = [functools.reduce(jnp.add, [e.sum(axis=-1, keepdims=True) for e in el]) for el in es]
        accs = [functools.reduce(jnp.add, [_dot(e.astype(BF16), vv) for e, vv in zip(el, p[2])])
                for el, p in zip(es, probs)]
        return [unstack_heads(a * (1.0 / d)) for a, d in zip(accs, dens)]

    rows_per_step = 2

    def row_body(i, _):
        probs = []
        for j in range(rows_per_step):
            r = i * rows_per_step + j
            ws = jnp.clip(r - wr // 2, 0, rows - wr)
            cls = ws - r + (WIN_R - 1)
            q_r = qn_s[pl.ds(pl.multiple_of(CTX_LEN + r * GRID_W, GRID_W), GRID_W), :]
            k0 = pl.multiple_of(CTX_LEN + ws * GRID_W, GRID_W)
            kw = kn_s[pl.ds(k0, wr * GRID_W), :]
            vw = v_ref[0, pl.ds(k0, wr * GRID_W), :]
            bias = jnp.concatenate([bias_ref[0, 0, cls], bias_ref[0, 1, cls]], axis=0)
            probs.append((stack_heads(q_r), [kw, kn_s[0:CTX_LEN, :]], [vw, v_ref[0, 0:CTX_LEN, :]],
                          [bias, None]))
        for j, y in enumerate(attend(probs)):
            r = i * rows_per_step + j
            y_ref[0, pl.ds(pl.multiple_of(off + r * GRID_W, GRID_W), GRID_W), :] = y.astype(BF16)
        return 0

    lax.fori_loop(0, rows // rows_per_step, row_body, 0)

    if need_ctx:
        half = CTX_LEN // 2
        probs = [(stack_heads(qn_s[j * half:(j + 1) * half, :]), [kn_s[0:CTX_LEN, :]], [v_ref[0, 0:CTX_LEN, :]],
                  [None]) for j in range(2)]
        for j, y in enumerate(attend(probs)):
            y_ref[0, j * half:(j + 1) * half, :] = y.astype(BF16)


def _na_call(u, q_gain, k_gain, bias_tab, need_ctx):
    bsz, lt, _ = u.shape
    lo = lt if need_ctx else lt - CTX_LEN
    npair = NA_HEADS // 2
    c0 = U_NA // 128
    qg = jnp.tile(q_gain.astype(F32), 2).reshape(1, 128)
    kg = jnp.tile(k_gain.astype(F32), 2).reshape(1, 128)
    return pl.pallas_call(
        functools.partial(_na_kernel, need_ctx=need_ctx),
        out_shape=jax.ShapeDtypeStruct((bsz, lo, NA_W), BF16),
        grid=(npair, bsz),
        in_specs=[pl.BlockSpec((1, lt, 128), lambda p, b: (b, 0, c0 + p)),
                  pl.BlockSpec((1, lt, 128), lambda p, b: (b, 0, c0 + npair + p)),
                  pl.BlockSpec((1, lt, 128), lambda p, b: (b, 0, c0 + 2 * npair + p)),
                  pl.BlockSpec((1, 128), lambda p, b: (0, 0)),
                  pl.BlockSpec((1, 128), lambda p, b: (0, 0)),
                  pl.BlockSpec((1, 2, WIN_R, GRID_W, WIN_R * GRID_W), lambda p, b: (p, 0, 0, 0, 0))],
        out_specs=pl.BlockSpec((1, lo, 128), lambda p, b: (b, 0, p)),
        scratch_shapes=[pltpu.VMEM((lt, 128), BF16), pltpu.VMEM((lt, 128), BF16)],
        compiler_params=_cparams(2),
        name="na",
    )(u, u, u, qg, kg, bias_tab)


def _na_bias_table(rpb):
    q = np.arange(GRID_W)[:, None]
    kc = np.arange(GRID_W)[None, :]
    wstart = np.clip(q - WIN_C // 2, 0, GRID_W - WIN_C)
    ok = (kc >= wstart) & (kc < wstart + WIN_C)
    dc = np.clip(kc - q, 1 - WIN_C, WIN_C - 1) + (WIN_C - 1)
    onehot = (np.arange(2 * WIN_C - 1)[:, None, None] == dc[None]) & ok[None]
    toep = jnp.einsum('hdc,cqk->hdqk', rpb.astype(F32), jnp.asarray(onehot, F32),
                      precision=lax.Precision.HIGHEST)
    toep = jnp.where(ok[None, None], toep, NEG)
    t = jnp.stack([toep[:, cls:cls + WIN_R] for cls in range(WIN_R)], axis=1)
    t = jnp.transpose(t, (0, 1, 3, 2, 4)).reshape(NA_HEADS, WIN_R, GRID_W, WIN_R * GRID_W)
    return t.reshape(NA_HEADS // 2, 2, WIN_R, GRID_W, WIN_R * GRID_W)


def _rope_tables(n_lat):
    pos = np.arange(n_lat)
    quarter = DN_DK // 4
    inv = ROPE_THETA ** (-np.arange(quarter, dtype=np.float64) / quarter)
    lane = np.arange(128)
    d = lane % DN_DK
    p = np.where((d < DN_DK // 2)[None, :], (pos // GRID_W)[:, None], (pos % GRID_W)[:, None])
    ang = p * inv[d % quarter][None, :]
    first = ((d % (DN_DK // 2)) < quarter)[None, :]
    cos = np.cos(ang)
    sin_up = np.where(first, -np.sin(ang), 0.0)
    sin_dn = np.where(first, 0.0, np.sin(ang))
    return (jnp.asarray(cos, F32), jnp.asarray(sin_up, F32), jnp.asarray(sin_dn, F32))


def _conv_kernel(prev_ref, cur_ref, next_ref, w_ref, b_ref, cos_ref, sup_ref, sdn_ref, o_ref, win_s, *, gdn):
    t = pl.program_id(1)
    nt = pl.num_programs(1)
    n_c = cur_ref.shape[2]
    pad = 8
    prev_ok = t >= 2
    next_ok = jnp.logical_and(t >= 1, t < nt - 1)
    pv = prev_ref[0].astype(F32)[HALO - pad:HALO, :]
    nx = next_ref[0].astype(F32)[0:pad, :]
    win_s[0:pad, :] = jnp.where(prev_ok, pv, jnp.zeros_like(pv))
    win_s[pad + TM:pad + TM + pad, :] = jnp.where(next_ok, nx, jnp.zeros_like(nx))
    win_s[pad:pad + TM, :] = cur_ref[0].astype(F32)

    sub = 64
    cb = 512
    wrows = sub + 2 * pad
    li = lax.broadcasted_iota(jnp.int32, (128, 128), 0) // DN_DK
    lj = lax.broadcasted_iota(jnp.int32, (128, 128), 1) // DN_DK
    bd = (li == lj).astype(BF16)
    is_lat = t >= 1
    for s in range(TM // sub):
        for c in range(n_c // cb):
            win = win_s[s * sub:s * sub + wrows, c * cb:(c + 1) * cb]
            acc = None
            for k in range(CONV_W):
                sh = (CONV_W // 2 - k) % wrows
                rolled = win if sh == 0 else pltpu.roll(win, shift=sh, axis=0)
                term = rolled[pad:pad + sub, :] * w_ref[k:k + 1, c * cb:(c + 1) * cb]
                acc = term if acc is None else acc + term
            acc = acc + b_ref[:, c * cb:(c + 1) * cb]
            acc = _silu(acc)
            if gdn and c < 2:
                pieces = []
                for j in range(cb // 128):
                    x = acc[:, j * 128:(j + 1) * 128]
                    xx = x * x
                    hi = xx.astype(BF16)
                    lo = (xx - hi.astype(F32)).astype(BF16)
                    ss = _dot(hi, bd) + _dot(lo, bd)
                    x = x * lax.rsqrt(ss + EPS)
                    cos = cos_ref[s * sub:(s + 1) * sub, :]
                    sup = sup_ref[s * sub:(s + 1) * sub, :]
                    sdn = sdn_ref[s * sub:(s + 1) * sub, :]
                    quarter = DN_DK // 4
                    xr = (x * cos + pltpu.roll(x, shift=128 - quarter, axis=1) * sup
                          + pltpu.roll(x, shift=quarter, axis=1) * sdn)
                    x = jnp.where(is_lat, xr, x)
                    if c == 0:
                        x = x * (DN_DK ** -0.5)
                    pieces.append(x)
                acc = jnp.concatenate(pieces, axis=1)
            o_ref[0, s * sub:(s + 1) * sub, c * cb:(c + 1) * cb] = acc.astype(BF16)


def _conv_call(u, col0, conv_w, conv_b, rope, gdn):
    bsz, lt, _ = u.shape
    nt = lt // TM
    width = 1536
    cblk = col0 // width
    hb = TM // HALO
    nhb = lt // HALO
    cos, sup, sdn = rope
    tab_spec = pl.BlockSpec((TM, 128), lambda b, t: (jnp.maximum(t - 1, 0), 0))
    return pl.pallas_call(
        functools.partial(_conv_kernel, gdn=gdn),
        out_shape=jax.ShapeDtypeStruct((bsz, lt, width), BF16),
        grid=(bsz, nt),
        in_specs=[pl.BlockSpec((1, HALO, width), lambda b, t: (b, jnp.maximum(t * hb - 1, 0), cblk)),
                  pl.BlockSpec((1, TM, width), lambda b, t: (b, t, cblk)),
                  pl.BlockSpec((1, HALO, width), lambda b, t: (b, jnp.minimum((t + 1) * hb, nhb - 1), cblk)),
                  pl.BlockSpec((CONV_W, width), lambda b, t: (0, 0)),
                  pl.BlockSpec((1, width), lambda b, t: (0, 0)),
                  tab_spec, tab_spec, tab_spec],
        out_specs=pl.BlockSpec((1, TM, width), lambda b, t: (b, t, 0)),
        scratch_shapes=[pltpu.VMEM((TM + 16, width), F32)],
        compiler_params=_cparams(2),
        name="conv_gdn" if gdn else "conv_ssd",
    )(u, u, u, conv_w.astype(F32), conv_b.astype(F32).reshape(1, width), cos, sup, sdn)


PAIR_W = 2 * CHUNK


def _pair_iotas():
    ii = lax.broadcasted_iota(jnp.int32, (CHUNK, PAIR_W), 0)
    jj = lax.broadcasted_iota(jnp.int32, (CHUNK, PAIR_W), 1) & (CHUNK - 1)
    return ii, jj


def _tri_masks(rev):
    ii, jj = _pair_iotas()
    if rev:
        return ii <= jj, ii < jj, ii == jj
    return ii >= jj, ii > jj, ii == jj


def _col_pair(x, lane):
    left = lax.broadcasted_iota(jnp.int32, (1, PAIR_W), 1) < CHUNK
    return jnp.where(left, jnp.broadcast_to(x[:, lane:lane + 1], (CHUNK, PAIR_W)),
                     jnp.broadcast_to(x[:, lane + 1:lane + 2], (CHUNK, PAIR_W)))


def _row_form(gc, eye):
    return jnp.sum(jnp.where(eye, gc, 0.0), axis=0, keepdims=True)


def _bd(y):
    r = lax.broadcasted_iota(jnp.int32, (PAIR_W, PAIR_W), 0) // CHUNK
    c = lax.broadcasted_iota(jnp.int32, (PAIR_W, PAIR_W), 1) // CHUNK
    return jnp.where(r == c, jnp.concatenate([y, y], axis=0), jnp.zeros((PAIR_W, PAIR_W), y.dtype))


def _bd_mask(x):
    r = lax.broadcasted_iota(jnp.int32, (PAIR_W, PAIR_W), 0) // CHUNK
    c = lax.broadcasted_iota(jnp.int32, (PAIR_W, PAIR_W), 1) // CHUNK
    return jnp.where(r == c, x, 0.0)


def _unit_tri_inverse(a_list, eye_f):
    ii, jj = _pair_iotas()
    base = (ii >> 1) == (jj >> 1)
    t = [eye_f - jnp.where(base, a, 0.0) for a in a_list]
    n = range(len(t))
    lev = 1
    while (1 << lev) < CHUNK:
        pair = jnp.logical_and((ii >> lev) != (jj >> lev), (ii >> (lev + 1)) == (jj >> (lev + 1)))
        lo = [_bd(jnp.where(pair, a, 0.0).astype(BF16)) for a in a_list]
        tb = [x.astype(BF16) for x in t]
        x1 = [_dot(tb[i], lo[i]).astype(BF16) for i in n]
        x2 = [_dot(x1[i], _bd(tb[i])) for i in n]
        t = [t[i] - x2[i] for i in n]
        lev += 1
    return t


def _scan_loops(n_ctx, n_chunks, need_ctx, pair_step):
    def run(c_lo, c_hi, with_out):
        def step(i, carry):
            pair_step(c_lo + i, c_hi - 1 - i, with_out)
            return carry

        lax.fori_loop(0, c_hi - c_lo, step, 0)

    run(0, n_ctx, need_ctx)
    run(n_ctx, n_chunks, True)


def _gdn_scan_kernel(qkv_ref, z_ref, us_ref, alog_ref, dtb_ref, og_ref, y_ref, feat_s, o_s, st_s, *, need_ctx):
    lt = qkv_ref.shape[1]
    n_chunks = lt // CHUNK
    n_ctx = CTX_LEN // CHUNK
    off = 0 if need_ctx else CTX_LEN
    lo = lt - off
    n_pairs = DN_HEADS // 2

    u = us_ref[0]
    lane = lax.broadcasted_iota(jnp.int32, (1, SMALL_W), 1)
    beta = _sigmoid(u)
    g = -jnp.exp(alog_ref[...]) * _softplus(u + dtb_ref[...])
    feat_s[...] = jnp.where(lane < S_DECAY, beta, g)
    st_s[...] = jnp.zeros(st_s.shape, F32)
    o_s[...] = jnp.zeros(o_s.shape, F32)

    masks = [_tri_masks(False), _tri_masks(True)]
    ci = lax.broadcasted_iota(jnp.int32, (CHUNK, CHUNK), 0)
    cj = lax.broadcasted_iota(jnp.int32, (CHUNK, CHUNK), 1)
    tri = [(ci >= cj).astype(F32), (ci <= cj).astype(F32)]
    eye = masks[0][2]
    eye_f = eye.astype(F32)
    last = [CHUNK - 1, 0]
    units = [(d, p) for d in range(2) for p in range(n_pairs)]
    nu = range(len(units))

    def pair(x, p):
        return x[:, p * PAIR_W:(p + 1) * PAIR_W]

    def pair_step(cf, cb, with_out):
        r0 = [pl.multiple_of(cf * CHUNK, CHUNK), pl.multiple_of(cb * CHUNK, CHUNK)]
        f = [feat_s[pl.ds(r, CHUNK), :] for r in r0]
        gam = [_dot_exact(tri[d], f[d]) for d in range(2)]
        k_all = [qkv_ref[0, pl.ds(r, CHUNK), DN_KW:2 * DN_KW].astype(F32) for r in r0]
        v_all = [qkv_ref[0, pl.ds(r, CHUNK), 2 * DN_KW:3 * DN_KW].astype(F32) for r in r0]
        k = [pair(k_all[d], p) for d, p in units]
        v = [pair(v_all[d], p) for d, p in units]
        bcol = [_col_pair(f[d], S_BETA + d * DN_HEADS + 2 * p) for d, p in units]
        gc = [_col_pair(gam[d], S_DECAY + d * DN_HEADS + 2 * p) for d, p in units]
        gr = [_row_form(x, eye) for x in gc]
        dec = [jnp.exp(jnp.where(masks[units[i][0]][0], gc[i] - gr[i], NEG)) for i in nu]
        kb = [k[i] * bcol[i] for i in nu]
        kbd = [_bd(x.astype(BF16)) for x in k]
        kk = [_dot_nt(kb[i].astype(BF16), kbd[i]) for i in nu]
        a = [jnp.where(masks[units[i][0]][1], kk[i] * dec[i], 0.0) for i in nu]
        tb = [t.astype(BF16) for t in _unit_tri_inverse(a, eye_f)]
        egam = [jnp.exp(x) for x in gc]
        uu = [_dot(tb[i], _bd((v[i] * bcol[i]).astype(BF16))) for i in nu]
        ww = [_dot(tb[i], _bd((kb[i] * egam[i]).astype(BF16))) for i in nu]
        glast = [gc[i][last[units[i][0]]:last[units[i][0]] + 1, :] for i in nu]
        kdt = [_transpose_bf16((k[i] * jnp.exp(glast[i] - gc[i])).astype(BF16)) for i in nu]
        s = [st_s[d, p] for d, p in units]
        sb = [x.astype(BF16) for x in s]
        pred = [_dot(ww[i].astype(BF16), sb[i]) for i in nu]
        vnb = [(uu[i] - pred[i]).astype(BF16) for i in nu]
        upd = [_bd_mask(_dot(kdt[i], vnb[i])) for i in nu]
        for i, (d, p) in enumerate(units):
            st_s[d, p] = s[i] * jnp.exp(glast[i]) + upd[i]
        if with_out:
            q_all = [qkv_ref[0, pl.ds(r, CHUNK), 0:DN_KW] for r in r0]
            q = [pair(q_all[d], p) for d, p in units]
            qk = [_dot_nt(q[i], kbd[i]) * dec[i] for i in nu]
            o1 = [_dot((q[i].astype(F32) * egam[i]).astype(BF16), sb[i]) for i in nu]
            o2 = [_dot(qk[i].astype(BF16), _bd(vnb[i])) for i in nu]
            for d in range(2):
                o_all = jnp.concatenate([o1[i] + o2[i] for i in nu if units[i][0] == d], axis=1)
                rows = pl.ds(pl.multiple_of(r0[d] - off, CHUNK), CHUNK)
                o_s[rows, :] = o_s[rows, :] + o_all

    _scan_loops(n_ctx, n_chunks, need_ctx, pair_step)

    li = lax.broadcasted_iota(jnp.int32, (PAIR_W, PAIR_W), 0) // DN_DK
    lj = lax.broadcasted_iota(jnp.int32, (PAIR_W, PAIR_W), 1) // DN_DK
    ones_bd = (li == lj).astype(BF16)

    def finalize(t, carry):
        ro = pl.multiple_of(t * CHUNK, CHUNK)
        o_all = o_s[pl.ds(ro, CHUNK), :]
        z = z_ref[0, pl.ds(pl.multiple_of(ro + off, CHUNK), CHUNK), :].astype(F32)
        ys = [_head_rms(pair(o_all, p), og_ref[...], ones_bd) for p in range(n_pairs)]
        y_ref[0, pl.ds(ro, CHUNK), :] = (jnp.concatenate(ys, axis=1) * _silu(z)).astype(BF16)
        return carry

    lax.fori_loop(0, lo // CHUNK, finalize, 0)


def _gdn_scan_call(qkvn, u, us, alog_pad, dtb_pad, o_gain, need_ctx):
    bsz, lt, _ = qkvn.shape
    lo = lt if need_ctx else lt - CTX_LEN
    return pl.pallas_call(
        functools.partial(_gdn_scan_kernel, need_ctx=need_ctx),
        out_shape=jax.ShapeDtypeStruct((bsz, lo, DN_VW), BF16),
        grid=(bsz,),
        in_specs=[pl.BlockSpec((1, lt, 3 * DN_KW), lambda b: (b, 0, 0)),
                  pl.BlockSpec((1, lt, DN_VW), lambda b: (b, 0, U_DNZ // DN_VW)),
                  pl.BlockSpec((1, lt, SMALL_W), lambda b: (b, 0, 0)),
                  pl.BlockSpec((1, SMALL_W), lambda b: (0, 0)),
                  pl.BlockSpec((1, SMALL_W), lambda b: (0, 0)),
                  pl.BlockSpec((1, PAIR_W), lambda b: (0, 0))],
        out_specs=pl.BlockSpec((1, lo, DN_VW), lambda b: (b, 0, 0)),
        scratch_shapes=[pltpu.VMEM((lt, SMALL_W), F32),
                        pltpu.VMEM((lo, DN_VW), F32),
                        pltpu.VMEM((2, DN_HEADS // 2, PAIR_W, PAIR_W), F32)],
        compiler_params=_cparams(1),
        name="gdn_scan",
    )(qkvn, u, us, alog_pad, dtb_pad, jnp.tile(o_gain.astype(F32), 2).reshape(1, PAIR_W))


def _ssd_scan_kernel(xbc_ref, z_ref, us_ref, alog_ref, dtb_ref, dskip_ref, og_ref, y_ref,
                     dt_s, la_s, y_s, st_s, *, need_ctx):
    lt = xbc_ref.shape[1]
    n_chunks = lt // CHUNK
    n_ctx = CTX_LEN // CHUNK
    off = 0 if need_ctx else CTX_LEN
    hpg = SSD_HEADS // SSD_GROUPS
    gw = SSD_DI // SSD_GROUPS

    lo = lt - off
    ppg = hpg // 2

    dt = _softplus(us_ref[0] + dtb_ref[...])
    dt_s[...] = dt
    la_s[...] = dt * (-jnp.exp(alog_ref[...]))
    st_s[...] = jnp.zeros(st_s.shape, F32)
    y_s[...] = jnp.zeros(y_s.shape, F32)

    masks = [_tri_masks(False), _tri_masks(True)]
    ci = lax.broadcasted_iota(jnp.int32, (CHUNK, CHUNK), 0)
    cj = lax.broadcasted_iota(jnp.int32, (CHUNK, CHUNK), 1)
    tri = [(ci >= cj).astype(F32), (ci <= cj).astype(F32)]
    eye = masks[0][2]
    last = [CHUNK - 1, 0]
    groups = [(d, gi) for d in range(2) for gi in range(SSD_GROUPS)]
    units = [(d, gi, r) for d, gi in groups for r in range(ppg)]
    nu = range(len(units))

    def pair_step(cf, cb, with_out):
        r0 = [pl.multiple_of(cf * CHUNK, CHUNK), pl.multiple_of(cb * CHUNK, CHUNK)]
        lam = [_dot_exact(tri[d], la_s[pl.ds(r0[d], CHUNK), :]) for d in range(2)]
        dtc = [dt_s[pl.ds(r, CHUNK), :] for r in r0]
        bg = {(d, gi): xbc_ref[0, pl.ds(r0[d], CHUNK), SSD_DI + gi * SSD_N:SSD_DI + (gi + 1) * SSD_N]
              for d, gi in groups}
        cg = {(d, gi): xbc_ref[0, pl.ds(r0[d], CHUNK),
                               SSD_DI + SSD_BC + gi * SSD_N:SSD_DI + SSD_BC + (gi + 1) * SSD_N]
              for d, gi in groups}
        bgt = {key: _transpose_bf16(bg[key]) for key in groups}
        x = [xbc_ref[0, pl.ds(r0[d], CHUNK), (gi * ppg + r) * PAIR_W:(gi * ppg + r + 1) * PAIR_W].astype(F32)
             for d, gi, r in units]
        ln = [S_DT + d * SSD_HEADS + 2 * (gi * ppg + r) for d, gi, r in units]
        gc = [_col_pair(lam[units[i][0]], ln[i]) for i in nu]
        xdt = [x[i] * _col_pair(dtc[units[i][0]], ln[i]) for i in nu]
        glast = [gc[i][last[units[i][0]]:last[units[i][0]] + 1, :] for i in nu]
        xdec = [(xdt[i] * jnp.exp(glast[i] - gc[i])).astype(BF16) for i in nu]
        ht = [st_s[d, gi * ppg + r] for d, gi, r in units]
        upd = [_dot(bgt[units[i][:2]], xdec[i]) for i in nu]
        for i, (d, gi, r) in enumerate(units):
            st_s[d, gi * ppg + r] = ht[i] * jnp.exp(glast[i]) + upd[i]
        if with_out:
            cb2 = {key: _dot_nt(cg[key], jnp.concatenate([bg[key], bg[key]], axis=0)) for key in groups}
            gr = [_row_form(g_, eye) for g_ in gc]
            dec = [jnp.exp(jnp.where(masks[units[i][0]][0], gc[i] - gr[i], NEG)) for i in nu]
            y1 = [_dot((cb2[units[i][:2]] * dec[i]).astype(BF16), _bd(xdt[i].astype(BF16))) for i in nu]
            y2 = [_dot(cg[units[i][:2]], ht[i].astype(BF16)) * jnp.exp(gc[i]) for i in nu]
            for d in range(2):
                y_all = jnp.concatenate([y1[i] + y2[i] for i in nu if units[i][0] == d], axis=1)
                rows = pl.ds(pl.multiple_of(r0[d] - off, CHUNK), CHUNK)
                y_s[rows, :] = y_s[rows, :] + y_all

    _scan_loops(n_ctx, n_chunks, need_ctx, pair_step)

    def finalize(t, carry):
        ro = pl.multiple_of(t * CHUNK, CHUNK)
        ri = pl.multiple_of(ro + off, CHUNK)
        xs = xbc_ref[0, pl.ds(ri, CHUNK), 0:SSD_DI].astype(F32)
        y_all = (y_s[pl.ds(ro, CHUNK), :] + xs * dskip_ref[...]) * _silu(z_ref[0, pl.ds(ri, CHUNK), :].astype(F32))
        ys = []
        for gi in range(SSD_GROUPS):
            yg = y_all[:, gi * gw:(gi + 1) * gw]
            ms = jnp.mean(yg * yg, axis=-1, keepdims=True)
            ys.append(yg * lax.rsqrt(ms + EPS) * og_ref[:, gi * gw:(gi + 1) * gw])
        y_ref[0, pl.ds(ro, CHUNK), :] = jnp.concatenate(ys, axis=1).astype(BF16)
        return carry

    lax.fori_loop(0, lo // CHUNK, finalize, 0)


def _ssd_scan_call(xbcn, u, us, alog_pad, dtb_pad, d_skip, o_gain, need_ctx):
    bsz, lt, _ = xbcn.shape
    lo = lt if need_ctx else lt - CTX_LEN
    dskip = jnp.repeat(d_skip.astype(F32), SSD_P).reshape(1, SSD_DI)
    return pl.pallas_call(
        functools.partial(_ssd_scan_kernel, need_ctx=need_ctx),
        out_shape=jax.ShapeDtypeStruct((bsz, lo, SSD_DI), BF16),
        grid=(bsz,),
        in_specs=[pl.BlockSpec((1, lt, SSD_XBC), lambda b: (b, 0, 0)),
                  pl.BlockSpec((1, lt, SSD_DI), lambda b: (b, 0, U_SSDZ // SSD_DI)),
                  pl.BlockSpec((1, lt, SMALL_W), lambda b: (b, 0, 0)),
                  pl.BlockSpec((1, SMALL_W), lambda b: (0, 0)),
                  pl.BlockSpec((1, SMALL_W), lambda b: (0, 0)),
                  pl.BlockSpec((1, SSD_DI), lambda b: (0, 0)),
                  pl.BlockSpec((1, SSD_DI), lambda b: (0, 0))],
        out_specs=pl.BlockSpec((1, lo, SSD_DI), lambda b: (b, 0, 0)),
        scratch_shapes=[pltpu.VMEM((lt, SMALL_W), F32),
                        pltpu.VMEM((lt, SMALL_W), F32),
                        pltpu.VMEM((lo, SSD_DI), F32),
                        pltpu.VMEM((2, SSD_HEADS // 2, SSD_N, PAIR_W), F32)],
        compiler_params=_cparams(1),
        name="ssd_scan",
    )(xbcn, u, us, alog_pad, dtb_pad, dskip, o_gain.astype(F32).reshape(1, SSD_DI))


def _merge_kernel(ya_ref, yb_ref, yc_ref, gt_ref, x_ref, gate_ref, wpa_ref, wpb_ref, wpc_ref, wout_ref, o_ref):
    g = _sigmoid(gt_ref[0].astype(F32))
    m = (g[:, 0:D_MODEL] * _dot(ya_ref[0], wpa_ref[...])
         + g[:, D_MODEL:2 * D_MODEL] * _dot(yb_ref[0], wpb_ref[...])
         + g[:, 2 * D_MODEL:3 * D_MODEL] * _dot(yc_ref[0], wpc_ref[...]))
    y = _dot(m.astype(BF16), wout_ref[...])
    o_ref[0] = x_ref[0] + gate_ref[0] * y


def _merge_call(ya, yb, yc, u, xs, mod3, wpa, wpb, wpc, wout, need_ctx):
    bsz, lt, _ = xs.shape
    nt = lt // TM
    ntl = nt if need_ctx else nt - 1
    t0 = 0 if need_ctx else 1

    def row(b, t):
        return _mod_row(b, t + t0, bsz)

    full = lambda b, t: (0, 0)
    return pl.pallas_call(
        _merge_kernel,
        out_shape=jax.ShapeDtypeStruct((bsz, ntl * TM, D_MODEL), F32),
        grid=(bsz, ntl),
        in_specs=[pl.BlockSpec((1, TM, NA_W), lambda b, t: (b, t, 0)),
                  pl.BlockSpec((1, TM, DN_VW), lambda b, t: (b, t, 0)),
                  pl.BlockSpec((1, TM, SSD_DI), lambda b, t: (b, t, 0)),
                  pl.BlockSpec((1, TM, 3 * D_MODEL), lambda b, t: (b, t + t0, U_GATE // (3 * D_MODEL))),
                  pl.BlockSpec((1, TM, D_MODEL), lambda b, t: (b, t + t0, 0)),
                  pl.BlockSpec((1, 1, D_MODEL), lambda b, t: (row(b, t), 0, 2)),
                  pl.BlockSpec((NA_W, D_MODEL), full),
                  pl.BlockSpec((DN_VW, D_MODEL), full),
                  pl.BlockSpec((SSD_DI, D_MODEL), full),
                  pl.BlockSpec((D_MODEL, D_MODEL), full)],
        out_specs=pl.BlockSpec((1, TM, D_MODEL), lambda b, t: (b, t, 0)),
        compiler_params=_cparams(2),
        name="merge",
    )(ya, yb, yc, u, xs, mod3, wpa, wpb, wpc, wout)


def _mlp_kernel(x_ref, shift_ref, scale_ref, gate_ref, g_ref, w1_ref, w2_ref, o_ref):
    x = x_ref[0]
    h = _norm_mod(x, g_ref[...], shift_ref[0], scale_ref[0]).astype(BF16)
    fc = 1024
    acc = None
    for c in range(D_FF // fc):
        a = jnp.maximum(_dot(h, w1_ref[:, c * fc:(c + 1) * fc]), 0.0)
        part = _dot((a * a).astype(BF16), w2_ref[c * fc:(c + 1) * fc, :])
        acc = part if acc is None else acc + part
    o_ref[0] = x + gate_ref[0] * acc


def _mlp_call(x1, mod3, g, w1, w2, need_ctx):
    bsz, lt, _ = x1.shape
    nt = lt // TM
    t0 = 0 if need_ctx else 1

    def row(b, t):
        return _mod_row(b, t + t0, bsz)

    full = lambda b, t: (0, 0)
    return pl.pallas_call(
        _mlp_kernel,
        out_shape=jax.ShapeDtypeStruct((bsz, lt, D_MODEL), F32),
        grid=(bsz, nt),
        in_specs=[pl.BlockSpec((1, TM, D_MODEL), lambda b, t: (b, t, 0)),
                  pl.BlockSpec((1, 1, D_MODEL), lambda b, t: (row(b, t), 0, 3)),
                  pl.BlockSpec((1, 1, D_MODEL), lambda b, t: (row(b, t), 0, 4)),
                  pl.BlockSpec((1, 1, D_MODEL), lambda b, t: (row(b, t), 0, 5)),
                  pl.BlockSpec((1, D_MODEL), full),
                  pl.BlockSpec((D_MODEL, D_FF), full),
                  pl.BlockSpec((D_FF, D_MODEL), full)],
        out_specs=pl.BlockSpec((1, TM, D_MODEL), lambda b, t: (b, t, 0)),
        compiler_params=_cparams(2),
        name="mlp",
    )(x1, mod3, mod3, mod3, g, w1, w2)


def _pad_lanes(pieces):
    row = jnp.zeros((SMALL_W,), F32)
    for o, v in pieces:
        row = lax.dynamic_update_slice(row, v.astype(F32).reshape(-1), (o,))
    return row.reshape(1, SMALL_W)


def _split_w_in(w):
    big = jnp.concatenate([w[:, _O_NA:_O_DNQKV], w[:, _O_DNQKV:_O_DNZ], w[:, _O_XBC:_O_SSDDT],
                           w[:, _O_DNZ:_O_DNB], w[:, _O_SSDZ:_O_XBC], w[:, _O_GATE:_O_END]], axis=1)
    small = jnp.concatenate([w[:, _O_DNB:_O_SSDZ], w[:, _O_SSDDT:_O_GATE],
                             jnp.zeros((w.shape[0], SMALL_W - 4 * DN_HEADS - 2 * SSD_HEADS), w.dtype)], axis=1)
    return big.astype(BF16), small.astype(BF16)


def kernel(x, c, ctx, c_ctx, w_ada, b_ada, norm1_g, norm2_g, w_in, na_q_gain, na_k_gain, na_rpb,
           dn_conv_w, dn_a_log, dn_dt_bias, dn_o_gain, ssd_conv_w, ssd_conv_b, ssd_a_log,
           ssd_dt_bias, ssd_d, ssd_o_gain, w_pa, w_pb, w_pc, w_out, w_ff1, w_ff2):
    bsz, seq, _ = x.shape
    assert bsz < MOD_ROWS and seq % TM == 0 and ctx.shape[1] == CTX_LEN
    n_l = w_ada.shape[0]
    cs = jnp.concatenate([c, c_ctx[None, :], jnp.zeros((MOD_ROWS - bsz - 1, D_MODEL), F32)], axis=0)
    mod = _ada_call(cs, w_ada, b_ada)
    xs = jnp.concatenate([ctx, x], axis=1)
    rope = _rope_tables(seq)
    zeros_w = jnp.zeros((1, 1536), F32)

    for l in range(n_l):
        need_ctx = l < n_l - 1
        mod3 = mod[l].reshape(MOD_ROWS, 1, 6 * D_MODEL)
        wb, ws = _split_w_in(w_in[l])
        u, us = _inproj_call(xs, mod3, norm1_g[l].reshape(1, D_MODEL), wb, ws)

        ya = _na_call(u, na_q_gain[l], na_k_gain[l], _na_bias_table(na_rpb[l]), need_ctx)

        qkvn = _conv_call(u, U_DN, dn_conv_w[l], zeros_w, rope, True)
        alog_dn = _pad_lanes([(S_DECAY, dn_a_log[l])])
        dtb_dn = _pad_lanes([(S_DECAY, dn_dt_bias[l])])
        yb = _gdn_scan_call(qkvn, u, us, alog_dn, dtb_dn, dn_o_gain[l], need_ctx)

        xbcn = _conv_call(u, U_XBC, ssd_conv_w[l], ssd_conv_b[l], rope, False)
        alog_ssd = _pad_lanes([(S_DT, ssd_a_log[l])])
        dtb_ssd = _pad_lanes([(S_DT, ssd_dt_bias[l])])
        yc = _ssd_scan_call(xbcn, u, us, alog_ssd, dtb_ssd, ssd_d[l], ssd_o_gain[l], need_ctx)

        x1 = _merge_call(ya, yb, yc, u, xs, mod3, w_pa[l].astype(BF16), w_pb[l].astype(BF16),
                         w_pc[l].astype(BF16), w_out[l].astype(BF16), need_ctx)
        xs = _mlp_call(x1, mod3, norm2_g[l].reshape(1, D_MODEL), w_ff1[l].astype(BF16),
                       w_ff2[l].astype(BF16), need_ctx)
    return xs
```

```python
import functools
import math

import numpy as np
import jax
import jax.numpy as jnp
from jax import lax
from jax.experimental import pallas as pl
from jax.experimental.pallas import tpu as pltpu

F32 = jnp.float32
BF16 = jnp.bfloat16

D_MODEL = 1024
DEPTH = 2
GRID_W = 64
CTX_LEN = 256
NA_HEADS = 8
NA_DIM = 64
NA_W = NA_HEADS * NA_DIM
WIN_R = 8
WIN_C = 16
DN_HEADS = 8
DN_DK = 64
DN_KW = DN_HEADS * DN_DK
DN_VW = DN_HEADS * DN_DK
SSD_HEADS = 16
SSD_P = 64
SSD_GROUPS = 2
SSD_N = 128
SSD_DI = SSD_HEADS * SSD_P
SSD_BC = SSD_GROUPS * SSD_N
SSD_XBC = SSD_DI + 2 * SSD_BC
CONV_W = 5
D_FF = 4 * D_MODEL
ROPE_THETA = 10000.0
EPS = 1e-6
NEG = -1e30

CHUNK = 64
TM = 256
HALO = 16
MOD_ROWS = 16
SMALL_W = 128

_O_NA = 0
_O_DNQKV = _O_NA + 3 * NA_W
_O_DNZ = _O_DNQKV + 2 * DN_KW + DN_VW
_O_DNB = _O_DNZ + DN_VW
_O_DNA = _O_DNB + 2 * DN_HEADS
_O_SSDZ = _O_DNA + 2 * DN_HEADS
_O_XBC = _O_SSDZ + SSD_DI
_O_SSDDT = _O_XBC + SSD_XBC
_O_GATE = _O_SSDDT + 2 * SSD_HEADS
_O_END = _O_GATE + 3 * D_MODEL

U_NA, U_DN, U_XBC, U_DNZ, U_SSDZ, U_GATE = 0, 1536, 3072, 4608, 5120, 6144
U_W = 9216
S_BETA, S_DECAY, S_DT = 0, 16, 32

VMEM_LIMIT = 56 * 1024 * 1024


def _cparams(n_axes):
    return pltpu.CompilerParams(dimension_semantics=("arbitrary",) * n_axes,
                                vmem_limit_bytes=VMEM_LIMIT)


def _dot(a, b):
    return jnp.dot(a, b, preferred_element_type=F32)


def _dot_nt(a, b):
    return lax.dot_general(a, b, (((1,), (1,)), ((), ())), preferred_element_type=F32)


def _chunk_cumsum(x, rev):
    row = lax.broadcasted_iota(jnp.int32, x.shape, 0)
    n = x.shape[0]
    sh = 1
    while sh < n:
        if rev:
            x = x + jnp.where(row < n - sh, pltpu.roll(x, shift=n - sh, axis=0), 0.0)
        else:
            x = x + jnp.where(row >= sh, pltpu.roll(x, shift=sh, axis=0), 0.0)
        sh *= 2
    return x


def _fill_chunk_cumsums(src_s, dst_s):
    def body(c, carry):
        rows = pl.ds(pl.multiple_of(c * CHUNK, CHUNK), CHUNK)
        x = src_s[rows, :]
        dst_s[0, rows, :] = _chunk_cumsum(x, False)
        dst_s[1, rows, :] = _chunk_cumsum(x, True)
        return carry

    lax.fori_loop(0, src_s.shape[0] // CHUNK, body, 0)


def _transpose_bf16(a):
    m = a.shape[1]
    eye = (lax.broadcasted_iota(jnp.int32, (m, m), 0) == lax.broadcasted_iota(jnp.int32, (m, m), 1))
    return _dot_nt(eye.astype(BF16), a).astype(BF16)


def _softplus(x):
    return jnp.maximum(x, 0.0) + jnp.log(1.0 + jnp.exp(-jnp.abs(x)))


def _sigmoid(x):
    return 1.0 / (1.0 + jnp.exp(-x))


def _silu(x):
    return x * _sigmoid(x)


def _ada_kernel(c_ref, w_ref, b_ref, o_ref):
    c = c_ref[...]
    a = _silu(c).astype(BF16)
    o_ref[0] = _dot(a, w_ref[0].astype(BF16)) + b_ref[0]


def _ada_call(cs, w_ada, b_ada):
    n_l = w_ada.shape[0]
    tn = 1536
    return pl.pallas_call(
        _ada_kernel,
        out_shape=jax.ShapeDtypeStruct((n_l, MOD_ROWS, 6 * D_MODEL), F32),
        grid=(n_l, 6 * D_MODEL // tn),
        in_specs=[pl.BlockSpec((MOD_ROWS, D_MODEL), lambda l, n: (0, 0)),
                  pl.BlockSpec((1, D_MODEL, tn), lambda l, n: (l, 0, n)),
                  pl.BlockSpec((1, 1, tn), lambda l, n: (l, 0, n))],
        out_specs=pl.BlockSpec((1, MOD_ROWS, tn), lambda l, n: (l, 0, n)),
        compiler_params=_cparams(2),
        name="ada",
    )(cs, w_ada, b_ada.reshape(n_l, 1, 6 * D_MODEL))


def _mod_row(b, t, bsz):
    return jnp.where(t == 0, bsz, b)


def _norm_mod(x, g, shift, scale):
    ms = jnp.mean(x * x, axis=-1, keepdims=True)
    y = x * lax.rsqrt(ms + EPS) * g
    return y * (1.0 + scale) + shift


def _inproj_kernel(x_ref, shift_ref, scale_ref, g_ref, wb_ref, ws_ref, u_ref, us_ref):
    h = _norm_mod(x_ref[0], g_ref[...], shift_ref[0], scale_ref[0]).astype(BF16)
    cw = 1536
    for c in range(U_W // cw):
        u_ref[0, :, c * cw:(c + 1) * cw] = _dot(h, wb_ref[:, c * cw:(c + 1) * cw]).astype(BF16)
    us_ref[0] = _dot(h, ws_ref[...])


def _inproj_call(xs, mod3, g, wb, ws):
    bsz, lt, _ = xs.shape
    nt = lt // TM
    return pl.pallas_call(
        _inproj_kernel,
        out_shape=(jax.ShapeDtypeStruct((bsz, lt, U_W), BF16),
                   jax.ShapeDtypeStruct((bsz, lt, SMALL_W), F32)),
        grid=(bsz, nt),
        in_specs=[pl.BlockSpec((1, TM, D_MODEL), lambda b, t: (b, t, 0)),
                  pl.BlockSpec((1, 1, D_MODEL), lambda b, t: (_mod_row(b, t, bsz), 0, 0)),
                  pl.BlockSpec((1, 1, D_MODEL), lambda b, t: (_mod_row(b, t, bsz), 0, 1)),
                  pl.BlockSpec((1, D_MODEL), lambda b, t: (0, 0)),
                  pl.BlockSpec((D_MODEL, U_W), lambda b, t: (0, 0)),
                  pl.BlockSpec((D_MODEL, SMALL_W), lambda b, t: (0, 0))],
        out_specs=(pl.BlockSpec((1, TM, U_W), lambda b, t: (b, t, 0)),
                   pl.BlockSpec((1, TM, SMALL_W), lambda b, t: (b, t, 0))),
        compiler_params=_cparams(2),
        name="inproj",
    )(xs, mod3, mod3, g, wb, ws)


def _head_rms(x, gain, bd):
    xx = x * x
    hi = xx.astype(BF16)
    lo = (xx - hi.astype(F32)).astype(BF16)
    ss = _dot(hi, bd) + _dot(lo, bd)
    return x * lax.rsqrt(ss * (1.0 / NA_DIM) + EPS) * gain


def _na_kernel(q_ref, k_ref, v_ref, qg_ref, kg_ref, bias_ref, y_ref, qn_s, kn_s, *, need_ctx):
    lt = q_ref.shape[1]
    rows = (lt - CTX_LEN) // GRID_W
    wr = min(WIN_R, rows)
    off = CTX_LEN if need_ctx else 0
    li = lax.broadcasted_iota(jnp.int32, (128, 128), 0) // NA_DIM
    lj = lax.broadcasted_iota(jnp.int32, (128, 128), 1) // NA_DIM
    bd = (li == lj).astype(BF16)
    scale = NA_DIM ** -0.5

    def norm_tile(i, _):
        r0 = pl.multiple_of(i * TM, TM)
        kn_s[pl.ds(r0, TM), :] = _head_rms(k_ref[0, pl.ds(r0, TM), :].astype(F32), kg_ref[...], bd).astype(BF16)
        qn_s[pl.ds(r0, TM), :] = (_head_rms(q_ref[0, pl.ds(r0, TM), :].astype(F32), qg_ref[...], bd)
                                  * scale).astype(BF16)
        return 0

    lax.fori_loop(0, lt // TM, norm_tile, 0)

    lane = lax.broadcasted_iota(jnp.int32, (1, 128), 1)
    head_mask = [lane < NA_DIM, lane >= NA_DIM]
    def stack_heads(q):
        z = jnp.zeros_like(q)
        return jnp.concatenate([jnp.where(head_mask[0], q, z), jnp.where(head_mask[1], q, z)], axis=0)

    def unstack_heads(o):
        r = o.shape[0] // 2
        return jnp.where(head_mask[0], o[0:r], o[r:])

    def attend(probs):
        ss = [[_dot_nt(p[0], kk) for kk in p[1]] for p in probs]
        ss = [[s if b is None else s + b for s, b in zip(sl, p[3])] for sl, p in zip(ss, probs)]
        ms = [functools.reduce(jnp.maximum, [s.max(axis=-1, keepdims=True) for s in sl]) for sl in ss]
        es = [[jnp.exp(s - m) for s in sl] for sl, m in zip(ss, ms)]
        dens = [functools.reduce(jnp.add, [e.sum(axis=-1, keepdims=True) for e in el]) for el in es]
        accs = [functools.reduce(jnp.add, [_dot(e.astype(BF16), vv) for e, vv in zip(el, p[2])])
                for el, p in zip(es, probs)]
        return [unstack_heads(a * (1.0 / d)) for a, d in zip(accs, dens)]

    rows_per_step = 4

    def row_body(i, _):
        probs = []
        for j in range(rows_per_step):
            r = i * rows_per_step + j
            ws = jnp.clip(r - wr // 2, 0, rows - wr)
            cls = ws - r + (WIN_R - 1)
            q_r = qn_s[pl.ds(pl.multiple_of(CTX_LEN + r * GRID_W, GRID_W), GRID_W), :]
            k0 = pl.multiple_of(CTX_LEN + ws * GRID_W, GRID_W)
            kw = kn_s[pl.ds(k0, wr * GRID_W), :]
            vw = v_ref[0, pl.ds(k0, wr * GRID_W), :]
            bias = jnp.concatenate([bias_ref[0, 0, cls], bias_ref[0, 1, cls]], axis=0)
            probs.append((stack_heads(q_r), [kw, kn_s[0:CTX_LEN, :]], [vw, v_ref[0, 0:CTX_LEN, :]],
                          [bias, None]))
        for j, y in enumerate(attend(probs)):
            r = i * rows_per_step + j
            y_ref[0, pl.ds(pl.multiple_of(off + r * GRID_W, GRID_W), GRID_W), :] = y.astype(BF16)
        return 0

    lax.fori_loop(0, rows // rows_per_step, row_body, 0)

    if need_ctx:
        half = CTX_LEN // 2
        probs = [(stack_heads(qn_s[j * half:(j + 1) * half, :]), [kn_s[0:CTX_LEN, :]], [v_ref[0, 0:CTX_LEN, :]],
                  [None]) for j in range(2)]
        for j, y in enumerate(attend(probs)):
            y_ref[0, j * half:(j + 1) * half, :] = y.astype(BF16)


def _na_call(u, q_gain, k_gain, bias_tab, need_ctx):
    bsz, lt, _ = u.shape
    lo = lt if need_ctx else lt - CTX_LEN
    npair = NA_HEADS // 2
    c0 = U_NA // 128
    qg = jnp.tile(q_gain.astype(F32), 2).reshape(1, 128)
    kg = jnp.tile(k_gain.astype(F32), 2).reshape(1, 128)
    return pl.pallas_call(
        functools.partial(_na_kernel, need_ctx=need_ctx),
        out_shape=jax.ShapeDtypeStruct((bsz, lo, NA_W), BF16),
        grid=(npair, bsz),
        in_specs=[pl.BlockSpec((1, lt, 128), lambda p, b: (b, 0, c0 + p)),
                  pl.BlockSpec((1, lt, 128), lambda p, b: (b, 0, c0 + npair + p)),
                  pl.BlockSpec((1, lt, 128), lambda p, b: (b, 0, c0 + 2 * npair + p)),
                  pl.BlockSpec((1, 128), lambda p, b: (0, 0)),
                  pl.BlockSpec((1, 128), lambda p, b: (0, 0)),
                  pl.BlockSpec((1, 2, WIN_R, GRID_W, WIN_R * GRID_W), lambda p, b: (p, 0, 0, 0, 0))],
        out_specs=pl.BlockSpec((1, lo, 128), lambda p, b: (b, 0, p)),
        scratch_shapes=[pltpu.VMEM((lt, 128), BF16), pltpu.VMEM((lt, 128), BF16)],
        compiler_params=_cparams(2),
        name="na",
    )(u, u, u, qg, kg, bias_tab)


def _na_bias_table(rpb):
    q = np.arange(GRID_W)[:, None]
    kc = np.arange(GRID_W)[None, :]
    wstart = np.clip(q - WIN_C // 2, 0, GRID_W - WIN_C)
    ok = (kc >= wstart) & (kc < wstart + WIN_C)
    dc = np.clip(kc - q, 1 - WIN_C, WIN_C - 1) + (WIN_C - 1)
    onehot = (np.arange(2 * WIN_C - 1)[:, None, None] == dc[None]) & ok[None]
    toep = jnp.einsum('hdc,cqk->hdqk', rpb.astype(F32), jnp.asarray(onehot, F32),
                      precision=lax.Precision.HIGHEST)
    toep = jnp.where(ok[None, None], toep, NEG)
    t = jnp.stack([toep[:, cls:cls + WIN_R] for cls in range(WIN_R)], axis=1)
    t = jnp.transpose(t, (0, 1, 3, 2, 4)).reshape(NA_HEADS, WIN_R, GRID_W, WIN_R * GRID_W)
    return t.reshape(NA_HEADS // 2, 2, WIN_R, GRID_W, WIN_R * GRID_W)


def _rope_tables(n_lat):
    pos = np.arange(n_lat)
    quarter = DN_DK // 4
    inv = ROPE_THETA ** (-np.arange(quarter, dtype=np.float64) / quarter)
    lane = np.arange(128)
    d = lane % DN_DK
    p = np.where((d < DN_DK // 2)[None, :], (pos // GRID_W)[:, None], (pos % GRID_W)[:, None])
    ang = p * inv[d % quarter][None, :]
    first = ((d % (DN_DK // 2)) < quarter)[None, :]
    cos = np.cos(ang)
    sin_up = np.where(first, -np.sin(ang), 0.0)
    sin_dn = np.where(first, 0.0, np.sin(ang))
    return (jnp.asarray(cos, F32), jnp.asarray(sin_up, F32), jnp.asarray(sin_dn, F32))


def _conv_kernel(prev_ref, cur_ref, next_ref, w_ref, b_ref, cos_ref, sup_ref, sdn_ref, o_ref, win_s, *, gdn):
    t = pl.program_id(1)
    nt = pl.num_programs(1)
    n_c = cur_ref.shape[2]
    pad = 8
    prev_ok = t >= 2
    next_ok = jnp.logical_and(t >= 1, t < nt - 1)
    pv = prev_ref[0].astype(F32)[HALO - pad:HALO, :]
    nx = next_ref[0].astype(F32)[0:pad, :]
    win_s[0:pad, :] = jnp.where(prev_ok, pv, jnp.zeros_like(pv))
    win_s[pad + TM:pad + TM + pad, :] = jnp.where(next_ok, nx, jnp.zeros_like(nx))
    win_s[pad:pad + TM, :] = cur_ref[0].astype(F32)

    sub = 64
    cb = 512
    wrows = sub + 2 * pad
    li = lax.broadcasted_iota(jnp.int32, (128, 128), 0) // DN_DK
    lj = lax.broadcasted_iota(jnp.int32, (128, 128), 1) // DN_DK
    bd = (li == lj).astype(BF16)
    is_lat = t >= 1
    for s in range(TM // sub):
        for c in range(n_c // cb):
            win = win_s[s * sub:s * sub + wrows, c * cb:(c + 1) * cb]
            acc = None
            for k in range(CONV_W):
                sh = (CONV_W // 2 - k) % wrows
                rolled = win if sh == 0 else pltpu.roll(win, shift=sh, axis=0)
                term = rolled[pad:pad + sub, :] * w_ref[k:k + 1, c * cb:(c + 1) * cb]
                acc = term if acc is None else acc + term
            acc = acc + b_ref[:, c * cb:(c + 1) * cb]
            acc = _silu(acc)
            if gdn and c < 2:
                pieces = []
                for j in range(cb // 128):
                    x = acc[:, j * 128:(j + 1) * 128]
                    xx = x * x
                    hi = xx.astype(BF16)
                    lo = (xx - hi.astype(F32)).astype(BF16)
                    ss = _dot(hi, bd) + _dot(lo, bd)
                    x = x * lax.rsqrt(ss + EPS)
                    cos = cos_ref[s * sub:(s + 1) * sub, :]
                    sup = sup_ref[s * sub:(s + 1) * sub, :]
                    sdn = sdn_ref[s * sub:(s + 1) * sub, :]
                    quarter = DN_DK // 4
                    xr = (x * cos + pltpu.roll(x, shift=128 - quarter, axis=1) * sup
                          + pltpu.roll(x, shift=quarter, axis=1) * sdn)
                    x = jnp.where(is_lat, xr, x)
                    if c == 0:
                        x = x * (DN_DK ** -0.5)
                    pieces.append(x)
                acc = jnp.concatenate(pieces, axis=1)
            o_ref[0, s * sub:(s + 1) * sub, c * cb:(c + 1) * cb] = acc.astype(BF16)


def _conv_call(u, col0, conv_w, conv_b, rope, gdn):
    bsz, lt, _ = u.shape
    nt = lt // TM
    width = 1536
    cblk = col0 // width
    hb = TM // HALO
    nhb = lt // HALO
    cos, sup, sdn = rope
    tab_spec = pl.BlockSpec((TM, 128), lambda b, t: (jnp.maximum(t - 1, 0), 0))
    return pl.pallas_call(
        functools.partial(_conv_kernel, gdn=gdn),
        out_shape=jax.ShapeDtypeStruct((bsz, lt, width), BF16),
        grid=(bsz, nt),
        in_specs=[pl.BlockSpec((1, HALO, width), lambda b, t: (b, jnp.maximum(t * hb - 1, 0), cblk)),
                  pl.BlockSpec((1, TM, width), lambda b, t: (b, t, cblk)),
                  pl.BlockSpec((1, HALO, width), lambda b, t: (b, jnp.minimum((t + 1) * hb, nhb - 1), cblk)),
                  pl.BlockSpec((CONV_W, width), lambda b, t: (0, 0)),
                  pl.BlockSpec((1, width), lambda b, t: (0, 0)),
                  tab_spec, tab_spec, tab_spec],
        out_specs=pl.BlockSpec((1, TM, width), lambda b, t: (b, t, 0)),
        scratch_shapes=[pltpu.VMEM((TM + 16, width), F32)],
        compiler_params=_cparams(2),
        name="conv_gdn" if gdn else "conv_ssd",
    )(u, u, u, conv_w.astype(F32), conv_b.astype(F32).reshape(1, width), cos, sup, sdn)


PAIR_W = 2 * CHUNK


def _pair_iotas():
    ii = lax.broadcasted_iota(jnp.int32, (CHUNK, PAIR_W), 0)
    jj = lax.broadcasted_iota(jnp.int32, (CHUNK, PAIR_W), 1) & (CHUNK - 1)
    return ii, jj


def _tri_masks(rev):
    ii, jj = _pair_iotas()
    if rev:
        return ii <= jj, ii < jj, ii == jj
    return ii >= jj, ii > jj, ii == jj


def _col_pair(x, lane):
    idx = lane + lax.broadcasted_iota(jnp.int32, (CHUNK, PAIR_W), 1) // CHUNK
    return jnp.take_along_axis(x, idx, axis=1)


def _row_form(gc, eye):
    return jnp.sum(jnp.where(eye, gc, 0.0), axis=0, keepdims=True)


def _bd(y):
    r = lax.broadcasted_iota(jnp.int32, (PAIR_W, PAIR_W), 0) // CHUNK
    c = lax.broadcasted_iota(jnp.int32, (PAIR_W, PAIR_W), 1) // CHUNK
    return jnp.where(r == c, jnp.concatenate([y, y], axis=0), jnp.zeros((PAIR_W, PAIR_W), y.dtype))


def _bd_mask(x):
    r = lax.broadcasted_iota(jnp.int32, (PAIR_W, PAIR_W), 0) // CHUNK
    c = lax.broadcasted_iota(jnp.int32, (PAIR_W, PAIR_W), 1) // CHUNK
    return jnp.where(r == c, x, 0.0)


TRI_LEVELS = CHUNK.bit_length() - 1


def _tri_inverse_levels(a_list, t_list, eye_f, lev_lo, lev_hi):
    ii, jj = _pair_iotas()
    n = range(len(a_list))
    t = t_list
    if lev_lo == 0:
        base = (ii >> 1) == (jj >> 1)
        t = [eye_f - jnp.where(base, a, 0.0) for a in a_list]
    for lev in range(max(lev_lo, 1), lev_hi):
        pair = jnp.logical_and((ii >> lev) != (jj >> lev), (ii >> (lev + 1)) == (jj >> (lev + 1)))
        lo = [_bd(jnp.where(pair, a, 0.0).astype(BF16)) for a in a_list]
        tb = [x.astype(BF16) for x in t]
        x1 = [_dot(tb[i], lo[i]).astype(BF16) for i in n]
        yield
        x2 = [_dot(x1[i], _bd(tb[i])) for i in n]
        yield
        t = [t[i] - x2[i] for i in n]
    return t


def _interleave(*gens):
    results = [None] * len(gens)
    active = list(enumerate(gens))
    while active:
        still = []
        for idx, g in active:
            try:
                next(g)
                still.append((idx, g))
            except StopIteration as e:
                results[idx] = e.value
        active = still
    return results


def _scan_loops(n_ctx, n_chunks, need_ctx, pair_step):
    def run(c_lo, c_hi, with_out):
        def step(i, carry):
            pair_step(c_lo + i, c_hi - 1 - i, with_out)
            return carry

        lax.fori_loop(0, c_hi - c_lo, step, 0)

    run(0, n_ctx, need_ctx)
    run(n_ctx, n_chunks, True)


def _gdn_scan_kernel(qkv_ref, z_ref, us_ref, alog_ref, dtb_ref, og_ref, y_ref, feat_s, gam_s, o_s, st_s,
                     *stage_s, need_ctx):
    pa_s, pb_s = stage_s[:3], stage_s[3:]
    lt = qkv_ref.shape[1]
    n_chunks = lt // CHUNK
    n_ctx = CTX_LEN // CHUNK
    off = 0 if need_ctx else CTX_LEN
    lo = lt - off
    n_pairs = DN_HEADS // 2

    u = us_ref[0]
    lane = lax.broadcasted_iota(jnp.int32, (1, SMALL_W), 1)
    beta = _sigmoid(u)
    g = -jnp.exp(alog_ref[...]) * _softplus(u + dtb_ref[...])
    feat_s[...] = jnp.where(lane < S_DECAY, beta, g)
    _fill_chunk_cumsums(feat_s, gam_s)
    st_s[...] = jnp.zeros(st_s.shape, F32)
    o_s[...] = jnp.zeros(o_s.shape, F32)

    masks = [_tri_masks(False), _tri_masks(True)]
    eye = masks[0][2]
    eye_f = eye.astype(F32)
    last = [CHUNK - 1, 0]
    units = [(d, p) for d in range(2) for p in range(n_pairs)]
    nu = range(len(units))

    def pair(x, p):
        return x[:, p * PAIR_W:(p + 1) * PAIR_W]

    split = TRI_LEVELS // 2

    def rows_of(s):
        cb = jnp.where(s < n_ctx, n_ctx - 1 - s, n_chunks - 1 + n_ctx - s)
        return [pl.multiple_of(s * CHUNK, CHUNK), pl.multiple_of(cb * CHUNK, CHUNK)]

    def load_pairs(r0, col0, dtype):
        blk = [qkv_ref[0, pl.ds(r, CHUNK), col0:col0 + DN_KW].astype(dtype) for r in r0]
        return [pair(blk[d], p) for d, p in units]

    def stage_a(s):
        r0 = rows_of(s)
        f = [feat_s[pl.ds(r, CHUNK), :] for r in r0]
        gam = [gam_s[d, pl.ds(r0[d], CHUNK), :] for d in range(2)]
        k = load_pairs(r0, DN_KW, F32)
        bcol = [_col_pair(f[d], S_BETA + d * DN_HEADS + 2 * p) for d, p in units]
        kb = [(k[i] * bcol[i]).astype(BF16) for i in nu]
        kbd = [_bd(x.astype(BF16)) for x in k]
        gc = [_col_pair(gam[d], S_DECAY + d * DN_HEADS + 2 * p) for d, p in units]
        gr = [_row_form(x, eye) for x in gc]
        dec = [jnp.exp(jnp.where(masks[units[i][0]][0], gc[i] - gr[i], NEG)) for i in nu]
        kk = [_dot_nt(kb[i], kbd[i]) for i in nu]
        yield
        a = [jnp.where(masks[units[i][0]][1], kk[i] * dec[i], 0.0) for i in nu]
        t = yield from _tri_inverse_levels(a, None, eye_f, 0, split)
        return a, t, dec

    def stage_b(s, pa):
        a, t, dec = pa
        r0 = rows_of(s)
        f = [feat_s[pl.ds(r, CHUNK), :] for r in r0]
        gam = [gam_s[d, pl.ds(r0[d], CHUNK), :] for d in range(2)]
        k = load_pairs(r0, DN_KW, F32)
        v = load_pairs(r0, 2 * DN_KW, F32)
        q = load_pairs(r0, 0, BF16)
        bcol = [_col_pair(f[d], S_BETA + d * DN_HEADS + 2 * p) for d, p in units]
        gc = [_col_pair(gam[d], S_DECAY + d * DN_HEADS + 2 * p) for d, p in units]
        egam = [jnp.exp(x) for x in gc]
        glast = [gc[i][last[units[i][0]]:last[units[i][0]] + 1, :] for i in nu]
        kdt = [_transpose_bf16((k[i] * jnp.exp(glast[i] - gc[i])).astype(BF16)) for i in nu]
        qk = [(_dot_nt(q[i], _bd(k[i].astype(BF16))) * dec[i]).astype(BF16) for i in nu]
        qg = [(q[i].astype(F32) * egam[i]).astype(BF16) for i in nu]
        eg = [jnp.exp(x) for x in glast]
        t = yield from _tri_inverse_levels(a, t, eye_f, split, TRI_LEVELS)
        tb = [x.astype(BF16) for x in t]
        uu = [_dot(tb[i], _bd((v[i] * bcol[i]).astype(BF16))) for i in nu]
        ww = [_dot(tb[i], _bd((k[i] * bcol[i] * egam[i]).astype(BF16))).astype(BF16) for i in nu]
        yield
        return uu, ww, kdt, eg, qk, qg

    def stage_c(s, pb, with_out):
        uu, ww, kdt, eg, qk, qg = pb
        r0 = rows_of(s)
        st = [st_s[d, p] for d, p in units]
        sb = [x.astype(BF16) for x in st]
        pred = [_dot(ww[i], sb[i]) for i in nu]
        if with_out:
            o1 = [_dot(qg[i], sb[i]) for i in nu]
        yield
        vnb = [(uu[i] - pred[i]).astype(BF16) for i in nu]
        upd = [_bd_mask(_dot(kdt[i], vnb[i])) for i in nu]
        if with_out:
            o2 = [_dot(qk[i], _bd(vnb[i])) for i in nu]
        yield
        for i, (d, p) in enumerate(units):
            st_s[d, p] = st[i] * eg[i] + upd[i]
        if with_out:
            for d in range(2):
                o_all = jnp.concatenate([o1[i] + o2[i] for i in nu if units[i][0] == d], axis=1)
                rows = pl.ds(pl.multiple_of(r0[d] - off, CHUNK), CHUNK)
                o_s[rows, :] = o_s[rows, :] + o_all

    def put(refs, vals):
        for ref, group in zip(refs, vals):
            for i, x in enumerate(group):
                ref[i] = x if x.shape[0] != 1 else jnp.broadcast_to(x, ref.shape[1:])

    def get(refs, n_rows):
        return tuple([ref[i] if nr is None else ref[i, 0:nr, :] for i in range(ref.shape[0])]
                     for ref, nr in zip(refs, n_rows))

    a_rows = (None, None, None)
    b_rows = (None, None, None, 1, None, None)

    def pipelined(s_lo, s_hi, with_out):
        def step(s, carry):
            pa, pb = get(pa_s, a_rows), get(pb_s, b_rows)
            _, rb, ra = _interleave(stage_c(s, pb, with_out), stage_b(s + 1, pa), stage_a(s + 2))
            put(pb_s, rb)
            put(pa_s, ra)
            return carry

        lax.fori_loop(s_lo, s_hi, step, 0)

    (ra,) = _interleave(stage_a(0))
    rb, ra = _interleave(stage_b(0, ra), stage_a(1))
    put(pb_s, rb)
    put(pa_s, ra)
    if need_ctx:
        pipelined(0, n_chunks - 2, True)
    else:
        pipelined(0, n_ctx, False)
        pipelined(n_ctx, n_chunks - 2, True)
    _, rb = _interleave(stage_c(n_chunks - 2, get(pb_s, b_rows), True),
                        stage_b(n_chunks - 1, get(pa_s, a_rows)))
    _interleave(stage_c(n_chunks - 1, rb, True))

    li = lax.broadcasted_iota(jnp.int32, (PAIR_W, PAIR_W), 0) // DN_DK
    lj = lax.broadcasted_iota(jnp.int32, (PAIR_W, PAIR_W), 1) // DN_DK
    ones_bd = (li == lj).astype(BF16)

    def finalize(t, carry):
        ro = pl.multiple_of(t * CHUNK, CHUNK)
        o_all = o_s[pl.ds(ro, CHUNK), :]
        z = z_ref[0, pl.ds(pl.multiple_of(ro + off, CHUNK), CHUNK), :].astype(F32)
        ys = [_head_rms(pair(o_all, p), og_ref[...], ones_bd) for p in range(n_pairs)]
        y_ref[0, pl.ds(ro, CHUNK), :] = (jnp.concatenate(ys, axis=1) * _silu(z)).astype(BF16)
        return carry

    lax.fori_loop(0, lo // CHUNK, finalize, 0)


def _gdn_scan_call(qkvn, u, us, alog_pad, dtb_pad, o_gain, need_ctx):
    bsz, lt, _ = qkvn.shape
    lo = lt if need_ctx else lt - CTX_LEN
    return pl.pallas_call(
        functools.partial(_gdn_scan_kernel, need_ctx=need_ctx),
        out_shape=jax.ShapeDtypeStruct((bsz, lo, DN_VW), BF16),
        grid=(bsz,),
        in_specs=[pl.BlockSpec((1, lt, 3 * DN_KW), lambda b: (b, 0, 0)),
                  pl.BlockSpec((1, lt, DN_VW), lambda b: (b, 0, U_DNZ // DN_VW)),
                  pl.BlockSpec((1, lt, SMALL_W), lambda b: (b, 0, 0)),
                  pl.BlockSpec((1, SMALL_W), lambda b: (0, 0)),
                  pl.BlockSpec((1, SMALL_W), lambda b: (0, 0)),
                  pl.BlockSpec((1, PAIR_W), lambda b: (0, 0))],
        out_specs=pl.BlockSpec((1, lo, DN_VW), lambda b: (b, 0, 0)),
        scratch_shapes=[pltpu.VMEM((lt, SMALL_W), F32),
                        pltpu.VMEM((2, lt, SMALL_W), F32),
                        pltpu.VMEM((lo, DN_VW), F32),
                        pltpu.VMEM((2, DN_HEADS // 2, PAIR_W, PAIR_W), F32),
                        pltpu.VMEM((DN_HEADS, CHUNK, PAIR_W), F32),
                        pltpu.VMEM((DN_HEADS, CHUNK, PAIR_W), F32),
                        pltpu.VMEM((DN_HEADS, CHUNK, PAIR_W), F32),
                        pltpu.VMEM((DN_HEADS, CHUNK, PAIR_W), F32),
                        pltpu.VMEM((DN_HEADS, CHUNK, PAIR_W), BF16),
                        pltpu.VMEM((DN_HEADS, PAIR_W, CHUNK), BF16),
                        pltpu.VMEM((DN_HEADS, 8, PAIR_W), F32),
                        pltpu.VMEM((DN_HEADS, CHUNK, PAIR_W), BF16),
                        pltpu.VMEM((DN_HEADS, CHUNK, PAIR_W), BF16)],
        compiler_params=_cparams(1),
        name="gdn_scan",
    )(qkvn, u, us, alog_pad, dtb_pad, jnp.tile(o_gain.astype(F32), 2).reshape(1, PAIR_W))


def _ssd_scan_kernel(xbc_ref, z_ref, us_ref, alog_ref, dtb_ref, dskip_ref, og_ref, y_ref,
                     dt_s, la_s, lam_s, y_s, st_s, *, need_ctx):
    lt = xbc_ref.shape[1]
    n_chunks = lt // CHUNK
    n_ctx = CTX_LEN // CHUNK
    off = 0 if need_ctx else CTX_LEN
    hpg = SSD_HEADS // SSD_GROUPS
    gw = SSD_DI // SSD_GROUPS

    lo = lt - off
    ppg = hpg // 2

    dt = _softplus(us_ref[0] + dtb_ref[...])
    dt_s[...] = dt
    la_s[...] = dt * (-jnp.exp(alog_ref[...]))
    _fill_chunk_cumsums(la_s, lam_s)
    st_s[...] = jnp.zeros(st_s.shape, F32)
    y_s[...] = jnp.zeros(y_s.shape, F32)

    masks = [_tri_masks(False), _tri_masks(True)]
    eye = masks[0][2]
    last = [CHUNK - 1, 0]
    groups = [(d, gi) for d in range(2) for gi in range(SSD_GROUPS)]

    def pair_step(cf, cb, with_out):
        r0 = [pl.multiple_of(cf * CHUNK, CHUNK), pl.multiple_of(cb * CHUNK, CHUNK)]
        lam = [lam_s[d, pl.ds(r0[d], CHUNK), :] for d in range(2)]
        dtc = [dt_s[pl.ds(r, CHUNK), :] for r in r0]
        for d, gi in groups:
            rows_in = pl.ds(r0[d], CHUNK)
            bg = xbc_ref[0, rows_in, SSD_DI + gi * SSD_N:SSD_DI + (gi + 1) * SSD_N]
            cg = xbc_ref[0, rows_in, SSD_DI + SSD_BC + gi * SSD_N:SSD_DI + SSD_BC + (gi + 1) * SSD_N]
            bgt = _transpose_bf16(bg)
            pairs = [gi * ppg + r for r in range(ppg)]
            nu = range(ppg)
            x = [xbc_ref[0, rows_in, pp * PAIR_W:(pp + 1) * PAIR_W].astype(F32) for pp in pairs]
            ln = [S_DT + d * SSD_HEADS + 2 * pp for pp in pairs]
            gc = [_col_pair(lam[d], ln[i]) for i in nu]
            xdt = [x[i] * _col_pair(dtc[d], ln[i]) for i in nu]
            glast = [gc[i][last[d]:last[d] + 1, :] for i in nu]
            xdec = [(xdt[i] * jnp.exp(glast[i] - gc[i])).astype(BF16) for i in nu]
            ht = [st_s[d, pp] for pp in pairs]
            upd = [_dot(bgt, xdec[i]) for i in nu]
            for i, pp in enumerate(pairs):
                st_s[d, pp] = ht[i] * jnp.exp(glast[i]) + upd[i]
            if with_out:
                cb2 = _dot_nt(cg, jnp.concatenate([bg, bg], axis=0))
                gr = [_row_form(g_, eye) for g_ in gc]
                dec = [jnp.exp(jnp.where(masks[d][0], gc[i] - gr[i], NEG)) for i in nu]
                y1 = [_dot((cb2 * dec[i]).astype(BF16), _bd(xdt[i].astype(BF16))) for i in nu]
                y2 = [_dot(cg, ht[i].astype(BF16)) * jnp.exp(gc[i]) for i in nu]
                y_grp = jnp.concatenate([y1[i] + y2[i] for i in nu], axis=1)
                rows = pl.ds(pl.multiple_of(r0[d] - off, CHUNK), CHUNK)
                y_s[rows, gi * gw:(gi + 1) * gw] = y_s[rows, gi * gw:(gi + 1) * gw] + y_grp

    _scan_loops(n_ctx, n_chunks, need_ctx, pair_step)

    def finalize(t, carry):
        ro = pl.multiple_of(t * CHUNK, CHUNK)
        ri = pl.multiple_of(ro + off, CHUNK)
        xs = xbc_ref[0, pl.ds(ri, CHUNK), 0:SSD_DI].astype(F32)
        y_all = (y_s[pl.ds(ro, CHUNK), :] + xs * dskip_ref[...]) * _silu(z_ref[0, pl.ds(ri, CHUNK), :].astype(F32))
        ys = []
        for gi in range(SSD_GROUPS):
            yg = y_all[:, gi * gw:(gi + 1) * gw]
            ms = jnp.mean(yg * yg, axis=-1, keepdims=True)
            ys.append(yg * lax.rsqrt(ms + EPS) * og_ref[:, gi * gw:(gi + 1) * gw])
        y_ref[0, pl.ds(ro, CHUNK), :] = jnp.concatenate(ys, axis=1).astype(BF16)
        return carry

    lax.fori_loop(0, lo // CHUNK, finalize, 0)


def _ssd_scan_call(xbcn, u, us, alog_pad, dtb_pad, d_skip, o_gain, need_ctx):
    bsz, lt, _ = xbcn.shape
    lo = lt if need_ctx else lt - CTX_LEN
    dskip = jnp.repeat(d_skip.astype(F32), SSD_P).reshape(1, SSD_DI)
    return pl.pallas_call(
        functools.partial(_ssd_scan_kernel, need_ctx=need_ctx),
        out_shape=jax.ShapeDtypeStruct((bsz, lo, SSD_DI), BF16),
        grid=(bsz,),
        in_specs=[pl.BlockSpec((1, lt, SSD_XBC), lambda b: (b, 0, 0)),
                  pl.BlockSpec((1, lt, SSD_DI), lambda b: (b, 0, U_SSDZ // SSD_DI)),
                  pl.BlockSpec((1, lt, SMALL_W), lambda b: (b, 0, 0)),
                  pl.BlockSpec((1, SMALL_W), lambda b: (0, 0)),
                  pl.BlockSpec((1, SMALL_W), lambda b: (0, 0)),
                  pl.BlockSpec((1, SSD_DI), lambda b: (0, 0)),
                  pl.BlockSpec((1, SSD_DI), lambda b: (0, 0))],
        out_specs=pl.BlockSpec((1, lo, SSD_DI), lambda b: (b, 0, 0)),
        scratch_shapes=[pltpu.VMEM((lt, SMALL_W), F32),
                        pltpu.VMEM((lt, SMALL_W), F32),
                        pltpu.VMEM((2, lt, SMALL_W), F32),
                        pltpu.VMEM((lo, SSD_DI), F32),
                        pltpu.VMEM((2, SSD_HEADS // 2, SSD_N, PAIR_W), F32)],
        compiler_params=_cparams(1),
        name="ssd_scan",
    )(xbcn, u, us, alog_pad, dtb_pad, dskip, o_gain.astype(F32).reshape(1, SSD_DI))


def _merge_kernel(ya_ref, yb_ref, yc_ref, gt_ref, x_ref, gate_ref, wpa_ref, wpb_ref, wpc_ref, wout_ref, o_ref):
    g = _sigmoid(gt_ref[0].astype(F32))
    m = (g[:, 0:D_MODEL] * _dot(ya_ref[0], wpa_ref[...])
         + g[:, D_MODEL:2 * D_MODEL] * _dot(yb_ref[0], wpb_ref[...])
         + g[:, 2 * D_MODEL:3 * D_MODEL] * _dot(yc_ref[0], wpc_ref[...]))
    y = _dot(m.astype(BF16), wout_ref[...])
    o_ref[0] = x_ref[0] + gate_ref[0] * y


def _merge_call(ya, yb, yc, u, xs, mod3, wpa, wpb, wpc, wout, need_ctx):
    bsz, lt, _ = xs.shape
    nt = lt // TM
    ntl = nt if need_ctx else nt - 1
    t0 = 0 if need_ctx else 1

    def row(b, t):
        return _mod_row(b, t + t0, bsz)

    full = lambda b, t: (0, 0)
    return pl.pallas_call(
        _merge_kernel,
        out_shape=jax.ShapeDtypeStruct((bsz, ntl * TM, D_MODEL), F32),
        grid=(bsz, ntl),
        in_specs=[pl.BlockSpec((1, TM, NA_W), lambda b, t: (b, t, 0)),
                  pl.BlockSpec((1, TM, DN_VW), lambda b, t: (b, t, 0)),
                  pl.BlockSpec((1, TM, SSD_DI), lambda b, t: (b, t, 0)),
                  pl.BlockSpec((1, TM, 3 * D_MODEL), lambda b, t: (b, t + t0, U_GATE // (3 * D_MODEL))),
                  pl.BlockSpec((1, TM, D_MODEL), lambda b, t: (b, t + t0, 0)),
                  pl.BlockSpec((1, 1, D_MODEL), lambda b, t: (row(b, t), 0, 2)),
                  pl.BlockSpec((NA_W, D_MODEL), full),
                  pl.BlockSpec((DN_VW, D_MODEL), full),
                  pl.BlockSpec((SSD_DI, D_MODEL), full),
                  pl.BlockSpec((D_MODEL, D_MODEL), full)],
        out_specs=pl.BlockSpec((1, TM, D_MODEL), lambda b, t: (b, t, 0)),
        compiler_params=_cparams(2),
        name="merge",
    )(ya, yb, yc, u, xs, mod3, wpa, wpb, wpc, wout)


def _mlp_kernel(x_ref, shift_ref, scale_ref, gate_ref, g_ref, w1_ref, w2_ref, o_ref):
    x = x_ref[0]
    h = _norm_mod(x, g_ref[...], shift_ref[0], scale_ref[0]).astype(BF16)
    fc = 1024
    acc = None
    for c in range(D_FF // fc):
        a = jnp.maximum(_dot(h, w1_ref[:, c * fc:(c + 1) * fc]), 0.0)
        part = _dot((a * a).astype(BF16), w2_ref[c * fc:(c + 1) * fc, :])
        acc = part if acc is None else acc + part
    o_ref[0] = x + gate_ref[0] * acc


def _mlp_call(x1, mod3, g, w1, w2, need_ctx):
    bsz, lt, _ = x1.shape
    nt = lt // TM
    t0 = 0 if need_ctx else 1

    def row(b, t):
        return _mod_row(b, t + t0, bsz)

    full = lambda b, t: (0, 0)
    return pl.pallas_call(
        _mlp_kernel,
        out_shape=jax.ShapeDtypeStruct((bsz, lt, D_MODEL), F32),
        grid=(bsz, nt),
        in_specs=[pl.BlockSpec((1, TM, D_MODEL), lambda b, t: (b, t, 0)),
                  pl.BlockSpec((1, 1, D_MODEL), lambda b, t: (row(b, t), 0, 3)),
                  pl.BlockSpec((1, 1, D_MODEL), lambda b, t: (row(b, t), 0, 4)),
                  pl.BlockSpec((1, 1, D_MODEL), lambda b, t: (row(b, t), 0, 5)),
                  pl.BlockSpec((1, D_MODEL), full),
                  pl.BlockSpec((D_MODEL, D_FF), full),
                  pl.BlockSpec((D_FF, D_MODEL), full)],
        out_specs=pl.BlockSpec((1, TM, D_MODEL), lambda b, t: (b, t, 0)),
        compiler_params=_cparams(2),
        name="mlp",
    )(x1, mod3, mod3, mod3, g, w1, w2)


def _pad_lanes(pieces):
    row = jnp.zeros((SMALL_W,), F32)
    for o, v in pieces:
        row = lax.dynamic_update_slice(row, v.astype(F32).reshape(-1), (o,))
    return row.reshape(1, SMALL_W)


def _split_w_in(w):
    big = jnp.concatenate([w[:, _O_NA:_O_DNQKV], w[:, _O_DNQKV:_O_DNZ], w[:, _O_XBC:_O_SSDDT],
                           w[:, _O_DNZ:_O_DNB], w[:, _O_SSDZ:_O_XBC], w[:, _O_GATE:_O_END]], axis=1)
    small = jnp.concatenate([w[:, _O_DNB:_O_SSDZ], w[:, _O_SSDDT:_O_GATE],
                             jnp.zeros((w.shape[0], SMALL_W - 4 * DN_HEADS - 2 * SSD_HEADS), w.dtype)], axis=1)
    return big.astype(BF16), small.astype(BF16)


def kernel(x, c, ctx, c_ctx, w_ada, b_ada, norm1_g, norm2_g, w_in, na_q_gain, na_k_gain, na_rpb,
           dn_conv_w, dn_a_log, dn_dt_bias, dn_o_gain, ssd_conv_w, ssd_conv_b, ssd_a_log,
           ssd_dt_bias, ssd_d, ssd_o_gain, w_pa, w_pb, w_pc, w_out, w_ff1, w_ff2):
    bsz, seq, _ = x.shape
    assert bsz < MOD_ROWS and seq % TM == 0 and ctx.shape[1] == CTX_LEN
    n_l = w_ada.shape[0]
    cs = jnp.concatenate([c, c_ctx[None, :], jnp.zeros((MOD_ROWS - bsz - 1, D_MODEL), F32)], axis=0)
    mod = _ada_call(cs, w_ada, b_ada)
    xs = jnp.concatenate([ctx, x], axis=1)
    rope = _rope_tables(seq)
    zeros_w = jnp.zeros((1, 1536), F32)

    for l in range(n_l):
        need_ctx = l < n_l - 1
        mod3 = mod[l].reshape(MOD_ROWS, 1, 6 * D_MODEL)
        wb, ws = _split_w_in(w_in[l])
        u, us = _inproj_call(xs, mod3, norm1_g[l].reshape(1, D_MODEL), wb, ws)

        ya = _na_call(u, na_q_gain[l], na_k_gain[l], _na_bias_table(na_rpb[l]), need_ctx)

        qkvn = _conv_call(u, U_DN, dn_conv_w[l], zeros_w, rope, True)
        alog_dn = _pad_lanes([(S_DECAY, dn_a_log[l])])
        dtb_dn = _pad_lanes([(S_DECAY, dn_dt_bias[l])])
        yb = _gdn_scan_call(qkvn, u, us, alog_dn, dtb_dn, dn_o_gain[l], need_ctx)

        xbcn = _conv_call(u, U_XBC, ssd_conv_w[l], ssd_conv_b[l], rope, False)
        alog_ssd = _pad_lanes([(S_DT, ssd_a_log[l])])
        dtb_ssd = _pad_lanes([(S_DT, ssd_dt_bias[l])])
        yc = _ssd_scan_call(xbcn, u, us, alog_ssd, dtb_ssd, ssd_d[l], ssd_o_gain[l], need_ctx)

        x1 = _merge_call(ya, yb, yc, u, xs, mod3, w_pa[l].astype(BF16), w_pb[l].astype(BF16),
                         w_pc[l].astype(BF16), w_out[l].astype(BF16), need_ctx)
        xs = _mlp_call(x1, mod3, norm2_g[l].reshape(1, D_MODEL), w_ff1[l].astype(BF16),
                       w_ff2[l].astype(BF16), need_ctx)
    return xs
```

```python
import functools
import math

import numpy as np
import jax
import jax.numpy as jnp
from jax import lax
from jax.experimental import pallas as pl
from jax.experimental.pallas import tpu as pltpu

F32 = jnp.float32
BF16 = jnp.bfloat16

D_MODEL = 1024
DEPTH = 2
GRID_W = 64
CTX_LEN = 256
NA_HEADS = 8
NA_DIM = 64
NA_W = NA_HEADS * NA_DIM
WIN_R = 8
WIN_C = 16
DN_HEADS = 8
DN_DK = 64
DN_KW = DN_HEADS * DN_DK
DN_VW = DN_HEADS * DN_DK
SSD_HEADS = 16
SSD_P = 64
SSD_GROUPS = 2
SSD_N = 128
SSD_DI = SSD_HEADS * SSD_P
SSD_BC = SSD_GROUPS * SSD_N
SSD_XBC = SSD_DI + 2 * SSD_BC
CONV_W = 5
D_FF = 4 * D_MODEL
ROPE_THETA = 10000.0
EPS = 1e-6
NEG = -1e30

CHUNK = 64
TM = 256
HALO = 16
MOD_ROWS = 16
SMALL_W = 128

_O_NA = 0
_O_DNQKV = _O_NA + 3 * NA_W
_O_DNZ = _O_DNQKV + 2 * DN_KW + DN_VW
_O_DNB = _O_DNZ + DN_VW
_O_DNA = _O_DNB + 2 * DN_HEADS
_O_SSDZ = _O_DNA + 2 * DN_HEADS
_O_XBC = _O_SSDZ + SSD_DI
_O_SSDDT = _O_XBC + SSD_XBC
_O_GATE = _O_SSDDT + 2 * SSD_HEADS
_O_END = _O_GATE + 3 * D_MODEL

U_NA, U_DN, U_XBC, U_DNZ, U_SSDZ, U_GATE = 0, 1536, 3072, 4608, 5120, 6144
U_W = 9216
S_BETA, S_DECAY, S_DT = 0, 16, 32

VMEM_LIMIT = 56 * 1024 * 1024


def _cparams(n_axes):
    return pltpu.CompilerParams(dimension_semantics=("arbitrary",) * n_axes,
                                vmem_limit_bytes=VMEM_LIMIT)


def _dot(a, b):
    return jnp.dot(a, b, preferred_element_type=F32)


def _dot_nt(a, b):
    return lax.dot_general(a, b, (((1,), (1,)), ((), ())), preferred_element_type=F32)


def _chunk_cumsum(x, rev):
    row = lax.broadcasted_iota(jnp.int32, x.shape, 0)
    n = x.shape[0]
    sh = 1
    while sh < n:
        if rev:
            x = x + jnp.where(row < n - sh, pltpu.roll(x, shift=n - sh, axis=0), 0.0)
        else:
            x = x + jnp.where(row >= sh, pltpu.roll(x, shift=sh, axis=0), 0.0)
        sh *= 2
    return x


def _fill_chunk_cumsums(src_s, dst_s):
    def body(c, carry):
        rows = pl.ds(pl.multiple_of(c * CHUNK, CHUNK), CHUNK)
        x = src_s[rows, :]
        dst_s[0, rows, :] = _chunk_cumsum(x, False)
        dst_s[1, rows, :] = _chunk_cumsum(x, True)
        return carry

    lax.fori_loop(0, src_s.shape[0] // CHUNK, body, 0)


def _transpose_bf16(a):
    m = a.shape[1]
    eye = (lax.broadcasted_iota(jnp.int32, (m, m), 0) == lax.broadcasted_iota(jnp.int32, (m, m), 1))
    return _dot_nt(eye.astype(BF16), a).astype(BF16)


def _softplus(x):
    return jnp.maximum(x, 0.0) + jnp.log(1.0 + jnp.exp(-jnp.abs(x)))


def _sigmoid(x):
    return 1.0 / (1.0 + jnp.exp(-x))


def _silu(x):
    return x * _sigmoid(x)


def _ada_kernel(c_ref, w_ref, b_ref, o_ref):
    c = c_ref[...]
    a = _silu(c).astype(BF16)
    o_ref[0] = _dot(a, w_ref[0].astype(BF16)) + b_ref[0]


def _ada_call(cs, w_ada, b_ada):
    n_l = w_ada.shape[0]
    tn = 1536
    return pl.pallas_call(
        _ada_kernel,
        out_shape=jax.ShapeDtypeStruct((n_l, MOD_ROWS, 6 * D_MODEL), F32),
        grid=(n_l, 6 * D_MODEL // tn),
        in_specs=[pl.BlockSpec((MOD_ROWS, D_MODEL), lambda l, n: (0, 0)),
                  pl.BlockSpec((1, D_MODEL, tn), lambda l, n: (l, 0, n)),
                  pl.BlockSpec((1, 1, tn), lambda l, n: (l, 0, n))],
        out_specs=pl.BlockSpec((1, MOD_ROWS, tn), lambda l, n: (l, 0, n)),
        compiler_params=_cparams(2),
        name="ada",
    )(cs, w_ada, b_ada.reshape(n_l, 1, 6 * D_MODEL))


def _mod_row(b, t, bsz):
    return jnp.where(t == 0, bsz, b)


def _norm_mod(x, g, shift, scale):
    ms = jnp.mean(x * x, axis=-1, keepdims=True)
    y = x * lax.rsqrt(ms + EPS) * g
    return y * (1.0 + scale) + shift


def _token_tile(x_ref, ctx_ref, t):
    return jnp.where(t == 0, ctx_ref[0], x_ref[0])


def _stream_specs(stream, t0):
    ctx_spec = pl.BlockSpec((1, TM, D_MODEL), lambda b, t: (b, 0, 0))
    if isinstance(stream, tuple):
        lat, ctx = stream
        return [pl.BlockSpec((1, TM, D_MODEL), lambda b, t: (b, jnp.maximum(t + t0 - 1, 0), 0)), ctx_spec], [lat, ctx]
    return [pl.BlockSpec((1, TM, D_MODEL), lambda b, t: (b, t + t0, 0)), ctx_spec], [stream, stream]


def _resident(shape):
    return pl.BlockSpec(shape, lambda b, t: (0,) * len(shape), pipeline_mode=pl.Buffered(1))


def _inproj_kernel(x_ref, ctx_ref, shift_ref, scale_ref, g_ref, wb_ref, ws_ref, u_ref, us_ref):
    x = _token_tile(x_ref, ctx_ref, pl.program_id(1))
    h = _norm_mod(x, g_ref[...], shift_ref[0], scale_ref[0]).astype(BF16)
    cw = 1536
    for c in range(U_W // cw):
        u_ref[0, :, c * cw:(c + 1) * cw] = _dot(h, wb_ref[:, c * cw:(c + 1) * cw]).astype(BF16)
    us_ref[0] = _dot(h, ws_ref[...])


def _inproj_call(stream, bsz, lt, mod3, g, wb, ws):
    nt = lt // TM
    x_specs, x_args = _stream_specs(stream, 0)
    return pl.pallas_call(
        _inproj_kernel,
        out_shape=(jax.ShapeDtypeStruct((bsz, lt, U_W), BF16),
                   jax.ShapeDtypeStruct((bsz, lt, SMALL_W), F32)),
        grid=(bsz, nt),
        in_specs=x_specs + [
            pl.BlockSpec((1, 1, D_MODEL), lambda b, t: (_mod_row(b, t, bsz), 0, 0)),
            pl.BlockSpec((1, 1, D_MODEL), lambda b, t: (_mod_row(b, t, bsz), 0, 1)),
            _resident((1, D_MODEL)), _resident((D_MODEL, U_W)), _resident((D_MODEL, SMALL_W))],
        out_specs=(pl.BlockSpec((1, TM, U_W), lambda b, t: (b, t, 0)),
                   pl.BlockSpec((1, TM, SMALL_W), lambda b, t: (b, t, 0))),
        compiler_params=_cparams(2),
        name="inproj",
    )(*x_args, mod3, mod3, g, wb, ws)


def _head_rms(x, gain, bd):
    xx = x * x
    hi = xx.astype(BF16)
    lo = (xx - hi.astype(F32)).astype(BF16)
    ss = _dot(hi, bd) + _dot(lo, bd)
    return x * lax.rsqrt(ss * (1.0 / NA_DIM) + EPS) * gain


def _na_kernel(q_ref, k_ref, v_ref, qg_ref, kg_ref, bias_ref, y_ref, qn_s, kn_s, *, need_ctx):
    lt = q_ref.shape[1]
    rows = (lt - CTX_LEN) // GRID_W
    wr = min(WIN_R, rows)
    off = CTX_LEN if need_ctx else 0
    li = lax.broadcasted_iota(jnp.int32, (128, 128), 0) // NA_DIM
    lj = lax.broadcasted_iota(jnp.int32, (128, 128), 1) // NA_DIM
    bd = (li == lj).astype(BF16)
    scale = NA_DIM ** -0.5

    def norm_tile(i, _):
        r0 = pl.multiple_of(i * TM, TM)
        kn_s[pl.ds(r0, TM), :] = _head_rms(k_ref[0, pl.ds(r0, TM), :].astype(F32), kg_ref[...], bd).astype(BF16)
        qn_s[pl.ds(r0, TM), :] = (_head_rms(q_ref[0, pl.ds(r0, TM), :].astype(F32), qg_ref[...], bd)
                                  * scale).astype(BF16)
        return 0

    lax.fori_loop(0, lt // TM, norm_tile, 0)

    lane = lax.broadcasted_iota(jnp.int32, (1, 128), 1)
    head_mask = [lane < NA_DIM, lane >= NA_DIM]
    def stack_heads(q):
        z = jnp.zeros_like(q)
        return jnp.concatenate([jnp.where(head_mask[0], q, z), jnp.where(head_mask[1], q, z)], axis=0)

    def unstack_heads(o):
        r = o.shape[0] // 2
        return jnp.where(head_mask[0], o[0:r], o[r:])

    def attend(probs):
        ss = [[_dot_nt(p[0], kk) for kk in p[1]] for p in probs]
        ss = [[s if b is None else s + b for s, b in zip(sl, p[3])] for sl, p in zip(ss, probs)]
        ms = [functools.reduce(jnp.maximum, [s.max(axis=-1, keepdims=True) for s in sl]) for sl in ss]
        es = [[jnp.exp(s - m) for s in sl] for sl, m in zip(ss, ms)]
        dens = [functools.reduce(jnp.add, [e.sum(axis=-1, keepdims=True) for e in el]) for el in es]
        accs = [functools.reduce(jnp.add, [_dot(e.astype(BF16), vv) for e, vv in zip(el, p[2])])
                for el, p in zip(es, probs)]
        return [unstack_heads(a * (1.0 / d)) for a, d in zip(accs, dens)]

    rows_per_step = 4

    def row_body(i, _):
        probs = []
        for j in range(rows_per_step):
            r = i * rows_per_step + j
            ws = jnp.clip(r - wr // 2, 0, rows - wr)
            cls = ws - r + (WIN_R - 1)
            q_r = qn_s[pl.ds(pl.multiple_of(CTX_LEN + r * GRID_W, GRID_W), GRID_W), :]
            k0 = pl.multiple_of(CTX_LEN + ws * GRID_W, GRID_W)
            kw = kn_s[pl.ds(k0, wr * GRID_W), :]
            vw = v_ref[0, pl.ds(k0, wr * GRID_W), :]
            bias = jnp.concatenate([bias_ref[0, 0, cls], bias_ref[0, 1, cls]], axis=0)
            probs.append((stack_heads(q_r), [kw, kn_s[0:CTX_LEN, :]], [vw, v_ref[0, 0:CTX_LEN, :]],
                          [bias, None]))
        for j, y in enumerate(attend(probs)):
            r = i * rows_per_step + j
            y_ref[0, pl.ds(pl.multiple_of(off + r * GRID_W, GRID_W), GRID_W), :] = y.astype(BF16)
        return 0

    lax.fori_loop(0, rows // rows_per_step, row_body, 0)

    if need_ctx:
        half = CTX_LEN // 2
        probs = [(stack_heads(qn_s[j * half:(j + 1) * half, :]), [kn_s[0:CTX_LEN, :]], [v_ref[0, 0:CTX_LEN, :]],
                  [None]) for j in range(2)]
        for j, y in enumerate(attend(probs)):
            y_ref[0, j * half:(j + 1) * half, :] = y.astype(BF16)


def _na_call(u, q_gain, k_gain, bias_tab, need_ctx):
    bsz, lt, _ = u.shape
    lo = lt if need_ctx else lt - CTX_LEN
    npair = NA_HEADS // 2
    c0 = U_NA // 128
    qg = jnp.tile(q_gain.astype(F32), 2).reshape(1, 128)
    kg = jnp.tile(k_gain.astype(F32), 2).reshape(1, 128)
    return pl.pallas_call(
        functools.partial(_na_kernel, need_ctx=need_ctx),
        out_shape=jax.ShapeDtypeStruct((bsz, lo, NA_W), BF16),
        grid=(npair, bsz),
        in_specs=[pl.BlockSpec((1, lt, 128), lambda p, b: (b, 0, c0 + p)),
                  pl.BlockSpec((1, lt, 128), lambda p, b: (b, 0, c0 + npair + p)),
                  pl.BlockSpec((1, lt, 128), lambda p, b: (b, 0, c0 + 2 * npair + p)),
                  pl.BlockSpec((1, 128), lambda p, b: (0, 0)),
                  pl.BlockSpec((1, 128), lambda p, b: (0, 0)),
                  pl.BlockSpec((1, 2, WIN_R, GRID_W, WIN_R * GRID_W), lambda p, b: (p, 0, 0, 0, 0))],
        out_specs=pl.BlockSpec((1, lo, 128), lambda p, b: (b, 0, p)),
        scratch_shapes=[pltpu.VMEM((lt, 128), BF16), pltpu.VMEM((lt, 128), BF16)],
        compiler_params=_cparams(2),
        name="na",
    )(u, u, u, qg, kg, bias_tab)


def _na_bias_table(rpb):
    q = np.arange(GRID_W)[:, None]
    kc = np.arange(GRID_W)[None, :]
    wstart = np.clip(q - WIN_C // 2, 0, GRID_W - WIN_C)
    ok = (kc >= wstart) & (kc < wstart + WIN_C)
    dc = np.clip(kc - q, 1 - WIN_C, WIN_C - 1) + (WIN_C - 1)
    onehot = (np.arange(2 * WIN_C - 1)[:, None, None] == dc[None]) & ok[None]
    toep = jnp.einsum('hdc,cqk->hdqk', rpb.astype(F32), jnp.asarray(onehot, F32),
                      precision=lax.Precision.HIGHEST)
    toep = jnp.where(ok[None, None], toep, NEG)
    t = jnp.stack([toep[:, cls:cls + WIN_R] for cls in range(WIN_R)], axis=1)
    t = jnp.transpose(t, (0, 1, 3, 2, 4)).reshape(NA_HEADS, WIN_R, GRID_W, WIN_R * GRID_W)
    return t.reshape(NA_HEADS // 2, 2, WIN_R, GRID_W, WIN_R * GRID_W)


def _rope_tables(n_lat):
    pos = np.arange(n_lat)
    quarter = DN_DK // 4
    inv = ROPE_THETA ** (-np.arange(quarter, dtype=np.float64) / quarter)
    lane = np.arange(128)
    d = lane % DN_DK
    p = np.where((d < DN_DK // 2)[None, :], (pos // GRID_W)[:, None], (pos % GRID_W)[:, None])
    ang = p * inv[d % quarter][None, :]
    first = ((d % (DN_DK // 2)) < quarter)[None, :]
    cos = np.cos(ang)
    sin_up = np.where(first, -np.sin(ang), 0.0)
    sin_dn = np.where(first, 0.0, np.sin(ang))
    return (jnp.asarray(cos, F32), jnp.asarray(sin_up, F32), jnp.asarray(sin_dn, F32))


def _conv_kernel(prev_ref, cur_ref, next_ref, w_ref, b_ref, cos_ref, sup_ref, sdn_ref, o_ref, win_s, *, gdn):
    t = pl.program_id(1)
    nt = pl.num_programs(1)
    n_c = cur_ref.shape[2]
    pad = 8
    prev_ok = t >= 2
    next_ok = jnp.logical_and(t >= 1, t < nt - 1)
    pv = prev_ref[0].astype(F32)[HALO - pad:HALO, :]
    nx = next_ref[0].astype(F32)[0:pad, :]
    win_s[0:pad, :] = jnp.where(prev_ok, pv, jnp.zeros_like(pv))
    win_s[pad + TM:pad + TM + pad, :] = jnp.where(next_ok, nx, jnp.zeros_like(nx))
    win_s[pad:pad + TM, :] = cur_ref[0].astype(F32)

    sub = 64
    cb = 512
    wrows = sub + 2 * pad
    li = lax.broadcasted_iota(jnp.int32, (128, 128), 0) // DN_DK
    lj = lax.broadcasted_iota(jnp.int32, (128, 128), 1) // DN_DK
    bd = (li == lj).astype(BF16)
    is_lat = t >= 1
    for s in range(TM // sub):
        for c in range(n_c // cb):
            win = win_s[s * sub:s * sub + wrows, c * cb:(c + 1) * cb]
            acc = None
            for k in range(CONV_W):
                sh = (CONV_W // 2 - k) % wrows
                rolled = win if sh == 0 else pltpu.roll(win, shift=sh, axis=0)
                term = rolled[pad:pad + sub, :] * w_ref[k:k + 1, c * cb:(c + 1) * cb]
                acc = term if acc is None else acc + term
            acc = acc + b_ref[:, c * cb:(c + 1) * cb]
            acc = _silu(acc)
            if gdn and c < 2:
                pieces = []
                for j in range(cb // 128):
                    x = acc[:, j * 128:(j + 1) * 128]
                    xx = x * x
                    hi = xx.astype(BF16)
                    lo = (xx - hi.astype(F32)).astype(BF16)
                    ss = _dot(hi, bd) + _dot(lo, bd)
                    x = x * lax.rsqrt(ss + EPS)
                    cos = cos_ref[s * sub:(s + 1) * sub, :]
                    sup = sup_ref[s * sub:(s + 1) * sub, :]
                    sdn = sdn_ref[s * sub:(s + 1) * sub, :]
                    quarter = DN_DK // 4
                    xr = (x * cos + pltpu.roll(x, shift=128 - quarter, axis=1) * sup
                          + pltpu.roll(x, shift=quarter, axis=1) * sdn)
                    x = jnp.where(is_lat, xr, x)
                    if c == 0:
                        x = x * (DN_DK ** -0.5)
                    pieces.append(x)
                acc = jnp.concatenate(pieces, axis=1)
            o_ref[0, s * sub:(s + 1) * sub, c * cb:(c + 1) * cb] = acc.astype(BF16)


def _conv_call(u, col0, conv_w, conv_b, rope, gdn):
    bsz, lt, _ = u.shape
    nt = lt // TM
    width = 1536
    cblk = col0 // width
    hb = TM // HALO
    nhb = lt // HALO
    cos, sup, sdn = rope
    tab_spec = pl.BlockSpec((TM, 128), lambda b, t: (jnp.maximum(t - 1, 0), 0))
    return pl.pallas_call(
        functools.partial(_conv_kernel, gdn=gdn),
        out_shape=jax.ShapeDtypeStruct((bsz, lt, width), BF16),
        grid=(bsz, nt),
        in_specs=[pl.BlockSpec((1, HALO, width), lambda b, t: (b, jnp.maximum(t * hb - 1, 0), cblk)),
                  pl.BlockSpec((1, TM, width), lambda b, t: (b, t, cblk)),
                  pl.BlockSpec((1, HALO, width), lambda b, t: (b, jnp.minimum((t + 1) * hb, nhb - 1), cblk)),
                  pl.BlockSpec((CONV_W, width), lambda b, t: (0, 0)),
                  pl.BlockSpec((1, width), lambda b, t: (0, 0)),
                  tab_spec, tab_spec, tab_spec],
        out_specs=pl.BlockSpec((1, TM, width), lambda b, t: (b, t, 0)),
        scratch_shapes=[pltpu.VMEM((TM + 16, width), F32)],
        compiler_params=_cparams(2),
        name="conv_gdn" if gdn else "conv_ssd",
    )(u, u, u, conv_w.astype(F32), conv_b.astype(F32).reshape(1, width), cos, sup, sdn)


PAIR_W = 2 * CHUNK


def _pair_iotas():
    ii = lax.broadcasted_iota(jnp.int32, (CHUNK, PAIR_W), 0)
    jj = lax.broadcasted_iota(jnp.int32, (CHUNK, PAIR_W), 1) & (CHUNK - 1)
    return ii, jj


def _tri_masks(rev):
    ii, jj = _pair_iotas()
    if rev:
        return ii <= jj, ii < jj, ii == jj
    return ii >= jj, ii > jj, ii == jj


def _col_pair(x, lane):
    idx = lane + lax.broadcasted_iota(jnp.int32, (CHUNK, PAIR_W), 1) // CHUNK
    return jnp.take_along_axis(x, idx, axis=1)


def _row_form(gc, eye):
    return jnp.sum(jnp.where(eye, gc, 0.0), axis=0, keepdims=True)


def _bd(y):
    r = lax.broadcasted_iota(jnp.int32, (PAIR_W, PAIR_W), 0) // CHUNK
    c = lax.broadcasted_iota(jnp.int32, (PAIR_W, PAIR_W), 1) // CHUNK
    return jnp.where(r == c, jnp.concatenate([y, y], axis=0), jnp.zeros((PAIR_W, PAIR_W), y.dtype))


def _bd_mask(x):
    r = lax.broadcasted_iota(jnp.int32, (PAIR_W, PAIR_W), 0) // CHUNK
    c = lax.broadcasted_iota(jnp.int32, (PAIR_W, PAIR_W), 1) // CHUNK
    return jnp.where(r == c, x, 0.0)


TRI_LEVELS = CHUNK.bit_length() - 1


def _tri_inverse_levels(a_list, t_list, eye_f, lev_lo, lev_hi):
    ii, jj = _pair_iotas()
    n = range(len(a_list))
    t = t_list
    if lev_lo == 0:
        base = (ii >> 1) == (jj >> 1)
        t = [eye_f - jnp.where(base, a, 0.0) for a in a_list]
    for lev in range(max(lev_lo, 1), lev_hi):
        pair = jnp.logical_and((ii >> lev) != (jj >> lev), (ii >> (lev + 1)) == (jj >> (lev + 1)))
        lo = [_bd(jnp.where(pair, a, 0.0).astype(BF16)) for a in a_list]
        tb = [x.astype(BF16) for x in t]
        x1 = [_dot(tb[i], lo[i]).astype(BF16) for i in n]
        yield
        x2 = [_dot(x1[i], _bd(tb[i])) for i in n]
        yield
        t = [t[i] - x2[i] for i in n]
    return t


def _interleave(*gens):
    results = [None] * len(gens)
    active = list(enumerate(gens))
    while active:
        still = []
        for idx, g in active:
            try:
                next(g)
                still.append((idx, g))
            except StopIteration as e:
                results[idx] = e.value
        active = still
    return results


def _gdn_scan_kernel(qkv_ref, z_ref, us_ref, alog_ref, dtb_ref, og_ref, y_ref, feat_s, gam_s, o_s, st_s,
                     *stage_s, need_ctx):
    pa_s, pb_s = stage_s[:3], stage_s[3:]
    lt = qkv_ref.shape[1]
    n_chunks = lt // CHUNK
    n_ctx = CTX_LEN // CHUNK
    off = 0 if need_ctx else CTX_LEN
    lo = lt - off
    n_pairs = DN_HEADS // 2

    u = us_ref[0]
    lane = lax.broadcasted_iota(jnp.int32, (1, SMALL_W), 1)
    beta = _sigmoid(u)
    g = -jnp.exp(alog_ref[...]) * _softplus(u + dtb_ref[...])
    feat_s[...] = jnp.where(lane < S_DECAY, beta, g)
    _fill_chunk_cumsums(feat_s, gam_s)
    st_s[...] = jnp.zeros(st_s.shape, F32)
    o_s[...] = jnp.zeros(o_s.shape, F32)

    masks = [_tri_masks(False), _tri_masks(True)]
    eye = masks[0][2]
    eye_f = eye.astype(F32)
    last = [CHUNK - 1, 0]
    units = [(d, p) for d in range(2) for p in range(n_pairs)]
    nu = range(len(units))

    def pair(x, p):
        return x[:, p * PAIR_W:(p + 1) * PAIR_W]

    split = TRI_LEVELS // 2

    def rows_of(s):
        cb = jnp.where(s < n_ctx, n_ctx - 1 - s, n_chunks - 1 + n_ctx - s)
        return [pl.multiple_of(s * CHUNK, CHUNK), pl.multiple_of(cb * CHUNK, CHUNK)]

    def load_pairs(r0, col0, dtype):
        blk = [qkv_ref[0, pl.ds(r, CHUNK), col0:col0 + DN_KW].astype(dtype) for r in r0]
        return [pair(blk[d], p) for d, p in units]

    def stage_a(s):
        r0 = rows_of(s)
        f = [feat_s[pl.ds(r, CHUNK), :] for r in r0]
        gam = [gam_s[d, pl.ds(r0[d], CHUNK), :] for d in range(2)]
        k = load_pairs(r0, DN_KW, F32)
        bcol = [_col_pair(f[d], S_BETA + d * DN_HEADS + 2 * p) for d, p in units]
        kb = [(k[i] * bcol[i]).astype(BF16) for i in nu]
        kbd = [_bd(x.astype(BF16)) for x in k]
        gc = [_col_pair(gam[d], S_DECAY + d * DN_HEADS + 2 * p) for d, p in units]
        gr = [_row_form(x, eye) for x in gc]
        dec = [jnp.exp(jnp.where(masks[units[i][0]][0], gc[i] - gr[i], NEG)) for i in nu]
        kk = [_dot_nt(kb[i], kbd[i]) for i in nu]
        yield
        a = [jnp.where(masks[units[i][0]][1], kk[i] * dec[i], 0.0) for i in nu]
        t = yield from _tri_inverse_levels(a, None, eye_f, 0, split)
        return a, t, dec

    def stage_b(s, pa):
        a, t, dec = pa
        r0 = rows_of(s)
        f = [feat_s[pl.ds(r, CHUNK), :] for r in r0]
        gam = [gam_s[d, pl.ds(r0[d], CHUNK), :] for d in range(2)]
        k = load_pairs(r0, DN_KW, F32)
        v = load_pairs(r0, 2 * DN_KW, F32)
        q = load_pairs(r0, 0, BF16)
        bcol = [_col_pair(f[d], S_BETA + d * DN_HEADS + 2 * p) for d, p in units]
        gc = [_col_pair(gam[d], S_DECAY + d * DN_HEADS + 2 * p) for d, p in units]
        egam = [jnp.exp(x) for x in gc]
        glast = [gc[i][last[units[i][0]]:last[units[i][0]] + 1, :] for i in nu]
        kdt = [_transpose_bf16((k[i] * jnp.exp(glast[i] - gc[i])).astype(BF16)) for i in nu]
        qk = [(_dot_nt(q[i], _bd(k[i].astype(BF16))) * dec[i]).astype(BF16) for i in nu]
        qg = [(q[i].astype(F32) * egam[i]).astype(BF16) for i in nu]
        eg = [jnp.exp(x) for x in glast]
        t = yield from _tri_inverse_levels(a, t, eye_f, split, TRI_LEVELS)
        tb = [x.astype(BF16) for x in t]
        uu = [_dot(tb[i], _bd((v[i] * bcol[i]).astype(BF16))) for i in nu]
        ww = [_dot(tb[i], _bd((k[i] * bcol[i] * egam[i]).astype(BF16))).astype(BF16) for i in nu]
        yield
        return uu, ww, kdt, eg, qk, qg

    def stage_c(s, pb, with_out):
        uu, ww, kdt, eg, qk, qg = pb
        r0 = rows_of(s)
        st = [st_s[d, p] for d, p in units]
        sb = [x.astype(BF16) for x in st]
        pred = [_dot(ww[i], sb[i]) for i in nu]
        if with_out:
            o1 = [_dot(qg[i], sb[i]) for i in nu]
        yield
        vnb = [(uu[i] - pred[i]).astype(BF16) for i in nu]
        upd = [_bd_mask(_dot(kdt[i], vnb[i])) for i in nu]
        if with_out:
            o2 = [_dot(qk[i], _bd(vnb[i])) for i in nu]
        yield
        for i, (d, p) in enumerate(units):
            st_s[d, p] = st[i] * eg[i] + upd[i]
        if with_out:
            for d in range(2):
                o_all = jnp.concatenate([o1[i] + o2[i] for i in nu if units[i][0] == d], axis=1)
                rows = pl.ds(pl.multiple_of(r0[d] - off, CHUNK), CHUNK)
                o_s[rows, :] = o_s[rows, :] + o_all

    def put(refs, vals):
        for ref, group in zip(refs, vals):
            for i, x in enumerate(group):
                ref[i] = x if x.shape[0] != 1 else jnp.broadcast_to(x, ref.shape[1:])

    def get(refs, n_rows):
        return tuple([ref[i] if nr is None else ref[i, 0:nr, :] for i in range(ref.shape[0])]
                     for ref, nr in zip(refs, n_rows))

    a_rows = (None, None, None)
    b_rows = (None, None, None, 1, None, None)

    def pipelined(s_lo, s_hi, with_out):
        def step(s, carry):
            pa, pb = get(pa_s, a_rows), get(pb_s, b_rows)
            _, rb, ra = _interleave(stage_c(s, pb, with_out), stage_b(s + 1, pa), stage_a(s + 2))
            put(pb_s, rb)
            put(pa_s, ra)
            return carry

        lax.fori_loop(s_lo, s_hi, step, 0)

    (ra,) = _interleave(stage_a(0))
    rb, ra = _interleave(stage_b(0, ra), stage_a(1))
    put(pb_s, rb)
    put(pa_s, ra)
    if need_ctx:
        pipelined(0, n_chunks - 2, True)
    else:
        pipelined(0, n_ctx, False)
        pipelined(n_ctx, n_chunks - 2, True)
    _, rb = _interleave(stage_c(n_chunks - 2, get(pb_s, b_rows), True),
                        stage_b(n_chunks - 1, get(pa_s, a_rows)))
    _interleave(stage_c(n_chunks - 1, rb, True))

    li = lax.broadcasted_iota(jnp.int32, (PAIR_W, PAIR_W), 0) // DN_DK
    lj = lax.broadcasted_iota(jnp.int32, (PAIR_W, PAIR_W), 1) // DN_DK
    ones_bd = (li == lj).astype(BF16)

    def finalize(t, carry):
        ro = pl.multiple_of(t * CHUNK, CHUNK)
        o_all = o_s[pl.ds(ro, CHUNK), :]
        z = z_ref[0, pl.ds(pl.multiple_of(ro + off, CHUNK), CHUNK), :].astype(F32)
        ys = [_head_rms(pair(o_all, p), og_ref[...], ones_bd) for p in range(n_pairs)]
        y_ref[0, pl.ds(ro, CHUNK), :] = (jnp.concatenate(ys, axis=1) * _silu(z)).astype(BF16)
        return carry

    lax.fori_loop(0, lo // CHUNK, finalize, 0)


def _gdn_scan_call(qkvn, u, us, alog_pad, dtb_pad, o_gain, need_ctx):
    bsz, lt, _ = qkvn.shape
    lo = lt if need_ctx else lt - CTX_LEN
    return pl.pallas_call(
        functools.partial(_gdn_scan_kernel, need_ctx=need_ctx),
        out_shape=jax.ShapeDtypeStruct((bsz, lo, DN_VW), BF16),
        grid=(bsz,),
        in_specs=[pl.BlockSpec((1, lt, 3 * DN_KW), lambda b: (b, 0, 0)),
                  pl.BlockSpec((1, lt, DN_VW), lambda b: (b, 0, U_DNZ // DN_VW)),
                  pl.BlockSpec((1, lt, SMALL_W), lambda b: (b, 0, 0)),
                  pl.BlockSpec((1, SMALL_W), lambda b: (0, 0)),
                  pl.BlockSpec((1, SMALL_W), lambda b: (0, 0)),
                  pl.BlockSpec((1, PAIR_W), lambda b: (0, 0))],
        out_specs=pl.BlockSpec((1, lo, DN_VW), lambda b: (b, 0, 0)),
        scratch_shapes=[pltpu.VMEM((lt, SMALL_W), F32),
                        pltpu.VMEM((2, lt, SMALL_W), F32),
                        pltpu.VMEM((lo, DN_VW), F32),
                        pltpu.VMEM((2, DN_HEADS // 2, PAIR_W, PAIR_W), F32),
                        pltpu.VMEM((DN_HEADS, CHUNK, PAIR_W), F32),
                        pltpu.VMEM((DN_HEADS, CHUNK, PAIR_W), F32),
                        pltpu.VMEM((DN_HEADS, CHUNK, PAIR_W), F32),
                        pltpu.VMEM((DN_HEADS, CHUNK, PAIR_W), F32),
                        pltpu.VMEM((DN_HEADS, CHUNK, PAIR_W), BF16),
                        pltpu.VMEM((DN_HEADS, PAIR_W, CHUNK), BF16),
                        pltpu.VMEM((DN_HEADS, 8, PAIR_W), F32),
                        pltpu.VMEM((DN_HEADS, CHUNK, PAIR_W), BF16),
                        pltpu.VMEM((DN_HEADS, CHUNK, PAIR_W), BF16)],
        compiler_params=_cparams(1),
        name="gdn_scan",
    )(qkvn, u, us, alog_pad, dtb_pad, jnp.tile(o_gain.astype(F32), 2).reshape(1, PAIR_W))


def _ssd_scan_kernel(xbc_ref, z_ref, us_ref, alog_ref, dtb_ref, dskip_ref, og_ref, y_ref,
                     dt_s, la_s, lam_s, y_s, st_s, *pa_s, need_ctx):
    lt = xbc_ref.shape[1]
    n_chunks = lt // CHUNK
    n_ctx = CTX_LEN // CHUNK
    off = 0 if need_ctx else CTX_LEN
    hpg = SSD_HEADS // SSD_GROUPS
    gw = SSD_DI // SSD_GROUPS

    lo = lt - off
    ppg = hpg // 2

    dt = _softplus(us_ref[0] + dtb_ref[...])
    dt_s[...] = dt
    la_s[...] = dt * (-jnp.exp(alog_ref[...]))
    _fill_chunk_cumsums(la_s, lam_s)
    st_s[...] = jnp.zeros(st_s.shape, F32)
    y_s[...] = jnp.zeros(y_s.shape, F32)

    masks = [_tri_masks(False), _tri_masks(True)]
    eye = masks[0][2]
    last = [CHUNK - 1, 0]
    groups = [(d, gi) for d in range(2) for gi in range(SSD_GROUPS)]

    units = [(d, gi, gi * ppg + r) for d, gi in groups for r in range(ppg)]
    nu = range(len(units))
    ng = range(len(groups))
    grp_of = [groups.index((d, gi)) for d, gi, _ in units]

    def rows_of(s):
        cb = jnp.where(s < n_ctx, n_ctx - 1 - s, n_chunks - 1 + n_ctx - s)
        return [pl.multiple_of(s * CHUNK, CHUNK), pl.multiple_of(cb * CHUNK, CHUNK)]

    def load_c(r0):
        return [xbc_ref[0, pl.ds(r0[d], CHUNK), SSD_DI + SSD_BC + gi * SSD_N:SSD_DI + SSD_BC + (gi + 1) * SSD_N]
                for d, gi in groups]

    def stage_a(s):
        r0 = rows_of(s)
        lam = [lam_s[d, pl.ds(r0[d], CHUNK), :] for d in range(2)]
        dtc = [dt_s[pl.ds(r, CHUNK), :] for r in r0]
        bg = [xbc_ref[0, pl.ds(r0[d], CHUNK), SSD_DI + gi * SSD_N:SSD_DI + (gi + 1) * SSD_N] for d, gi in groups]
        cg = load_c(r0)
        bgt = [_transpose_bf16(b) for b in bg]
        cb2 = [_dot_nt(cg[g], jnp.concatenate([bg[g], bg[g]], axis=0)) for g in ng]
        yield
        x = [xbc_ref[0, pl.ds(r0[d], CHUNK), pp * PAIR_W:(pp + 1) * PAIR_W].astype(F32) for d, _, pp in units]
        ln = [S_DT + d * SSD_HEADS + 2 * pp for d, _, pp in units]
        gc = [_col_pair(lam[units[i][0]], ln[i]) for i in nu]
        xdt = [x[i] * _col_pair(dtc[units[i][0]], ln[i]) for i in nu]
        glast = [gc[i][last[units[i][0]]:last[units[i][0]] + 1, :] for i in nu]
        xdec = [(xdt[i] * jnp.exp(glast[i] - gc[i])).astype(BF16) for i in nu]
        gr = [_row_form(g_, eye) for g_ in gc]
        dec = [jnp.exp(jnp.where(masks[units[i][0]][0], gc[i] - gr[i], NEG)) for i in nu]
        y1 = [_dot((cb2[grp_of[i]] * dec[i]).astype(BF16), _bd(xdt[i].astype(BF16))) for i in nu]
        yield
        return bgt, xdec, y1, [jnp.exp(g_) for g_ in gc], [jnp.exp(g_) for g_ in glast]

    def stage_c(s, pa, with_out):
        bgt, xdec, y1, egc, eg = pa
        r0 = rows_of(s)
        ht = [st_s[d, pp] for d, _, pp in units]
        upd = [_dot(bgt[grp_of[i]], xdec[i]) for i in nu]
        if with_out:
            cg = load_c(r0)
            y2 = [_dot(cg[grp_of[i]], ht[i].astype(BF16)) for i in nu]
        yield
        for i, (d, _, pp) in enumerate(units):
            st_s[d, pp] = ht[i] * eg[i] + upd[i]
        if with_out:
            for g, (d, gi) in enumerate(groups):
                y_grp = jnp.concatenate([y1[i] + y2[i] * egc[i] for i in nu if grp_of[i] == g], axis=1)
                rows = pl.ds(pl.multiple_of(r0[d] - off, CHUNK), CHUNK)
                y_s[rows, gi * gw:(gi + 1) * gw] = y_s[rows, gi * gw:(gi + 1) * gw] + y_grp

    def put(vals):
        for ref, group in zip(pa_s, vals):
            for i, v in enumerate(group):
                ref[i] = v if v.shape[0] != 1 else jnp.broadcast_to(v, ref.shape[1:])

    def get():
        return tuple([ref[i] if nr is None else ref[i, 0:nr, :] for i in range(ref.shape[0])]
                     for ref, nr in zip(pa_s, (None, None, None, None, 1)))

    def pipelined(s_lo, s_hi, with_out):
        def step(s, carry):
            _, ra = _interleave(stage_c(s, get(), with_out), stage_a(s + 1))
            put(ra)
            return carry

        lax.fori_loop(s_lo, s_hi, step, 0)

    put(_interleave(stage_a(0))[0])
    if need_ctx:
        pipelined(0, n_chunks - 1, True)
    else:
        pipelined(0, n_ctx, False)
        pipelined(n_ctx, n_chunks - 1, True)
    _interleave(stage_c(n_chunks - 1, get(), True))

    def finalize(t, carry):
        ro = pl.multiple_of(t * CHUNK, CHUNK)
        ri = pl.multiple_of(ro + off, CHUNK)
        xs = xbc_ref[0, pl.ds(ri, CHUNK), 0:SSD_DI].astype(F32)
        y_all = (y_s[pl.ds(ro, CHUNK), :] + xs * dskip_ref[...]) * _silu(z_ref[0, pl.ds(ri, CHUNK), :].astype(F32))
        ys = []
        for gi in range(SSD_GROUPS):
            yg = y_all[:, gi * gw:(gi + 1) * gw]
            ms = jnp.mean(yg * yg, axis=-1, keepdims=True)
            ys.append(yg * lax.rsqrt(ms + EPS) * og_ref[:, gi * gw:(gi + 1) * gw])
        y_ref[0, pl.ds(ro, CHUNK), :] = jnp.concatenate(ys, axis=1).astype(BF16)
        return carry

    lax.fori_loop(0, lo // CHUNK, finalize, 0)


def _ssd_scan_call(xbcn, u, us, alog_pad, dtb_pad, d_skip, o_gain, need_ctx):
    bsz, lt, _ = xbcn.shape
    lo = lt if need_ctx else lt - CTX_LEN
    dskip = jnp.repeat(d_skip.astype(F32), SSD_P).reshape(1, SSD_DI)
    return pl.pallas_call(
        functools.partial(_ssd_scan_kernel, need_ctx=need_ctx),
        out_shape=jax.ShapeDtypeStruct((bsz, lo, SSD_DI), BF16),
        grid=(bsz,),
        in_specs=[pl.BlockSpec((1, lt, SSD_XBC), lambda b: (b, 0, 0)),
                  pl.BlockSpec((1, lt, SSD_DI), lambda b: (b, 0, U_SSDZ // SSD_DI)),
                  pl.BlockSpec((1, lt, SMALL_W), lambda b: (b, 0, 0)),
                  pl.BlockSpec((1, SMALL_W), lambda b: (0, 0)),
                  pl.BlockSpec((1, SMALL_W), lambda b: (0, 0)),
                  pl.BlockSpec((1, SSD_DI), lambda b: (0, 0)),
                  pl.BlockSpec((1, SSD_DI), lambda b: (0, 0))],
        out_specs=pl.BlockSpec((1, lo, SSD_DI), lambda b: (b, 0, 0)),
        scratch_shapes=[pltpu.VMEM((lt, SMALL_W), F32),
                        pltpu.VMEM((lt, SMALL_W), F32),
                        pltpu.VMEM((2, lt, SMALL_W), F32),
                        pltpu.VMEM((lo, SSD_DI), F32),
                        pltpu.VMEM((2, SSD_HEADS // 2, SSD_N, PAIR_W), F32),
                        pltpu.VMEM((2 * SSD_GROUPS, SSD_N, CHUNK), BF16),
                        pltpu.VMEM((SSD_HEADS, CHUNK, PAIR_W), BF16),
                        pltpu.VMEM((SSD_HEADS, CHUNK, PAIR_W), F32),
                        pltpu.VMEM((SSD_HEADS, CHUNK, PAIR_W), F32),
                        pltpu.VMEM((SSD_HEADS, 8, PAIR_W), F32)],
        compiler_params=_cparams(1),
        name="ssd_scan",
    )(xbcn, u, us, alog_pad, dtb_pad, dskip, o_gain.astype(F32).reshape(1, SSD_DI))


def _post_kernel(ya_ref, yb_ref, yc_ref, gt_ref, x_ref, ctx_ref, gate1_ref, shift_ref, scale_ref, gate2_ref,
                 g_ref, wpa_ref, wpb_ref, wpc_ref, wout_ref, w1_ref, w2_ref, o_ref, *, t0):
    x = _token_tile(x_ref, ctx_ref, pl.program_id(1) + t0)
    g = _sigmoid(gt_ref[0].astype(F32))
    m = (g[:, 0:D_MODEL] * _dot(ya_ref[0], wpa_ref[...])
         + g[:, D_MODEL:2 * D_MODEL] * _dot(yb_ref[0], wpb_ref[...])
         + g[:, 2 * D_MODEL:3 * D_MODEL] * _dot(yc_ref[0], wpc_ref[...]))
    x1 = x + gate1_ref[0] * _dot(m.astype(BF16), wout_ref[...])
    h = _norm_mod(x1, g_ref[...], shift_ref[0], scale_ref[0]).astype(BF16)
    fc = 1024
    acc = None
    for c in range(D_FF // fc):
        a = jnp.maximum(_dot(h, w1_ref[:, c * fc:(c + 1) * fc]), 0.0)
        part = _dot((a * a).astype(BF16), w2_ref[c * fc:(c + 1) * fc, :])
        acc = part if acc is None else acc + part
    o_ref[0] = x1 + gate2_ref[0] * acc


def _post_call(ya, yb, yc, u, stream, mod3, g, wpa, wpb, wpc, wout, w1, w2, need_ctx):
    bsz, lt, _ = u.shape
    nt = lt // TM
    t0 = 0 if need_ctx else 1
    x_specs, x_args = _stream_specs(stream, t0)

    def mod_spec(k):
        return pl.BlockSpec((1, 1, D_MODEL), lambda b, t: (_mod_row(b, t + t0, bsz), 0, k))

    return pl.pallas_call(
        functools.partial(_post_kernel, t0=t0),
        out_shape=jax.ShapeDtypeStruct((bsz, (nt - t0) * TM, D_MODEL), F32),
        grid=(bsz, nt - t0),
        in_specs=[pl.BlockSpec((1, TM, NA_W), lambda b, t: (b, t, 0)),
                  pl.BlockSpec((1, TM, DN_VW), lambda b, t: (b, t, 0)),
                  pl.BlockSpec((1, TM, SSD_DI), lambda b, t: (b, t, 0)),
                  pl.BlockSpec((1, TM, 3 * D_MODEL), lambda b, t: (b, t + t0, U_GATE // (3 * D_MODEL)))]
        + x_specs + [mod_spec(2), mod_spec(3), mod_spec(4), mod_spec(5),
                     _resident((1, D_MODEL)), _resident((NA_W, D_MODEL)), _resident((DN_VW, D_MODEL)),
                     _resident((SSD_DI, D_MODEL)), _resident((D_MODEL, D_MODEL)),
                     _resident((D_MODEL, D_FF)), _resident((D_FF, D_MODEL))],
        out_specs=pl.BlockSpec((1, TM, D_MODEL), lambda b, t: (b, t, 0)),
        compiler_params=_cparams(2),
        name="post",
    )(ya, yb, yc, u, *x_args, mod3, mod3, mod3, mod3, g, wpa, wpb, wpc, wout, w1, w2)


def _pad_lanes(pieces):
    row = jnp.zeros((SMALL_W,), F32)
    for o, v in pieces:
        row = lax.dynamic_update_slice(row, v.astype(F32).reshape(-1), (o,))
    return row.reshape(1, SMALL_W)


def _split_w_in(w):
    big = jnp.concatenate([w[:, _O_NA:_O_DNQKV], w[:, _O_DNQKV:_O_DNZ], w[:, _O_XBC:_O_SSDDT],
                           w[:, _O_DNZ:_O_DNB], w[:, _O_SSDZ:_O_XBC], w[:, _O_GATE:_O_END]], axis=1)
    small = jnp.concatenate([w[:, _O_DNB:_O_SSDZ], w[:, _O_SSDDT:_O_GATE],
                             jnp.zeros((w.shape[0], SMALL_W - 4 * DN_HEADS - 2 * SSD_HEADS), w.dtype)], axis=1)
    return big.astype(BF16), small.astype(BF16)


def kernel(x, c, ctx, c_ctx, w_ada, b_ada, norm1_g, norm2_g, w_in, na_q_gain, na_k_gain, na_rpb,
           dn_conv_w, dn_a_log, dn_dt_bias, dn_o_gain, ssd_conv_w, ssd_conv_b, ssd_a_log,
           ssd_dt_bias, ssd_d, ssd_o_gain, w_pa, w_pb, w_pc, w_out, w_ff1, w_ff2):
    bsz, seq, _ = x.shape
    assert bsz < MOD_ROWS and seq % TM == 0 and ctx.shape[1] == CTX_LEN
    n_l = w_ada.shape[0]
    cs = jnp.concatenate([c, c_ctx[None, :], jnp.zeros((MOD_ROWS - bsz - 1, D_MODEL), F32)], axis=0)
    mod = _ada_call(cs, w_ada, b_ada)
    stream = (x, ctx)
    lt = CTX_LEN + seq
    rope = _rope_tables(seq)
    zeros_w = jnp.zeros((1, 1536), F32)

    for l in range(n_l):
        need_ctx = l < n_l - 1
        mod3 = mod[l].reshape(MOD_ROWS, 1, 6 * D_MODEL)
        wb, ws = _split_w_in(w_in[l])
        u, us = _inproj_call(stream, bsz, lt, mod3, norm1_g[l].reshape(1, D_MODEL), wb, ws)

        ya = _na_call(u, na_q_gain[l], na_k_gain[l], _na_bias_table(na_rpb[l]), need_ctx)

        qkvn = _conv_call(u, U_DN, dn_conv_w[l], zeros_w, rope, True)
        alog_dn = _pad_lanes([(S_DECAY, dn_a_log[l])])
        dtb_dn = _pad_lanes([(S_DECAY, dn_dt_bias[l])])
        yb = _gdn_scan_call(qkvn, u, us, alog_dn, dtb_dn, dn_o_gain[l], need_ctx)

        xbcn = _conv_call(u, U_XBC, ssd_conv_w[l], ssd_conv_b[l], rope, False)
        alog_ssd = _pad_lanes([(S_DT, ssd_a_log[l])])
        dtb_ssd = _pad_lanes([(S_DT, ssd_dt_bias[l])])
        yc = _ssd_scan_call(xbcn, u, us, alog_ssd, dtb_ssd, ssd_d[l], ssd_o_gain[l], need_ctx)

        stream = _post_call(ya, yb, yc, u, stream, mod3, norm2_g[l].reshape(1, D_MODEL),
                            w_pa[l].astype(BF16), w_pb[l].astype(BF16), w_pc[l].astype(BF16),
                            w_out[l].astype(BF16), w_ff1[l].astype(BF16), w_ff2[l].astype(BF16), need_ctx)
    return stream
```

```python
import functools
import math

import numpy as np
import jax
import jax.numpy as jnp
from jax import lax
from jax.experimental import pallas as pl
from jax.experimental.pallas import tpu as pltpu

F32 = jnp.float32
BF16 = jnp.bfloat16

D_MODEL = 1024
DEPTH = 2
GRID_W = 64
CTX_LEN = 256
NA_HEADS = 8
NA_DIM = 64
NA_W = NA_HEADS * NA_DIM
WIN_R = 8
WIN_C = 16
DN_HEADS = 8
DN_DK = 64
DN_KW = DN_HEADS * DN_DK
DN_VW = DN_HEADS * DN_DK
SSD_HEADS = 16
SSD_P = 64
SSD_GROUPS = 2
SSD_N = 128
SSD_DI = SSD_HEADS * SSD_P
SSD_BC = SSD_GROUPS * SSD_N
SSD_XBC = SSD_DI + 2 * SSD_BC
CONV_W = 5
D_FF = 4 * D_MODEL
ROPE_THETA = 10000.0
EPS = 1e-6
NEG = -1e30

CHUNK = 64
TM = 256
HALO = 16
MOD_ROWS = 16
SMALL_W = 128

_O_NA = 0
_O_DNQKV = _O_NA + 3 * NA_W
_O_DNZ = _O_DNQKV + 2 * DN_KW + DN_VW
_O_DNB = _O_DNZ + DN_VW
_O_DNA = _O_DNB + 2 * DN_HEADS
_O_SSDZ = _O_DNA + 2 * DN_HEADS
_O_XBC = _O_SSDZ + SSD_DI
_O_SSDDT = _O_XBC + SSD_XBC
_O_GATE = _O_SSDDT + 2 * SSD_HEADS
_O_END = _O_GATE + 3 * D_MODEL

U_NA, U_DN, U_XBC, U_DNZ, U_SSDZ, U_GATE = 0, 1536, 3072, 4608, 5120, 6144
U_W = 9216
S_BETA, S_DECAY, S_DT = 0, 16, 32

VMEM_LIMIT = 56 * 1024 * 1024


def _cparams(n_axes):
    return pltpu.CompilerParams(dimension_semantics=("arbitrary",) * n_axes,
                                vmem_limit_bytes=VMEM_LIMIT)


def _dot(a, b):
    return jnp.dot(a, b, preferred_element_type=F32)


def _dot_nt(a, b):
    return lax.dot_general(a, b, (((1,), (1,)), ((), ())), preferred_element_type=F32)


def _chunk_cumsum(x, rev):
    row = lax.broadcasted_iota(jnp.int32, x.shape, 0)
    n = x.shape[0]
    sh = 1
    while sh < n:
        if rev:
            x = x + jnp.where(row < n - sh, pltpu.roll(x, shift=n - sh, axis=0), 0.0)
        else:
            x = x + jnp.where(row >= sh, pltpu.roll(x, shift=sh, axis=0), 0.0)
        sh *= 2
    return x


def _fill_chunk_cumsums(src_s, dst_s):
    def body(c, carry):
        rows = pl.ds(pl.multiple_of(c * CHUNK, CHUNK), CHUNK)
        x = src_s[rows, :]
        dst_s[0, rows, :] = _chunk_cumsum(x, False)
        dst_s[1, rows, :] = _chunk_cumsum(x, True)
        return carry

    lax.fori_loop(0, src_s.shape[0] // CHUNK, body, 0)


def _transpose_bf16(a):
    m = a.shape[1]
    eye = (lax.broadcasted_iota(jnp.int32, (m, m), 0) == lax.broadcasted_iota(jnp.int32, (m, m), 1))
    return _dot_nt(eye.astype(BF16), a).astype(BF16)


def _softplus(x):
    return jnp.maximum(x, 0.0) + jnp.log(1.0 + jnp.exp(-jnp.abs(x)))


def _sigmoid(x):
    return 1.0 / (1.0 + jnp.exp(-x))


def _silu(x):
    return x * _sigmoid(x)


def _ada_kernel(c_ref, w_ref, b_ref, o_ref):
    c = c_ref[...]
    a = _silu(c).astype(BF16)
    o_ref[0] = _dot(a, w_ref[0].astype(BF16)) + b_ref[0]


def _ada_call(cs, w_ada, b_ada):
    n_l = w_ada.shape[0]
    tn = 1536
    return pl.pallas_call(
        _ada_kernel,
        out_shape=jax.ShapeDtypeStruct((n_l, MOD_ROWS, 6 * D_MODEL), F32),
        grid=(n_l, 6 * D_MODEL // tn),
        in_specs=[pl.BlockSpec((MOD_ROWS, D_MODEL), lambda l, n: (0, 0)),
                  pl.BlockSpec((1, D_MODEL, tn), lambda l, n: (l, 0, n)),
                  pl.BlockSpec((1, 1, tn), lambda l, n: (l, 0, n))],
        out_specs=pl.BlockSpec((1, MOD_ROWS, tn), lambda l, n: (l, 0, n)),
        compiler_params=_cparams(2),
        name="ada",
    )(cs, w_ada, b_ada.reshape(n_l, 1, 6 * D_MODEL))


def _mod_row(b, t, bsz):
    return jnp.where(t == 0, bsz, b)


def _norm_mod(x, g, shift, scale):
    ms = jnp.mean(x * x, axis=-1, keepdims=True)
    y = x * lax.rsqrt(ms + EPS) * g
    return y * (1.0 + scale) + shift


def _token_tile(x_ref, ctx_ref, t):
    return jnp.where(t == 0, ctx_ref[0], x_ref[0])


def _stream_specs(stream, t0):
    ctx_spec = pl.BlockSpec((1, TM, D_MODEL), lambda b, t: (b, 0, 0))
    if isinstance(stream, tuple):
        lat, ctx = stream
        return [pl.BlockSpec((1, TM, D_MODEL), lambda b, t: (b, jnp.maximum(t + t0 - 1, 0), 0)), ctx_spec], [lat, ctx]
    return [pl.BlockSpec((1, TM, D_MODEL), lambda b, t: (b, t + t0, 0)), ctx_spec], [stream, stream]


def _resident(shape):
    return pl.BlockSpec(shape, lambda b, t: (0,) * len(shape), pipeline_mode=pl.Buffered(1))


def _inproj_kernel(x_ref, ctx_ref, shift_ref, scale_ref, g_ref, wb_ref, ws_ref, u_ref, us_ref):
    x = _token_tile(x_ref, ctx_ref, pl.program_id(1))
    h = _norm_mod(x, g_ref[...], shift_ref[0], scale_ref[0]).astype(BF16)
    cw = 1536
    for c in range(U_W // cw):
        u_ref[0, :, c * cw:(c + 1) * cw] = _dot(h, wb_ref[:, c * cw:(c + 1) * cw]).astype(BF16)
    us_ref[0] = _dot(h, ws_ref[...])


def _inproj_call(stream, bsz, lt, mod3, g, wb, ws):
    nt = lt // TM
    x_specs, x_args = _stream_specs(stream, 0)
    return pl.pallas_call(
        _inproj_kernel,
        out_shape=(jax.ShapeDtypeStruct((bsz, lt, U_W), BF16),
                   jax.ShapeDtypeStruct((bsz, lt, SMALL_W), F32)),
        grid=(bsz, nt),
        in_specs=x_specs + [
            pl.BlockSpec((1, 1, D_MODEL), lambda b, t: (_mod_row(b, t, bsz), 0, 0)),
            pl.BlockSpec((1, 1, D_MODEL), lambda b, t: (_mod_row(b, t, bsz), 0, 1)),
            _resident((1, D_MODEL)), _resident((D_MODEL, U_W)), _resident((D_MODEL, SMALL_W))],
        out_specs=(pl.BlockSpec((1, TM, U_W), lambda b, t: (b, t, 0)),
                   pl.BlockSpec((1, TM, SMALL_W), lambda b, t: (b, t, 0))),
        compiler_params=_cparams(2),
        name="inproj",
    )(*x_args, mod3, mod3, g, wb, ws)


def _head_rms(x, gain, bd):
    xx = x * x
    hi = xx.astype(BF16)
    lo = (xx - hi.astype(F32)).astype(BF16)
    ss = _dot(hi, bd) + _dot(lo, bd)
    return x * lax.rsqrt(ss * (1.0 / NA_DIM) + EPS) * gain


def _na_kernel(q_ref, k_ref, v_ref, qg_ref, kg_ref, bias_ref, y_ref, qn_s, kn_s, *, need_ctx):
    lt = q_ref.shape[1]
    rows = (lt - CTX_LEN) // GRID_W
    wr = min(WIN_R, rows)
    off = CTX_LEN if need_ctx else 0
    li = lax.broadcasted_iota(jnp.int32, (128, 128), 0) // NA_DIM
    lj = lax.broadcasted_iota(jnp.int32, (128, 128), 1) // NA_DIM
    bd = (li == lj).astype(BF16)
    scale = NA_DIM ** -0.5

    def norm_tile(i, _):
        r0 = pl.multiple_of(i * TM, TM)
        kn_s[pl.ds(r0, TM), :] = _head_rms(k_ref[0, pl.ds(r0, TM), :].astype(F32), kg_ref[...], bd).astype(BF16)
        qn_s[pl.ds(r0, TM), :] = (_head_rms(q_ref[0, pl.ds(r0, TM), :].astype(F32), qg_ref[...], bd)
                                  * scale).astype(BF16)
        return 0

    lax.fori_loop(0, lt // TM, norm_tile, 0)

    lane = lax.broadcasted_iota(jnp.int32, (1, 128), 1)
    head_mask = [lane < NA_DIM, lane >= NA_DIM]
    def stack_heads(q):
        z = jnp.zeros_like(q)
        return jnp.concatenate([jnp.where(head_mask[0], q, z), jnp.where(head_mask[1], q, z)], axis=0)

    def unstack_heads(o):
        r = o.shape[0] // 2
        return jnp.where(head_mask[0], o[0:r], o[r:])

    def attend(probs):
        ss = [[_dot_nt(p[0], kk) for kk in p[1]] for p in probs]
        ss = [[s if b is None else s + b for s, b in zip(sl, p[3])] for sl, p in zip(ss, probs)]
        ms = [functools.reduce(jnp.maximum, [s.max(axis=-1, keepdims=True) for s in sl]) for sl in ss]
        es = [[jnp.exp(s - m) for s in sl] for sl, m in zip(ss, ms)]
        dens = [functools.reduce(jnp.add, [e.sum(axis=-1, keepdims=True) for e in el]) for el in es]
        accs = [functools.reduce(jnp.add, [_dot(e.astype(BF16), vv) for e, vv in zip(el, p[2])])
                for el, p in zip(es, probs)]
        return [unstack_heads(a * (1.0 / d)) for a, d in zip(accs, dens)]

    rows_per_step = 4

    def row_body(i, _):
        probs = []
        for j in range(rows_per_step):
            r = i * rows_per_step + j
            ws = jnp.clip(r - wr // 2, 0, rows - wr)
            cls = ws - r + (WIN_R - 1)
            q_r = qn_s[pl.ds(pl.multiple_of(CTX_LEN + r * GRID_W, GRID_W), GRID_W), :]
            k0 = pl.multiple_of(CTX_LEN + ws * GRID_W, GRID_W)
            kw = kn_s[pl.ds(k0, wr * GRID_W), :]
            vw = v_ref[0, pl.ds(k0, wr * GRID_W), :]
            bias = jnp.concatenate([bias_ref[0, 0, cls], bias_ref[0, 1, cls]], axis=0)
            probs.append((stack_heads(q_r), [kw, kn_s[0:CTX_LEN, :]], [vw, v_ref[0, 0:CTX_LEN, :]],
                          [bias, None]))
        for j, y in enumerate(attend(probs)):
            r = i * rows_per_step + j
            y_ref[0, pl.ds(pl.multiple_of(off + r * GRID_W, GRID_W), GRID_W), :] = y.astype(BF16)
        return 0

    lax.fori_loop(0, rows // rows_per_step, row_body, 0)

    if need_ctx:
        half = CTX_LEN // 2
        probs = [(stack_heads(qn_s[j * half:(j + 1) * half, :]), [kn_s[0:CTX_LEN, :]], [v_ref[0, 0:CTX_LEN, :]],
                  [None]) for j in range(2)]
        for j, y in enumerate(attend(probs)):
            y_ref[0, j * half:(j + 1) * half, :] = y.astype(BF16)


def _na_call(u, q_gain, k_gain, bias_tab, need_ctx):
    bsz, lt, _ = u.shape
    lo = lt if need_ctx else lt - CTX_LEN
    npair = NA_HEADS // 2
    c0 = U_NA // 128
    qg = jnp.tile(q_gain.astype(F32), 2).reshape(1, 128)
    kg = jnp.tile(k_gain.astype(F32), 2).reshape(1, 128)
    return pl.pallas_call(
        functools.partial(_na_kernel, need_ctx=need_ctx),
        out_shape=jax.ShapeDtypeStruct((bsz, lo, NA_W), BF16),
        grid=(npair, bsz),
        in_specs=[pl.BlockSpec((1, lt, 128), lambda p, b: (b, 0, c0 + p)),
                  pl.BlockSpec((1, lt, 128), lambda p, b: (b, 0, c0 + npair + p)),
                  pl.BlockSpec((1, lt, 128), lambda p, b: (b, 0, c0 + 2 * npair + p)),
                  pl.BlockSpec((1, 128), lambda p, b: (0, 0)),
                  pl.BlockSpec((1, 128), lambda p, b: (0, 0)),
                  pl.BlockSpec((1, 2, WIN_R, GRID_W, WIN_R * GRID_W), lambda p, b: (p, 0, 0, 0, 0))],
        out_specs=pl.BlockSpec((1, lo, 128), lambda p, b: (b, 0, p)),
        scratch_shapes=[pltpu.VMEM((lt, 128), BF16), pltpu.VMEM((lt, 128), BF16)],
        compiler_params=_cparams(2),
        name="na",
    )(u, u, u, qg, kg, bias_tab)


def _na_bias_table(rpb):
    q = np.arange(GRID_W)[:, None]
    kc = np.arange(GRID_W)[None, :]
    wstart = np.clip(q - WIN_C // 2, 0, GRID_W - WIN_C)
    ok = (kc >= wstart) & (kc < wstart + WIN_C)
    dc = np.clip(kc - q, 1 - WIN_C, WIN_C - 1) + (WIN_C - 1)
    onehot = (np.arange(2 * WIN_C - 1)[:, None, None] == dc[None]) & ok[None]
    toep = jnp.einsum('hdc,cqk->hdqk', rpb.astype(F32), jnp.asarray(onehot, F32),
                      precision=lax.Precision.HIGHEST)
    toep = jnp.where(ok[None, None], toep, NEG)
    t = jnp.stack([toep[:, cls:cls + WIN_R] for cls in range(WIN_R)], axis=1)
    t = jnp.transpose(t, (0, 1, 3, 2, 4)).reshape(NA_HEADS, WIN_R, GRID_W, WIN_R * GRID_W)
    return t.reshape(NA_HEADS // 2, 2, WIN_R, GRID_W, WIN_R * GRID_W)


def _rope_tables(n_lat):
    pos = np.arange(n_lat)
    quarter = DN_DK // 4
    inv = ROPE_THETA ** (-np.arange(quarter, dtype=np.float64) / quarter)
    lane = np.arange(128)
    d = lane % DN_DK
    p = np.where((d < DN_DK // 2)[None, :], (pos // GRID_W)[:, None], (pos % GRID_W)[:, None])
    ang = p * inv[d % quarter][None, :]
    first = ((d % (DN_DK // 2)) < quarter)[None, :]
    cos = np.cos(ang)
    sin_up = np.where(first, -np.sin(ang), 0.0)
    sin_dn = np.where(first, 0.0, np.sin(ang))
    return (jnp.asarray(cos, F32), jnp.asarray(sin_up, F32), jnp.asarray(sin_dn, F32))


def _conv_kernel(prev_ref, cur_ref, next_ref, w_ref, b_ref, cos_ref, sup_ref, sdn_ref, o_ref, win_s, *, gdn):
    t = pl.program_id(1)
    nt = pl.num_programs(1)
    n_c = cur_ref.shape[2]
    prev_ok = t >= 2
    next_ok = jnp.logical_and(t >= 1, t < nt - 1)
    pv = prev_ref[0]
    nx = next_ref[0]
    win_s[0:HALO, :] = jnp.where(prev_ok, pv, jnp.zeros_like(pv))
    win_s[HALO + TM:HALO + TM + HALO, :] = jnp.where(next_ok, nx, jnp.zeros_like(nx))
    win_s[HALO:HALO + TM, :] = cur_ref[0]

    sub = 64
    cb = 512
    wrows = sub + 2 * HALO
    taps = [k for k in range(CONV_W) if k != CONV_W // 2]
    ri = lax.broadcasted_iota(jnp.int32, (len(taps) * sub, wrows), 0)
    ci = lax.broadcasted_iota(jnp.int32, (len(taps) * sub, wrows), 1)
    shift = functools.reduce(lambda acc, it: jnp.where(ri // sub == it[0], it[1] - CONV_W // 2, acc),
                             list(enumerate(taps)), jnp.zeros_like(ri))
    sel = (ci == (ri % sub) + HALO + shift).astype(BF16)
    li = lax.broadcasted_iota(jnp.int32, (128, 128), 0) // DN_DK
    lj = lax.broadcasted_iota(jnp.int32, (128, 128), 1) // DN_DK
    bd = (li == lj).astype(BF16)
    is_lat = t >= 1
    def block(s, c):
        cols = slice(c * cb, (c + 1) * cb)
        shifted = _dot(sel, win_s[s * sub:s * sub + wrows, cols])
        yield
        mid = CONV_W // 2
        acc = win_s[s * sub + HALO:s * sub + HALO + sub, cols].astype(F32) * w_ref[mid:mid + 1, cols]
        for i, k in enumerate(taps):
            acc = acc + shifted[i * sub:(i + 1) * sub, :] * w_ref[k:k + 1, cols]
        acc = _silu(acc + b_ref[:, cols])
        if gdn and c * cb < 2 * DN_KW:
            xs = [acc[:, j * 128:(j + 1) * 128] for j in range(cb // 128)]
            sq = [x * x for x in xs]
            hi = [x.astype(BF16) for x in sq]
            lo = [(x - h.astype(F32)).astype(BF16) for x, h in zip(sq, hi)]
            ss = [_dot(h, bd) + _dot(l_, bd) for h, l_ in zip(hi, lo)]
            yield
            cos = cos_ref[s * sub:(s + 1) * sub, :]
            sup = sup_ref[s * sub:(s + 1) * sub, :]
            sdn = sdn_ref[s * sub:(s + 1) * sub, :]
            quarter = DN_DK // 4
            pieces = []
            for x, s2 in zip(xs, ss):
                x = x * lax.rsqrt(s2 + EPS)
                xr = (x * cos + pltpu.roll(x, shift=128 - quarter, axis=1) * sup
                      + pltpu.roll(x, shift=quarter, axis=1) * sdn)
                x = jnp.where(is_lat, xr, x)
                pieces.append(x * (DN_DK ** -0.5) if c * cb < DN_KW else x)
            acc = jnp.concatenate(pieces, axis=1)
        o_ref[0, s * sub:(s + 1) * sub, cols] = acc.astype(BF16)

    for s in range(0, TM // sub, 2):
        _interleave(*[block(s + i, c) for i in range(2) for c in range(n_c // cb)])


def _conv_call(u, col0, conv_w, conv_b, rope, gdn):
    bsz, lt, _ = u.shape
    nt = lt // TM
    width = 1536
    cblk = col0 // width
    hb = TM // HALO
    nhb = lt // HALO
    cos, sup, sdn = rope
    tab_spec = pl.BlockSpec((TM, 128), lambda b, t: (jnp.maximum(t - 1, 0), 0))
    return pl.pallas_call(
        functools.partial(_conv_kernel, gdn=gdn),
        out_shape=jax.ShapeDtypeStruct((bsz, lt, width), BF16),
        grid=(bsz, nt),
        in_specs=[pl.BlockSpec((1, HALO, width), lambda b, t: (b, jnp.maximum(t * hb - 1, 0), cblk)),
                  pl.BlockSpec((1, TM, width), lambda b, t: (b, t, cblk)),
                  pl.BlockSpec((1, HALO, width), lambda b, t: (b, jnp.minimum((t + 1) * hb, nhb - 1), cblk)),
                  pl.BlockSpec((CONV_W, width), lambda b, t: (0, 0)),
                  pl.BlockSpec((1, width), lambda b, t: (0, 0)),
                  tab_spec, tab_spec, tab_spec],
        out_specs=pl.BlockSpec((1, TM, width), lambda b, t: (b, t, 0)),
        scratch_shapes=[pltpu.VMEM((TM + 2 * HALO, width), BF16)],
        compiler_params=_cparams(2),
        name="conv_gdn" if gdn else "conv_ssd",
    )(u, u, u, conv_w.astype(F32), conv_b.astype(F32).reshape(1, width), cos, sup, sdn)


PAIR_W = 2 * CHUNK


def _pair_iotas():
    ii = lax.broadcasted_iota(jnp.int32, (CHUNK, PAIR_W), 0)
    jj = lax.broadcasted_iota(jnp.int32, (CHUNK, PAIR_W), 1) & (CHUNK - 1)
    return ii, jj


def _tri_masks(rev):
    ii, jj = _pair_iotas()
    if rev:
        return ii <= jj, ii < jj, ii == jj
    return ii >= jj, ii > jj, ii == jj


def _col_pair(x, lane):
    idx = lane + lax.broadcasted_iota(jnp.int32, (CHUNK, PAIR_W), 1) // CHUNK
    return jnp.take_along_axis(x, idx, axis=1)


def _row_form(gc, eye):
    return jnp.sum(jnp.where(eye, gc, 0.0), axis=0, keepdims=True)


def _bd(y):
    r = lax.broadcasted_iota(jnp.int32, (PAIR_W, PAIR_W), 0) // CHUNK
    c = lax.broadcasted_iota(jnp.int32, (PAIR_W, PAIR_W), 1) // CHUNK
    return jnp.where(r == c, jnp.concatenate([y, y], axis=0), jnp.zeros((PAIR_W, PAIR_W), y.dtype))


def _bd_mask(x):
    r = lax.broadcasted_iota(jnp.int32, (PAIR_W, PAIR_W), 0) // CHUNK
    c = lax.broadcasted_iota(jnp.int32, (PAIR_W, PAIR_W), 1) // CHUNK
    return jnp.where(r == c, x, 0.0)


TRI_LEVELS = CHUNK.bit_length() - 1


def _tri_inverse_levels(a_list, t_list, eye_f, lev_lo, lev_hi):
    n = range(len(a_list))
    t = t_list
    if lev_lo == 0:
        ii, jj = _pair_iotas()
        base = (ii >> 1) == (jj >> 1)
        t = [eye_f - jnp.where(base, a[0:CHUNK, :].astype(F32), 0.0) for a in a_list]
    ri = lax.broadcasted_iota(jnp.int32, (PAIR_W, PAIR_W), 0)
    ci = lax.broadcasted_iota(jnp.int32, (PAIR_W, PAIR_W), 1)
    same_head = (ri // CHUNK) == (ci // CHUNK)
    ii, jj = ri & (CHUNK - 1), ci & (CHUNK - 1)
    for lev in range(max(lev_lo, 1), lev_hi):
        pair = jnp.logical_and((ii >> lev) != (jj >> lev), (ii >> (lev + 1)) == (jj >> (lev + 1)))
        pair = jnp.logical_and(pair, same_head)
        lo = [jnp.where(pair, a, jnp.zeros_like(a)) for a in a_list]
        tb = [x.astype(BF16) for x in t]
        x1 = [_dot(tb[i], lo[i]).astype(BF16) for i in n]
        yield
        x2 = [_dot(x1[i], _bd(tb[i])) for i in n]
        yield
        t = [t[i] - x2[i] for i in n]
    return t


def _interleave(*gens):
    results = [None] * len(gens)
    active = list(enumerate(gens))
    while active:
        still = []
        for idx, g in active:
            try:
                next(g)
                still.append((idx, g))
            except StopIteration as e:
                results[idx] = e.value
        active = still
    return results


def _gdn_scan_kernel(qkv_ref, z_ref, us_ref, alog_ref, dtb_ref, og_ref, y_ref, feat_s, gam_s, o_s, st_s,
                     *stage_s, need_ctx):
    pa_s, pb_s = stage_s[:3], stage_s[3:]
    lt = qkv_ref.shape[1]
    n_chunks = lt // CHUNK
    n_ctx = CTX_LEN // CHUNK
    off = 0 if need_ctx else CTX_LEN
    lo = lt - off
    n_pairs = DN_HEADS // 2

    u = us_ref[0]
    lane = lax.broadcasted_iota(jnp.int32, (1, SMALL_W), 1)
    beta = _sigmoid(u)
    g = -jnp.exp(alog_ref[...]) * _softplus(u + dtb_ref[...])
    feat_s[...] = jnp.where(lane < S_DECAY, beta, g)
    _fill_chunk_cumsums(feat_s, gam_s)
    st_s[...] = jnp.zeros(st_s.shape, F32)
    o_s[...] = jnp.zeros(o_s.shape, F32)

    masks = [_tri_masks(False), _tri_masks(True)]
    eye = masks[0][2]
    eye_f = eye.astype(F32)
    last = [CHUNK - 1, 0]
    units = [(d, p) for d in range(2) for p in range(n_pairs)]
    nu = range(len(units))

    def pair(x, p):
        return x[:, p * PAIR_W:(p + 1) * PAIR_W]

    split = TRI_LEVELS // 2

    def rows_of(s):
        cb = jnp.where(s < n_ctx, n_ctx - 1 - s, n_chunks - 1 + n_ctx - s)
        return [pl.multiple_of(s * CHUNK, CHUNK), pl.multiple_of(cb * CHUNK, CHUNK)]

    def load_pairs(r0, col0, dtype):
        blk = [qkv_ref[0, pl.ds(r, CHUNK), col0:col0 + DN_KW].astype(dtype) for r in r0]
        return [pair(blk[d], p) for d, p in units]

    def stage_a(s):
        r0 = rows_of(s)
        f = [feat_s[pl.ds(r, CHUNK), :] for r in r0]
        gam = [gam_s[d, pl.ds(r0[d], CHUNK), :] for d in range(2)]
        k = load_pairs(r0, DN_KW, F32)
        bcol = [_col_pair(f[d], S_BETA + d * DN_HEADS + 2 * p) for d, p in units]
        kb = [(k[i] * bcol[i]).astype(BF16) for i in nu]
        kbd = [_bd(x.astype(BF16)) for x in k]
        gc = [_col_pair(gam[d], S_DECAY + d * DN_HEADS + 2 * p) for d, p in units]
        gr = [_row_form(x, eye) for x in gc]
        dec = [jnp.exp(jnp.where(masks[units[i][0]][0], gc[i] - gr[i], NEG)) for i in nu]
        kk = [_dot_nt(kb[i], kbd[i]) for i in nu]
        yield
        a = [jnp.where(masks[units[i][0]][1], kk[i] * dec[i], 0.0).astype(BF16) for i in nu]
        a = [jnp.concatenate([x, x], axis=0) for x in a]
        t = yield from _tri_inverse_levels(a, None, eye_f, 0, split)
        return a, t, dec

    def stage_b(s, pa):
        a, t, dec = pa
        r0 = rows_of(s)
        f = [feat_s[pl.ds(r, CHUNK), :] for r in r0]
        gam = [gam_s[d, pl.ds(r0[d], CHUNK), :] for d in range(2)]
        k = load_pairs(r0, DN_KW, F32)
        v = load_pairs(r0, 2 * DN_KW, F32)
        q = load_pairs(r0, 0, BF16)
        bcol = [_col_pair(f[d], S_BETA + d * DN_HEADS + 2 * p) for d, p in units]
        gc = [_col_pair(gam[d], S_DECAY + d * DN_HEADS + 2 * p) for d, p in units]
        egam = [jnp.exp(x) for x in gc]
        glast = [gc[i][last[units[i][0]]:last[units[i][0]] + 1, :] for i in nu]
        kdt = [_transpose_bf16((k[i] * jnp.exp(glast[i] - gc[i])).astype(BF16)) for i in nu]
        qk = [(_dot_nt(q[i], _bd(k[i].astype(BF16))) * dec[i]).astype(BF16) for i in nu]
        qg = [(q[i].astype(F32) * egam[i]).astype(BF16) for i in nu]
        eg = [jnp.exp(x) for x in glast]
        t = yield from _tri_inverse_levels(a, t, eye_f, split, TRI_LEVELS)
        tb = [x.astype(BF16) for x in t]
        uu = [_dot(tb[i], _bd((v[i] * bcol[i]).astype(BF16))) for i in nu]
        ww = [_dot(tb[i], _bd((k[i] * bcol[i] * egam[i]).astype(BF16))).astype(BF16) for i in nu]
        yield
        return uu, ww, kdt, eg, qk, qg

    def stage_c(s, pb, with_out):
        uu, ww, kdt, eg, qk, qg = pb
        r0 = rows_of(s)
        st = [st_s[d, p] for d, p in units]
        sb = [x.astype(BF16) for x in st]
        pred = [_dot(ww[i], sb[i]) for i in nu]
        if with_out:
            o1 = [_dot(qg[i], sb[i]) for i in nu]
        yield
        vnb = [(uu[i] - pred[i]).astype(BF16) for i in nu]
        upd = [_bd_mask(_dot(kdt[i], vnb[i])) for i in nu]
        if with_out:
            o2 = [_dot(qk[i], _bd(vnb[i])) for i in nu]
        yield
        for i, (d, p) in enumerate(units):
            st_s[d, p] = st[i] * eg[i] + upd[i]
        if with_out:
            for d in range(2):
                o_all = jnp.concatenate([o1[i] + o2[i] for i in nu if units[i][0] == d], axis=1)
                rows = pl.ds(pl.multiple_of(r0[d] - off, CHUNK), CHUNK)
                o_s[rows, :] = o_s[rows, :] + o_all

    def put(refs, vals):
        for ref, group in zip(refs, vals):
            for i, x in enumerate(group):
                ref[i] = x if x.shape[0] != 1 else jnp.broadcast_to(x, ref.shape[1:])

    def get(refs, n_rows):
        return tuple([ref[i] if nr is None else ref[i, 0:nr, :] for i in range(ref.shape[0])]
                     for ref, nr in zip(refs, n_rows))

    a_rows = (None, None, None)
    b_rows = (None, None, None, 1, None, None)

    def pipelined(s_lo, s_hi, with_out):
        def step(s, carry):
            pa, pb = get(pa_s, a_rows), get(pb_s, b_rows)
            _, rb, ra = _interleave(stage_c(s, pb, with_out), stage_b(s + 1, pa), stage_a(s + 2))
            put(pb_s, rb)
            put(pa_s, ra)
            return carry

        lax.fori_loop(s_lo, s_hi, step, 0)

    (ra,) = _interleave(stage_a(0))
    rb, ra = _interleave(stage_b(0, ra), stage_a(1))
    put(pb_s, rb)
    put(pa_s, ra)
    if need_ctx:
        pipelined(0, n_chunks - 2, True)
    else:
        pipelined(0, n_ctx, False)
        pipelined(n_ctx, n_chunks - 2, True)
    _, rb = _interleave(stage_c(n_chunks - 2, get(pb_s, b_rows), True),
                        stage_b(n_chunks - 1, get(pa_s, a_rows)))
    _interleave(stage_c(n_chunks - 1, rb, True))

    li = lax.broadcasted_iota(jnp.int32, (PAIR_W, PAIR_W), 0) // DN_DK
    lj = lax.broadcasted_iota(jnp.int32, (PAIR_W, PAIR_W), 1) // DN_DK
    ones_bd = (li == lj).astype(BF16)

    def finalize(t, carry):
        ro = pl.multiple_of(t * CHUNK, CHUNK)
        o_all = o_s[pl.ds(ro, CHUNK), :]
        z = z_ref[0, pl.ds(pl.multiple_of(ro + off, CHUNK), CHUNK), :].astype(F32)
        ys = [_head_rms(pair(o_all, p), og_ref[...], ones_bd) for p in range(n_pairs)]
        y_ref[0, pl.ds(ro, CHUNK), :] = (jnp.concatenate(ys, axis=1) * _silu(z)).astype(BF16)
        return carry

    lax.fori_loop(0, lo // CHUNK, finalize, 0)


def _gdn_scan_call(qkvn, u, us, alog_pad, dtb_pad, o_gain, need_ctx):
    bsz, lt, _ = qkvn.shape
    lo = lt if need_ctx else lt - CTX_LEN
    return pl.pallas_call(
        functools.partial(_gdn_scan_kernel, need_ctx=need_ctx),
        out_shape=jax.ShapeDtypeStruct((bsz, lo, DN_VW), BF16),
        grid=(bsz,),
        in_specs=[pl.BlockSpec((1, lt, 3 * DN_KW), lambda b: (b, 0, 0)),
                  pl.BlockSpec((1, lt, DN_VW), lambda b: (b, 0, U_DNZ // DN_VW)),
                  pl.BlockSpec((1, lt, SMALL_W), lambda b: (b, 0, 0)),
                  pl.BlockSpec((1, SMALL_W), lambda b: (0, 0)),
                  pl.BlockSpec((1, SMALL_W), lambda b: (0, 0)),
                  pl.BlockSpec((1, PAIR_W), lambda b: (0, 0))],
        out_specs=pl.BlockSpec((1, lo, DN_VW), lambda b: (b, 0, 0)),
        scratch_shapes=[pltpu.VMEM((lt, SMALL_W), F32),
                        pltpu.VMEM((2, lt, SMALL_W), F32),
                        pltpu.VMEM((lo, DN_VW), F32),
                        pltpu.VMEM((2, DN_HEADS // 2, PAIR_W, PAIR_W), F32),
                        pltpu.VMEM((DN_HEADS, PAIR_W, PAIR_W), BF16),
                        pltpu.VMEM((DN_HEADS, CHUNK, PAIR_W), F32),
                        pltpu.VMEM((DN_HEADS, CHUNK, PAIR_W), F32),
                        pltpu.VMEM((DN_HEADS, CHUNK, PAIR_W), F32),
                        pltpu.VMEM((DN_HEADS, CHUNK, PAIR_W), BF16),
                        pltpu.VMEM((DN_HEADS, PAIR_W, CHUNK), BF16),
                        pltpu.VMEM((DN_HEADS, 8, PAIR_W), F32),
                        pltpu.VMEM((DN_HEADS, CHUNK, PAIR_W), BF16),
                        pltpu.VMEM((DN_HEADS, CHUNK, PAIR_W), BF16)],
        compiler_params=_cparams(1),
        name="gdn_scan",
    )(qkvn, u, us, alog_pad, dtb_pad, jnp.tile(o_gain.astype(F32), 2).reshape(1, PAIR_W))


def _ssd_scan_kernel(xbc_ref, z_ref, us_ref, alog_ref, dtb_ref, dskip_ref, og_ref, y_ref,
                     dt_s, la_s, lam_s, y_s, st_s, *pa_s, need_ctx):
    lt = xbc_ref.shape[1]
    n_chunks = lt // CHUNK
    n_ctx = CTX_LEN // CHUNK
    off = 0 if need_ctx else CTX_LEN
    hpg = SSD_HEADS // SSD_GROUPS
    gw = SSD_DI // SSD_GROUPS

    lo = lt - off
    ppg = hpg // 2

    dt = _softplus(us_ref[0] + dtb_ref[...])
    dt_s[...] = dt
    la_s[...] = dt * (-jnp.exp(alog_ref[...]))
    _fill_chunk_cumsums(la_s, lam_s)
    st_s[...] = jnp.zeros(st_s.shape, F32)
    y_s[...] = jnp.zeros(y_s.shape, F32)

    masks = [_tri_masks(False), _tri_masks(True)]
    eye = masks[0][2]
    last = [CHUNK - 1, 0]
    groups = [(d, gi) for d in range(2) for gi in range(SSD_GROUPS)]

    units = [(d, gi, gi * ppg + r) for d, gi in groups for r in range(ppg)]
    nu = range(len(units))
    ng = range(len(groups))
    grp_of = [groups.index((d, gi)) for d, gi, _ in units]

    def rows_of(s):
        cb = jnp.where(s < n_ctx, n_ctx - 1 - s, n_chunks - 1 + n_ctx - s)
        return [pl.multiple_of(s * CHUNK, CHUNK), pl.multiple_of(cb * CHUNK, CHUNK)]

    def load_c(r0):
        return [xbc_ref[0, pl.ds(r0[d], CHUNK), SSD_DI + SSD_BC + gi * SSD_N:SSD_DI + SSD_BC + (gi + 1) * SSD_N]
                for d, gi in groups]

    def stage_a(s):
        r0 = rows_of(s)
        lam = [lam_s[d, pl.ds(r0[d], CHUNK), :] for d in range(2)]
        dtc = [dt_s[pl.ds(r, CHUNK), :] for r in r0]
        bg = [xbc_ref[0, pl.ds(r0[d], CHUNK), SSD_DI + gi * SSD_N:SSD_DI + (gi + 1) * SSD_N] for d, gi in groups]
        cg = load_c(r0)
        bgt = [_transpose_bf16(b) for b in bg]
        cb2 = [_dot_nt(cg[g], jnp.concatenate([bg[g], bg[g]], axis=0)) for g in ng]
        yield
        x = [xbc_ref[0, pl.ds(r0[d], CHUNK), pp * PAIR_W:(pp + 1) * PAIR_W].astype(F32) for d, _, pp in units]
        ln = [S_DT + d * SSD_HEADS + 2 * pp for d, _, pp in units]
        gc = [_col_pair(lam[units[i][0]], ln[i]) for i in nu]
        xdt = [x[i] * _col_pair(dtc[units[i][0]], ln[i]) for i in nu]
        glast = [gc[i][last[units[i][0]]:last[units[i][0]] + 1, :] for i in nu]
        xdec = [(xdt[i] * jnp.exp(glast[i] - gc[i])).astype(BF16) for i in nu]
        gr = [_row_form(g_, eye) for g_ in gc]
        dec = [jnp.exp(jnp.where(masks[units[i][0]][0], gc[i] - gr[i], NEG)) for i in nu]
        y1 = [_dot((cb2[grp_of[i]] * dec[i]).astype(BF16), _bd(xdt[i].astype(BF16))) for i in nu]
        yield
        return bgt, xdec, y1, [jnp.exp(g_) for g_ in gc], [jnp.exp(g_) for g_ in glast]

    def stage_c(s, pa, with_out):
        bgt, xdec, y1, egc, eg = pa
        r0 = rows_of(s)
        ht = [st_s[d, pp] for d, _, pp in units]
        upd = [_dot(bgt[grp_of[i]], xdec[i]) for i in nu]
        if with_out:
            cg = load_c(r0)
            y2 = [_dot(cg[grp_of[i]], ht[i].astype(BF16)) for i in nu]
        yield
        for i, (d, _, pp) in enumerate(units):
            st_s[d, pp] = ht[i] * eg[i] + upd[i]
        if with_out:
            for g, (d, gi) in enumerate(groups):
                y_grp = jnp.concatenate([y1[i] + y2[i] * egc[i] for i in nu if grp_of[i] == g], axis=1)
                rows = pl.ds(pl.multiple_of(r0[d] - off, CHUNK), CHUNK)
                y_s[rows, gi * gw:(gi + 1) * gw] = y_s[rows, gi * gw:(gi + 1) * gw] + y_grp

    def put(vals):
        for ref, group in zip(pa_s, vals):
            for i, v in enumerate(group):
                ref[i] = v if v.shape[0] != 1 else jnp.broadcast_to(v, ref.shape[1:])

    def get():
        return tuple([ref[i] if nr is None else ref[i, 0:nr, :] for i in range(ref.shape[0])]
                     for ref, nr in zip(pa_s, (None, None, None, None, 1)))

    def pipelined(s_lo, s_hi, with_out):
        def step(s, carry):
            _, ra = _interleave(stage_c(s, get(), with_out), stage_a(s + 1))
            put(ra)
            return carry

        lax.fori_loop(s_lo, s_hi, step, 0)

    put(_interleave(stage_a(0))[0])
    if need_ctx:
        pipelined(0, n_chunks - 1, True)
    else:
        pipelined(0, n_ctx, False)
        pipelined(n_ctx, n_chunks - 1, True)
    _interleave(stage_c(n_chunks - 1, get(), True))

    def finalize(t, carry):
        ro = pl.multiple_of(t * CHUNK, CHUNK)
        ri = pl.multiple_of(ro + off, CHUNK)
        xs = xbc_ref[0, pl.ds(ri, CHUNK), 0:SSD_DI].astype(F32)
        y_all = (y_s[pl.ds(ro, CHUNK), :] + xs * dskip_ref[...]) * _silu(z_ref[0, pl.ds(ri, CHUNK), :].astype(F32))
        ys = []
        for gi in range(SSD_GROUPS):
            yg = y_all[:, gi * gw:(gi + 1) * gw]
            ms = jnp.mean(yg * yg, axis=-1, keepdims=True)
            ys.append(yg * lax.rsqrt(ms + EPS) * og_ref[:, gi * gw:(gi + 1) * gw])
        y_ref[0, pl.ds(ro, CHUNK), :] = jnp.concatenate(ys, axis=1).astype(BF16)
        return carry

    lax.fori_loop(0, lo // CHUNK, finalize, 0)


def _ssd_scan_call(xbcn, u, us, alog_pad, dtb_pad, d_skip, o_gain, need_ctx):
    bsz, lt, _ = xbcn.shape
    lo = lt if need_ctx else lt - CTX_LEN
    dskip = jnp.repeat(d_skip.astype(F32), SSD_P).reshape(1, SSD_DI)
    return pl.pallas_call(
        functools.partial(_ssd_scan_kernel, need_ctx=need_ctx),
        out_shape=jax.ShapeDtypeStruct((bsz, lo, SSD_DI), BF16),
        grid=(bsz,),
        in_specs=[pl.BlockSpec((1, lt, SSD_XBC), lambda b: (b, 0, 0)),
                  pl.BlockSpec((1, lt, SSD_DI), lambda b: (b, 0, U_SSDZ // SSD_DI)),
                  pl.BlockSpec((1, lt, SMALL_W), lambda b: (b, 0, 0)),
                  pl.BlockSpec((1, SMALL_W), lambda b: (0, 0)),
                  pl.BlockSpec((1, SMALL_W), lambda b: (0, 0)),
                  pl.BlockSpec((1, SSD_DI), lambda b: (0, 0)),
                  pl.BlockSpec((1, SSD_DI), lambda b: (0, 0))],
        out_specs=pl.BlockSpec((1, lo, SSD_DI), lambda b: (b, 0, 0)),
        scratch_shapes=[pltpu.VMEM((lt, SMALL_W), F32),
                        pltpu.VMEM((lt, SMALL_W), F32),
                        pltpu.VMEM((2, lt, SMALL_W), F32),
                        pltpu.VMEM((lo, SSD_DI), F32),
                        pltpu.VMEM((2, SSD_HEADS // 2, SSD_N, PAIR_W), F32),
                        pltpu.VMEM((2 * SSD_GROUPS, SSD_N, CHUNK), BF16),
                        pltpu.VMEM((SSD_HEADS, CHUNK, PAIR_W), BF16),
                        pltpu.VMEM((SSD_HEADS, CHUNK, PAIR_W), F32),
                        pltpu.VMEM((SSD_HEADS, CHUNK, PAIR_W), F32),
                        pltpu.VMEM((SSD_HEADS, 8, PAIR_W), F32)],
        compiler_params=_cparams(1),
        name="ssd_scan",
    )(xbcn, u, us, alog_pad, dtb_pad, dskip, o_gain.astype(F32).reshape(1, SSD_DI))


def _post_kernel(ya_ref, yb_ref, yc_ref, gt_ref, x_ref, ctx_ref, gate1_ref, shift_ref, scale_ref, gate2_ref,
                 g_ref, wpa_ref, wpb_ref, wpc_ref, wout_ref, w1_ref, w2_ref, o_ref, *, t0):
    x = _token_tile(x_ref, ctx_ref, pl.program_id(1) + t0)
    g = _sigmoid(gt_ref[0].astype(F32))
    m = (g[:, 0:D_MODEL] * _dot(ya_ref[0], wpa_ref[...])
         + g[:, D_MODEL:2 * D_MODEL] * _dot(yb_ref[0], wpb_ref[...])
         + g[:, 2 * D_MODEL:3 * D_MODEL] * _dot(yc_ref[0], wpc_ref[...]))
    x1 = x + gate1_ref[0] * _dot(m.astype(BF16), wout_ref[...])
    h = _norm_mod(x1, g_ref[...], shift_ref[0], scale_ref[0]).astype(BF16)
    fc = 1024
    acc = None
    for c in range(D_FF // fc):
        a = jnp.maximum(_dot(h, w1_ref[:, c * fc:(c + 1) * fc]), 0.0)
        part = _dot((a * a).astype(BF16), w2_ref[c * fc:(c + 1) * fc, :])
        acc = part if acc is None else acc + part
    o_ref[0] = x1 + gate2_ref[0] * acc


def _post_call(ya, yb, yc, u, stream, mod3, g, wpa, wpb, wpc, wout, w1, w2, need_ctx):
    bsz, lt, _ = u.shape
    nt = lt // TM
    t0 = 0 if need_ctx else 1
    x_specs, x_args = _stream_specs(stream, t0)

    def mod_spec(k):
        return pl.BlockSpec((1, 1, D_MODEL), lambda b, t: (_mod_row(b, t + t0, bsz), 0, k))

    return pl.pallas_call(
        functools.partial(_post_kernel, t0=t0),
        out_shape=jax.ShapeDtypeStruct((bsz, (nt - t0) * TM, D_MODEL), F32),
        grid=(bsz, nt - t0),
        in_specs=[pl.BlockSpec((1, TM, NA_W), lambda b, t: (b, t, 0)),
                  pl.BlockSpec((1, TM, DN_VW), lambda b, t: (b, t, 0)),
                  pl.BlockSpec((1, TM, SSD_DI), lambda b, t: (b, t, 0)),
                  pl.BlockSpec((1, TM, 3 * D_MODEL), lambda b, t: (b, t + t0, U_GATE // (3 * D_MODEL)))]
        + x_specs + [mod_spec(2), mod_spec(3), mod_spec(4), mod_spec(5),
                     _resident((1, D_MODEL)), _resident((NA_W, D_MODEL)), _resident((DN_VW, D_MODEL)),
                     _resident((SSD_DI, D_MODEL)), _resident((D_MODEL, D_MODEL)),
                     _resident((D_MODEL, D_FF)), _resident((D_FF, D_MODEL))],
        out_specs=pl.BlockSpec((1, TM, D_MODEL), lambda b, t: (b, t, 0)),
        compiler_params=_cparams(2),
        name="post",
    )(ya, yb, yc, u, *x_args, mod3, mod3, mod3, mod3, g, wpa, wpb, wpc, wout, w1, w2)


def _pad_lanes(pieces):
    row = jnp.zeros((SMALL_W,), F32)
    for o, v in pieces:
        row = lax.dynamic_update_slice(row, v.astype(F32).reshape(-1), (o,))
    return row.reshape(1, SMALL_W)


def _split_w_in(w):
    big = jnp.concatenate([w[:, _O_NA:_O_DNQKV], w[:, _O_DNQKV:_O_DNZ], w[:, _O_XBC:_O_SSDDT],
                           w[:, _O_DNZ:_O_DNB], w[:, _O_SSDZ:_O_XBC], w[:, _O_GATE:_O_END]], axis=1)
    small = jnp.concatenate([w[:, _O_DNB:_O_SSDZ], w[:, _O_SSDDT:_O_GATE],
                             jnp.zeros((w.shape[0], SMALL_W - 4 * DN_HEADS - 2 * SSD_HEADS), w.dtype)], axis=1)
    return big.astype(BF16), small.astype(BF16)


def kernel(x, c, ctx, c_ctx, w_ada, b_ada, norm1_g, norm2_g, w_in, na_q_gain, na_k_gain, na_rpb,
           dn_conv_w, dn_a_log, dn_dt_bias, dn_o_gain, ssd_conv_w, ssd_conv_b, ssd_a_log,
           ssd_dt_bias, ssd_d, ssd_o_gain, w_pa, w_pb, w_pc, w_out, w_ff1, w_ff2):
    bsz, seq, _ = x.shape
    assert bsz < MOD_ROWS and seq % TM == 0 and ctx.shape[1] == CTX_LEN
    n_l = w_ada.shape[0]
    cs = jnp.concatenate([c, c_ctx[None, :], jnp.zeros((MOD_ROWS - bsz - 1, D_MODEL), F32)], axis=0)
    mod = _ada_call(cs, w_ada, b_ada)
    stream = (x, ctx)
    lt = CTX_LEN + seq
    rope = _rope_tables(seq)
    zeros_w = jnp.zeros((1, 1536), F32)

    for l in range(n_l):
        need_ctx = l < n_l - 1
        mod3 = mod[l].reshape(MOD_ROWS, 1, 6 * D_MODEL)
        wb, ws = _split_w_in(w_in[l])
        u, us = _inproj_call(stream, bsz, lt, mod3, norm1_g[l].reshape(1, D_MODEL), wb, ws)

        ya = _na_call(u, na_q_gain[l], na_k_gain[l], _na_bias_table(na_rpb[l]), need_ctx)

        qkvn = _conv_call(u, U_DN, dn_conv_w[l], zeros_w, rope, True)
        alog_dn = _pad_lanes([(S_DECAY, dn_a_log[l])])
        dtb_dn = _pad_lanes([(S_DECAY, dn_dt_bias[l])])
        yb = _gdn_scan_call(qkvn, u, us, alog_dn, dtb_dn, dn_o_gain[l], need_ctx)

        xbcn = _conv_call(u, U_XBC, ssd_conv_w[l], ssd_conv_b[l], rope, False)
        alog_ssd = _pad_lanes([(S_DT, ssd_a_log[l])])
        dtb_ssd = _pad_lanes([(S_DT, ssd_dt_bias[l])])
        yc = _ssd_scan_call(xbcn, u, us, alog_ssd, dtb_ssd, ssd_d[l], ssd_o_gain[l], need_ctx)

        stream = _post_call(ya, yb, yc, u, stream, mod3, norm2_g[l].reshape(1, D_MODEL),
                            w_pa[l].astype(BF16), w_pb[l].astype(BF16), w_pc[l].astype(BF16),
                            w_out[l].astype(BF16), w_ff1[l].astype(BF16), w_ff2[l].astype(BF16), need_ctx)
    return stream
```

```python
import functools
import math

import numpy as np
import jax
import jax.numpy as jnp
from jax import lax
from jax.experimental import pallas as pl
from jax.experimental.pallas import tpu as pltpu

F32 = jnp.float32
BF16 = jnp.bfloat16

D_MODEL = 1024
DEPTH = 2
GRID_W = 64
CTX_LEN = 256
NA_HEADS = 8
NA_DIM = 64
NA_W = NA_HEADS * NA_DIM
WIN_R = 8
WIN_C = 16
DN_HEADS = 8
DN_DK = 64
DN_KW = DN_HEADS * DN_DK
DN_VW = DN_HEADS * DN_DK
SSD_HEADS = 16
SSD_P = 64
SSD_GROUPS = 2
SSD_N = 128
SSD_DI = SSD_HEADS * SSD_P
SSD_BC = SSD_GROUPS * SSD_N
SSD_XBC = SSD_DI + 2 * SSD_BC
CONV_W = 5
D_FF = 4 * D_MODEL
ROPE_THETA = 10000.0
EPS = 1e-6
NEG = -1e30

CHUNK = 64
TM = 256
HALO = 16
MOD_ROWS = 16
SMALL_W = 128

_O_NA = 0
_O_DNQKV = _O_NA + 3 * NA_W
_O_DNZ = _O_DNQKV + 2 * DN_KW + DN_VW
_O_DNB = _O_DNZ + DN_VW
_O_DNA = _O_DNB + 2 * DN_HEADS
_O_SSDZ = _O_DNA + 2 * DN_HEADS
_O_XBC = _O_SSDZ + SSD_DI
_O_SSDDT = _O_XBC + SSD_XBC
_O_GATE = _O_SSDDT + 2 * SSD_HEADS
_O_END = _O_GATE + 3 * D_MODEL

U_NA, U_DN, U_XBC, U_DNZ, U_SSDZ, U_GATE = 0, 1536, 3072, 4608, 5120, 6144
U_W = 9216
S_BETA, S_DECAY, S_DT = 0, 16, 32

VMEM_LIMIT = 56 * 1024 * 1024


def _cparams(n_axes):
    return pltpu.CompilerParams(dimension_semantics=("arbitrary",) * n_axes,
                                vmem_limit_bytes=VMEM_LIMIT)


def _dot(a, b):
    return jnp.dot(a, b, preferred_element_type=F32)


def _dot_nt(a, b):
    return lax.dot_general(a, b, (((1,), (1,)), ((), ())), preferred_element_type=F32)


def _chunk_cumsum(x, rev):
    row = lax.broadcasted_iota(jnp.int32, x.shape, 0)
    n = x.shape[0]
    sh = 1
    while sh < n:
        if rev:
            x = x + jnp.where(row < n - sh, pltpu.roll(x, shift=n - sh, axis=0), 0.0)
        else:
            x = x + jnp.where(row >= sh, pltpu.roll(x, shift=sh, axis=0), 0.0)
        sh *= 2
    return x


def _fill_chunk_cumsums(src_s, dst_s):
    def body(c, carry):
        rows = pl.ds(pl.multiple_of(c * CHUNK, CHUNK), CHUNK)
        x = src_s[rows, :]
        dst_s[0, rows, :] = _chunk_cumsum(x, False)
        dst_s[1, rows, :] = _chunk_cumsum(x, True)
        return carry

    lax.fori_loop(0, src_s.shape[0] // CHUNK, body, 0)


def _transpose_bf16(a):
    m = a.shape[1]
    eye = (lax.broadcasted_iota(jnp.int32, (m, m), 0) == lax.broadcasted_iota(jnp.int32, (m, m), 1))
    return _dot_nt(eye.astype(BF16), a).astype(BF16)


def _softplus(x):
    return jnp.maximum(x, 0.0) + jnp.log(1.0 + jnp.exp(-jnp.abs(x)))


def _sigmoid(x):
    return 1.0 / (1.0 + jnp.exp(-x))


def _silu(x):
    return x * _sigmoid(x)


def _ada_kernel(c_ref, w_ref, b_ref, o_ref):
    c = c_ref[...]
    a = _silu(c).astype(BF16)
    o_ref[0] = _dot(a, w_ref[0].astype(BF16)) + b_ref[0]


def _ada_call(cs, w_ada, b_ada):
    n_l = w_ada.shape[0]
    tn = 1536
    return pl.pallas_call(
        _ada_kernel,
        out_shape=jax.ShapeDtypeStruct((n_l, MOD_ROWS, 6 * D_MODEL), F32),
        grid=(n_l, 6 * D_MODEL // tn),
        in_specs=[pl.BlockSpec((MOD_ROWS, D_MODEL), lambda l, n: (0, 0)),
                  pl.BlockSpec((1, D_MODEL, tn), lambda l, n: (l, 0, n)),
                  pl.BlockSpec((1, 1, tn), lambda l, n: (l, 0, n))],
        out_specs=pl.BlockSpec((1, MOD_ROWS, tn), lambda l, n: (l, 0, n)),
        compiler_params=_cparams(2),
        name="ada",
    )(cs, w_ada, b_ada.reshape(n_l, 1, 6 * D_MODEL))


def _mod_row(b, t, bsz):
    return jnp.where(t == 0, bsz, b)


def _norm_mod(x, g, shift, scale):
    ms = jnp.mean(x * x, axis=-1, keepdims=True)
    y = x * lax.rsqrt(ms + EPS) * g
    return y * (1.0 + scale) + shift


def _token_tile(x_ref, ctx_ref, t):
    return jnp.where(t == 0, ctx_ref[0], x_ref[0])


def _stream_specs(stream, t0):
    ctx_spec = pl.BlockSpec((1, TM, D_MODEL), lambda b, t: (b, 0, 0))
    if isinstance(stream, tuple):
        lat, ctx = stream
        return [pl.BlockSpec((1, TM, D_MODEL), lambda b, t: (b, jnp.maximum(t + t0 - 1, 0), 0)), ctx_spec], [lat, ctx]
    return [pl.BlockSpec((1, TM, D_MODEL), lambda b, t: (b, t + t0, 0)), ctx_spec], [stream, stream]


def _resident(shape):
    return pl.BlockSpec(shape, lambda b, t: (0,) * len(shape), pipeline_mode=pl.Buffered(1))


def _inproj_kernel(x_ref, ctx_ref, shift_ref, scale_ref, g_ref, wb_ref, ws_ref, u_ref, us_ref):
    x = _token_tile(x_ref, ctx_ref, pl.program_id(1))
    h = _norm_mod(x, g_ref[...], shift_ref[0], scale_ref[0]).astype(BF16)
    cw = 1536
    for c in range(U_W // cw):
        u_ref[0, :, c * cw:(c + 1) * cw] = _dot(h, wb_ref[:, c * cw:(c + 1) * cw]).astype(BF16)
    us_ref[0] = _dot(h, ws_ref[...])


def _inproj_call(stream, bsz, lt, mod3, g, wb, ws):
    nt = lt // TM
    x_specs, x_args = _stream_specs(stream, 0)
    return pl.pallas_call(
        _inproj_kernel,
        out_shape=(jax.ShapeDtypeStruct((bsz, lt, U_W), BF16),
                   jax.ShapeDtypeStruct((bsz, lt, SMALL_W), F32)),
        grid=(bsz, nt),
        in_specs=x_specs + [
            pl.BlockSpec((1, 1, D_MODEL), lambda b, t: (_mod_row(b, t, bsz), 0, 0)),
            pl.BlockSpec((1, 1, D_MODEL), lambda b, t: (_mod_row(b, t, bsz), 0, 1)),
            _resident((1, D_MODEL)), _resident((D_MODEL, U_W)), _resident((D_MODEL, SMALL_W))],
        out_specs=(pl.BlockSpec((1, TM, U_W), lambda b, t: (b, t, 0)),
                   pl.BlockSpec((1, TM, SMALL_W), lambda b, t: (b, t, 0))),
        compiler_params=_cparams(2),
        name="inproj",
    )(*x_args, mod3, mod3, g, wb, ws)


def _head_rms(x, gain, bd):
    xx = x * x
    hi = xx.astype(BF16)
    lo = (xx - hi.astype(F32)).astype(BF16)
    ss = _dot(hi, bd) + _dot(lo, bd)
    return x * lax.rsqrt(ss * (1.0 / NA_DIM) + EPS) * gain


def _na_kernel(q_ref, k_ref, v_ref, qg_ref, kg_ref, bias_ref, y_ref, qn_s, kn_s, *, need_ctx):
    lt = q_ref.shape[1]
    rows = (lt - CTX_LEN) // GRID_W
    wr = min(WIN_R, rows)
    off = CTX_LEN if need_ctx else 0
    li = lax.broadcasted_iota(jnp.int32, (128, 128), 0) // NA_DIM
    lj = lax.broadcasted_iota(jnp.int32, (128, 128), 1) // NA_DIM
    bd = (li == lj).astype(BF16)
    scale = NA_DIM ** -0.5

    def norm_tile(i, _):
        r0 = pl.multiple_of(i * TM, TM)
        kn_s[pl.ds(r0, TM), :] = _head_rms(k_ref[0, pl.ds(r0, TM), :].astype(F32), kg_ref[...], bd).astype(BF16)
        qn_s[pl.ds(r0, TM), :] = (_head_rms(q_ref[0, pl.ds(r0, TM), :].astype(F32), qg_ref[...], bd)
                                  * scale).astype(BF16)
        return 0

    lax.fori_loop(0, lt // TM, norm_tile, 0)

    lane = lax.broadcasted_iota(jnp.int32, (1, 128), 1)
    head_mask = [lane < NA_DIM, lane >= NA_DIM]
    def stack_heads(q):
        z = jnp.zeros_like(q)
        return jnp.concatenate([jnp.where(head_mask[0], q, z), jnp.where(head_mask[1], q, z)], axis=0)

    def unstack_heads(o):
        r = o.shape[0] // 2
        return jnp.where(head_mask[0], o[0:r], o[r:])

    def attend(probs):
        ss = [[_dot_nt(p[0], kk) for kk in p[1]] for p in probs]
        ss = [[s if b is None else s + b for s, b in zip(sl, p[3])] for sl, p in zip(ss, probs)]
        ms = [functools.reduce(jnp.maximum, [s.max(axis=-1, keepdims=True) for s in sl]) for sl in ss]
        es = [[jnp.exp(s - m) for s in sl] for sl, m in zip(ss, ms)]
        dens = [functools.reduce(jnp.add, [e.sum(axis=-1, keepdims=True) for e in el]) for el in es]
        accs = [functools.reduce(jnp.add, [_dot(e.astype(BF16), vv) for e, vv in zip(el, p[2])])
                for el, p in zip(es, probs)]
        return [unstack_heads(a * (1.0 / d)) for a, d in zip(accs, dens)]

    rows_per_step = 4

    def row_body(i, _):
        probs = []
        for j in range(rows_per_step):
            r = i * rows_per_step + j
            ws = jnp.clip(r - wr // 2, 0, rows - wr)
            cls = ws - r + (WIN_R - 1)
            q_r = qn_s[pl.ds(pl.multiple_of(CTX_LEN + r * GRID_W, GRID_W), GRID_W), :]
            k0 = pl.multiple_of(CTX_LEN + ws * GRID_W, GRID_W)
            kw = kn_s[pl.ds(k0, wr * GRID_W), :]
            vw = v_ref[0, pl.ds(k0, wr * GRID_W), :]
            bias = jnp.concatenate([bias_ref[0, 0, cls], bias_ref[0, 1, cls]], axis=0)
            probs.append((stack_heads(q_r), [kw, kn_s[0:CTX_LEN, :]], [vw, v_ref[0, 0:CTX_LEN, :]],
                          [bias, None]))
        for j, y in enumerate(attend(probs)):
            r = i * rows_per_step + j
            y_ref[0, pl.ds(pl.multiple_of(off + r * GRID_W, GRID_W), GRID_W), :] = y.astype(BF16)
        return 0

    lax.fori_loop(0, rows // rows_per_step, row_body, 0)

    if need_ctx:
        half = CTX_LEN // 2
        probs = [(stack_heads(qn_s[j * half:(j + 1) * half, :]), [kn_s[0:CTX_LEN, :]], [v_ref[0, 0:CTX_LEN, :]],
                  [None]) for j in range(2)]
        for j, y in enumerate(attend(probs)):
            y_ref[0, j * half:(j + 1) * half, :] = y.astype(BF16)


def _na_call(u, q_gain, k_gain, bias_tab, need_ctx):
    bsz, lt, _ = u.shape
    lo = lt if need_ctx else lt - CTX_LEN
    npair = NA_HEADS // 2
    c0 = U_NA // 128
    qg = jnp.tile(q_gain.astype(F32), 2).reshape(1, 128)
    kg = jnp.tile(k_gain.astype(F32), 2).reshape(1, 128)
    return pl.pallas_call(
        functools.partial(_na_kernel, need_ctx=need_ctx),
        out_shape=jax.ShapeDtypeStruct((bsz, lo, NA_W), BF16),
        grid=(npair, bsz),
        in_specs=[pl.BlockSpec((1, lt, 128), lambda p, b: (b, 0, c0 + p)),
                  pl.BlockSpec((1, lt, 128), lambda p, b: (b, 0, c0 + npair + p)),
                  pl.BlockSpec((1, lt, 128), lambda p, b: (b, 0, c0 + 2 * npair + p)),
                  pl.BlockSpec((1, 128), lambda p, b: (0, 0)),
                  pl.BlockSpec((1, 128), lambda p, b: (0, 0)),
                  pl.BlockSpec((1, 2, WIN_R, GRID_W, WIN_R * GRID_W), lambda p, b: (p, 0, 0, 0, 0))],
        out_specs=pl.BlockSpec((1, lo, 128), lambda p, b: (b, 0, p)),
        scratch_shapes=[pltpu.VMEM((lt, 128), BF16), pltpu.VMEM((lt, 128), BF16)],
        compiler_params=_cparams(2),
        name="na",
    )(u, u, u, qg, kg, bias_tab)


def _na_bias_table(rpb):
    q = np.arange(GRID_W)[:, None]
    kc = np.arange(GRID_W)[None, :]
    wstart = np.clip(q - WIN_C // 2, 0, GRID_W - WIN_C)
    ok = (kc >= wstart) & (kc < wstart + WIN_C)
    dc = np.clip(kc - q, 1 - WIN_C, WIN_C - 1) + (WIN_C - 1)
    onehot = (np.arange(2 * WIN_C - 1)[:, None, None] == dc[None]) & ok[None]
    toep = jnp.einsum('lhdc,cqk->lhdqk', rpb.astype(F32), jnp.asarray(onehot, F32),
                      precision=lax.Precision.HIGHEST)
    toep = jnp.where(ok, toep, NEG)
    t = jnp.stack([toep[:, :, cls:cls + WIN_R] for cls in range(WIN_R)], axis=2)
    t = jnp.transpose(t, (0, 1, 2, 4, 3, 5))
    return t.reshape(rpb.shape[0], NA_HEADS // 2, 2, WIN_R, GRID_W, WIN_R * GRID_W)


def _rope_tables(n_lat):
    pos = np.arange(n_lat)
    quarter = DN_DK // 4
    inv = ROPE_THETA ** (-np.arange(quarter, dtype=np.float64) / quarter)
    lane = np.arange(128)
    d = lane % DN_DK
    p = np.where((d < DN_DK // 2)[None, :], (pos // GRID_W)[:, None], (pos % GRID_W)[:, None])
    ang = p * inv[d % quarter][None, :]
    first = ((d % (DN_DK // 2)) < quarter)[None, :]
    cos = np.cos(ang)
    sin_up = np.where(first, -np.sin(ang), 0.0)
    sin_dn = np.where(first, 0.0, np.sin(ang))
    return (jnp.asarray(cos, F32), jnp.asarray(sin_up, F32), jnp.asarray(sin_dn, F32))


def _conv_kernel(prev_ref, cur_ref, next_ref, w_ref, b_ref, cos_ref, sup_ref, sdn_ref, o_ref, win_s, *, gdn):
    t = pl.program_id(1)
    nt = pl.num_programs(1)
    n_c = cur_ref.shape[2]
    prev_ok = t >= 2
    next_ok = jnp.logical_and(t >= 1, t < nt - 1)
    pv = prev_ref[0]
    nx = next_ref[0]
    win_s[0:HALO, :] = jnp.where(prev_ok, pv, jnp.zeros_like(pv))
    win_s[HALO + TM:HALO + TM + HALO, :] = jnp.where(next_ok, nx, jnp.zeros_like(nx))
    win_s[HALO:HALO + TM, :] = cur_ref[0]

    sub = 64
    cb = 512
    wrows = sub + 2 * HALO
    taps = [k for k in range(CONV_W) if k != CONV_W // 2]
    ri = lax.broadcasted_iota(jnp.int32, (len(taps) * sub, wrows), 0)
    ci = lax.broadcasted_iota(jnp.int32, (len(taps) * sub, wrows), 1)
    shift = functools.reduce(lambda acc, it: jnp.where(ri // sub == it[0], it[1] - CONV_W // 2, acc),
                             list(enumerate(taps)), jnp.zeros_like(ri))
    sel = (ci == (ri % sub) + HALO + shift).astype(BF16)
    li = lax.broadcasted_iota(jnp.int32, (128, 128), 0) // DN_DK
    lj = lax.broadcasted_iota(jnp.int32, (128, 128), 1) // DN_DK
    bd = (li == lj).astype(BF16)
    is_lat = t >= 1
    def block(s, c):
        cols = slice(c * cb, (c + 1) * cb)
        shifted = _dot(sel, win_s[s * sub:s * sub + wrows, cols])
        yield
        mid = CONV_W // 2
        acc = win_s[s * sub + HALO:s * sub + HALO + sub, cols].astype(F32) * w_ref[mid:mid + 1, cols]
        for i, k in enumerate(taps):
            acc = acc + shifted[i * sub:(i + 1) * sub, :] * w_ref[k:k + 1, cols]
        acc = _silu(acc + b_ref[:, cols])
        if gdn and c * cb < 2 * DN_KW:
            xs = [acc[:, j * 128:(j + 1) * 128] for j in range(cb // 128)]
            sq = [x * x for x in xs]
            hi = [x.astype(BF16) for x in sq]
            lo = [(x - h.astype(F32)).astype(BF16) for x, h in zip(sq, hi)]
            ss = [_dot(h, bd) + _dot(l_, bd) for h, l_ in zip(hi, lo)]
            yield
            cos = cos_ref[s * sub:(s + 1) * sub, :]
            sup = sup_ref[s * sub:(s + 1) * sub, :]
            sdn = sdn_ref[s * sub:(s + 1) * sub, :]
            quarter = DN_DK // 4
            pieces = []
            for x, s2 in zip(xs, ss):
                x = x * lax.rsqrt(s2 + EPS)
                xr = (x * cos + pltpu.roll(x, shift=128 - quarter, axis=1) * sup
                      + pltpu.roll(x, shift=quarter, axis=1) * sdn)
                x = jnp.where(is_lat, xr, x)
                pieces.append(x * (DN_DK ** -0.5) if c * cb < DN_KW else x)
            acc = jnp.concatenate(pieces, axis=1)
        o_ref[0, s * sub:(s + 1) * sub, cols] = acc.astype(BF16)

    for s in range(0, TM // sub, 2):
        _interleave(*[block(s + i, c) for i in range(2) for c in range(n_c // cb)])


def _conv_call(u, col0, conv_w, conv_b, rope, gdn):
    bsz, lt, _ = u.shape
    nt = lt // TM
    width = 1536
    cblk = col0 // width
    hb = TM // HALO
    nhb = lt // HALO
    cos, sup, sdn = rope
    tab_spec = pl.BlockSpec((TM, 128), lambda b, t: (jnp.maximum(t - 1, 0), 0))
    return pl.pallas_call(
        functools.partial(_conv_kernel, gdn=gdn),
        out_shape=jax.ShapeDtypeStruct((bsz, lt, width), BF16),
        grid=(bsz, nt),
        in_specs=[pl.BlockSpec((1, HALO, width), lambda b, t: (b, jnp.maximum(t * hb - 1, 0), cblk)),
                  pl.BlockSpec((1, TM, width), lambda b, t: (b, t, cblk)),
                  pl.BlockSpec((1, HALO, width), lambda b, t: (b, jnp.minimum((t + 1) * hb, nhb - 1), cblk)),
                  pl.BlockSpec((CONV_W, width), lambda b, t: (0, 0)),
                  pl.BlockSpec((1, width), lambda b, t: (0, 0)),
                  tab_spec, tab_spec, tab_spec],
        out_specs=pl.BlockSpec((1, TM, width), lambda b, t: (b, t, 0)),
        scratch_shapes=[pltpu.VMEM((TM + 2 * HALO, width), BF16)],
        compiler_params=_cparams(2),
        name="conv_gdn" if gdn else "conv_ssd",
    )(u, u, u, conv_w.astype(F32), conv_b.astype(F32).reshape(1, width), cos, sup, sdn)


PAIR_W = 2 * CHUNK


def _pair_iotas():
    ii = lax.broadcasted_iota(jnp.int32, (CHUNK, PAIR_W), 0)
    jj = lax.broadcasted_iota(jnp.int32, (CHUNK, PAIR_W), 1) & (CHUNK - 1)
    return ii, jj


def _tri_masks(rev):
    ii, jj = _pair_iotas()
    if rev:
        return ii <= jj, ii < jj, ii == jj
    return ii >= jj, ii > jj, ii == jj


def _col_pair(x, lane):
    idx = lane + lax.broadcasted_iota(jnp.int32, (CHUNK, PAIR_W), 1) // CHUNK
    return jnp.take_along_axis(x, idx, axis=1)


def _row_form(gc, eye):
    return jnp.sum(jnp.where(eye, gc, 0.0), axis=0, keepdims=True)


def _bd(y):
    r = lax.broadcasted_iota(jnp.int32, (PAIR_W, PAIR_W), 0) // CHUNK
    c = lax.broadcasted_iota(jnp.int32, (PAIR_W, PAIR_W), 1) // CHUNK
    return jnp.where(r == c, jnp.concatenate([y, y], axis=0), jnp.zeros((PAIR_W, PAIR_W), y.dtype))


def _bd_mask(x):
    r = lax.broadcasted_iota(jnp.int32, (PAIR_W, PAIR_W), 0) // CHUNK
    c = lax.broadcasted_iota(jnp.int32, (PAIR_W, PAIR_W), 1) // CHUNK
    return jnp.where(r == c, x, 0.0)


TRI_LEVELS = CHUNK.bit_length() - 1


def _tri_inverse_levels(a_list, t_list, eye_f, lev_lo, lev_hi):
    n = range(len(a_list))
    t = t_list
    if lev_lo == 0:
        ii, jj = _pair_iotas()
        base = (ii >> 1) == (jj >> 1)
        t = [eye_f - jnp.where(base, a[0:CHUNK, :].astype(F32), 0.0) for a in a_list]
    ri = lax.broadcasted_iota(jnp.int32, (PAIR_W, PAIR_W), 0)
    ci = lax.broadcasted_iota(jnp.int32, (PAIR_W, PAIR_W), 1)
    same_head = (ri // CHUNK) == (ci // CHUNK)
    ii, jj = ri & (CHUNK - 1), ci & (CHUNK - 1)
    for lev in range(max(lev_lo, 1), lev_hi):
        pair = jnp.logical_and((ii >> lev) != (jj >> lev), (ii >> (lev + 1)) == (jj >> (lev + 1)))
        pair = jnp.logical_and(pair, same_head)
        lo = [jnp.where(pair, a, jnp.zeros_like(a)) for a in a_list]
        tb = [x.astype(BF16) for x in t]
        x1 = [_dot(tb[i], lo[i]).astype(BF16) for i in n]
        yield
        x2 = [_dot(x1[i], _bd(tb[i])) for i in n]
        yield
        t = [t[i] - x2[i] for i in n]
    return t


def _interleave(*gens):
    results = [None] * len(gens)
    active = list(enumerate(gens))
    while active:
        still = []
        for idx, g in active:
            try:
                next(g)
                still.append((idx, g))
            except StopIteration as e:
                results[idx] = e.value
        active = still
    return results


def _gdn_scan_kernel(qkv_ref, z_ref, us_ref, alog_ref, dtb_ref, og_ref, y_ref, feat_s, gam_s, o_s, st_s,
                     *stage_s, need_ctx):
    pa_s, pb_s = stage_s[:3], stage_s[3:]
    lt = qkv_ref.shape[1]
    n_chunks = lt // CHUNK
    n_ctx = CTX_LEN // CHUNK
    off = 0 if need_ctx else CTX_LEN
    lo = lt - off
    n_pairs = DN_HEADS // 2

    u = us_ref[0]
    lane = lax.broadcasted_iota(jnp.int32, (1, SMALL_W), 1)
    beta = _sigmoid(u)
    g = -jnp.exp(alog_ref[...]) * _softplus(u + dtb_ref[...])
    feat_s[...] = jnp.where(lane < S_DECAY, beta, g)
    _fill_chunk_cumsums(feat_s, gam_s)
    st_s[...] = jnp.zeros(st_s.shape, F32)
    o_s[...] = jnp.zeros(o_s.shape, F32)

    masks = [_tri_masks(False), _tri_masks(True)]
    eye = masks[0][2]
    eye_f = eye.astype(F32)
    last = [CHUNK - 1, 0]
    units = [(d, p) for d in range(2) for p in range(n_pairs)]
    nu = range(len(units))

    def pair(x, p):
        return x[:, p * PAIR_W:(p + 1) * PAIR_W]

    split = TRI_LEVELS // 2

    def rows_of(s):
        cb = jnp.where(s < n_ctx, n_ctx - 1 - s, n_chunks - 1 + n_ctx - s)
        return [pl.multiple_of(s * CHUNK, CHUNK), pl.multiple_of(cb * CHUNK, CHUNK)]

    def load_pairs(r0, col0, dtype):
        blk = [qkv_ref[0, pl.ds(r, CHUNK), col0:col0 + DN_KW].astype(dtype) for r in r0]
        return [pair(blk[d], p) for d, p in units]

    def stage_a(s):
        r0 = rows_of(s)
        f = [feat_s[pl.ds(r, CHUNK), :] for r in r0]
        gam = [gam_s[d, pl.ds(r0[d], CHUNK), :] for d in range(2)]
        k = load_pairs(r0, DN_KW, F32)
        q = load_pairs(r0, 0, BF16)
        bcol = [_col_pair(f[d], S_BETA + d * DN_HEADS + 2 * p) for d, p in units]
        kbq = [jnp.concatenate([(k[i] * bcol[i]).astype(BF16), q[i]], axis=0) for i in nu]
        kbd = [_bd(x.astype(BF16)) for x in k]
        gc = [_col_pair(gam[d], S_DECAY + d * DN_HEADS + 2 * p) for d, p in units]
        gr = [_row_form(x, eye) for x in gc]
        dec = [jnp.exp(jnp.where(masks[units[i][0]][0], gc[i] - gr[i], NEG)) for i in nu]
        kkq = [_dot_nt(kbq[i], kbd[i]) for i in nu]
        yield
        a = [jnp.where(masks[units[i][0]][1], kkq[i][0:CHUNK] * dec[i], 0.0).astype(BF16) for i in nu]
        a = [jnp.concatenate([x, x], axis=0) for x in a]
        qk = [(kkq[i][CHUNK:] * dec[i]).astype(BF16) for i in nu]
        t = yield from _tri_inverse_levels(a, None, eye_f, 0, split)
        return a, t, qk

    def stage_b(s, pa):
        a, t, qk = pa
        r0 = rows_of(s)
        f = [feat_s[pl.ds(r, CHUNK), :] for r in r0]
        gam = [gam_s[d, pl.ds(r0[d], CHUNK), :] for d in range(2)]
        k = load_pairs(r0, DN_KW, F32)
        v = load_pairs(r0, 2 * DN_KW, F32)
        q = load_pairs(r0, 0, BF16)
        bcol = [_col_pair(f[d], S_BETA + d * DN_HEADS + 2 * p) for d, p in units]
        gc = [_col_pair(gam[d], S_DECAY + d * DN_HEADS + 2 * p) for d, p in units]
        egam = [jnp.exp(x) for x in gc]
        glast = [gc[i][last[units[i][0]]:last[units[i][0]] + 1, :] for i in nu]
        kdt = [jnp.transpose(k[i] * jnp.exp(glast[i] - gc[i])).astype(BF16) for i in nu]
        qg =[(q[i].astype(F32) * egam[i]).astype(BF16) for i in nu]
        eg = [jnp.exp(x) for x in glast]
        t = yield from _tri_inverse_levels(a, t, eye_f, split, TRI_LEVELS)
        tb = [x.astype(BF16) for x in t]
        rhs = [jnp.concatenate([_bd((v[i] * bcol[i]).astype(BF16)),
                                _bd((k[i] * bcol[i] * egam[i]).astype(BF16))], axis=1) for i in nu]
        uw = [_dot(tb[i], rhs[i]) for i in nu]
        yield
        uu = [x[:, 0:PAIR_W] for x in uw]
        wq = [jnp.concatenate([uw[i][:, PAIR_W:].astype(BF16), qg[i]], axis=0) for i in nu]
        return uu, wq, kdt, eg, qk

    def stage_c(s, pb, with_out):
        uu, wq, kdt, eg, qk = pb
        r0 = rows_of(s)
        st = [st_s[d, p] for d, p in units]
        sb = [x.astype(BF16) for x in st]
        if with_out:
            po = [_dot(wq[i], sb[i]) for i in nu]
            pred = [x[0:CHUNK] for x in po]
            o1 = [x[CHUNK:] for x in po]
        else:
            pred = [_dot(wq[i][0:CHUNK], sb[i]) for i in nu]
        yield
        vnb = [(uu[i] - pred[i]).astype(BF16) for i in nu]
        upd = [_bd_mask(_dot(kdt[i], vnb[i])) for i in nu]
        if with_out:
            o2 = [_dot(qk[i], _bd(vnb[i])) for i in nu]
        yield
        for i, (d, p) in enumerate(units):
            st_s[d, p] = st[i] * eg[i] + upd[i]
        if with_out:
            for d in range(2):
                o_all = jnp.concatenate([o1[i] + o2[i] for i in nu if units[i][0] == d], axis=1)
                rows = pl.ds(pl.multiple_of(r0[d] - off, CHUNK), CHUNK)
                o_s[rows, :] = o_s[rows, :] + o_all

    def put(refs, vals):
        for ref, group in zip(refs, vals):
            for i, x in enumerate(group):
                ref[i] = x if x.shape[0] != 1 else jnp.broadcast_to(x, ref.shape[1:])

    def get(refs, n_rows):
        return tuple([ref[i] if nr is None else ref[i, 0:nr, :] for i in range(ref.shape[0])]
                     for ref, nr in zip(refs, n_rows))

    a_rows = (None, None, None)
    b_rows = (None, None, None, 1, None)

    def pipelined(s_lo, s_hi, with_out):
        def step(s, carry):
            pa, pb = get(pa_s, a_rows), get(pb_s, b_rows)
            _, rb, ra = _interleave(stage_c(s, pb, with_out), stage_b(s + 1, pa), stage_a(s + 2))
            put(pb_s, rb)
            put(pa_s, ra)
            return carry

        lax.fori_loop(s_lo, s_hi, step, 0)

    (ra,) = _interleave(stage_a(0))
    rb, ra = _interleave(stage_b(0, ra), stage_a(1))
    put(pb_s, rb)
    put(pa_s, ra)
    if need_ctx:
        pipelined(0, n_chunks - 2, True)
    else:
        pipelined(0, n_ctx, False)
        pipelined(n_ctx, n_chunks - 2, True)
    _, rb = _interleave(stage_c(n_chunks - 2, get(pb_s, b_rows), True),
                        stage_b(n_chunks - 1, get(pa_s, a_rows)))
    _interleave(stage_c(n_chunks - 1, rb, True))

    li = lax.broadcasted_iota(jnp.int32, (PAIR_W, PAIR_W), 0) // DN_DK
    lj = lax.broadcasted_iota(jnp.int32, (PAIR_W, PAIR_W), 1) // DN_DK
    ones_bd = (li == lj).astype(BF16)

    def finalize(t, carry):
        ro = pl.multiple_of(t * CHUNK, CHUNK)
        o_all = o_s[pl.ds(ro, CHUNK), :]
        z = z_ref[0, pl.ds(pl.multiple_of(ro + off, CHUNK), CHUNK), :].astype(F32)
        ys = [_head_rms(pair(o_all, p), og_ref[...], ones_bd) for p in range(n_pairs)]
        y_ref[0, pl.ds(ro, CHUNK), :] = (jnp.concatenate(ys, axis=1) * _silu(z)).astype(BF16)
        return carry

    lax.fori_loop(0, lo // CHUNK, finalize, 0)


def _gdn_scan_call(qkvn, u, us, alog_pad, dtb_pad, o_gain, need_ctx):
    bsz, lt, _ = qkvn.shape
    lo = lt if need_ctx else lt - CTX_LEN
    return pl.pallas_call(
        functools.partial(_gdn_scan_kernel, need_ctx=need_ctx),
        out_shape=jax.ShapeDtypeStruct((bsz, lo, DN_VW), BF16),
        grid=(bsz,),
        in_specs=[pl.BlockSpec((1, lt, 3 * DN_KW), lambda b: (b, 0, 0)),
                  pl.BlockSpec((1, lt, DN_VW), lambda b: (b, 0, U_DNZ // DN_VW)),
                  pl.BlockSpec((1, lt, SMALL_W), lambda b: (b, 0, 0)),
                  pl.BlockSpec((1, SMALL_W), lambda b: (0, 0)),
                  pl.BlockSpec((1, SMALL_W), lambda b: (0, 0)),
                  pl.BlockSpec((1, PAIR_W), lambda b: (0, 0))],
        out_specs=pl.BlockSpec((1, lo, DN_VW), lambda b: (b, 0, 0)),
        scratch_shapes=[pltpu.VMEM((lt, SMALL_W), F32),
                        pltpu.VMEM((2, lt, SMALL_W), F32),
                        pltpu.VMEM((lo, DN_VW), F32),
                        pltpu.VMEM((2, DN_HEADS // 2, PAIR_W, PAIR_W), F32),
                        pltpu.VMEM((DN_HEADS, PAIR_W, PAIR_W), BF16),
                        pltpu.VMEM((DN_HEADS, CHUNK, PAIR_W), F32),
                        pltpu.VMEM((DN_HEADS, CHUNK, PAIR_W), BF16),
                        pltpu.VMEM((DN_HEADS, CHUNK, PAIR_W), F32),
                        pltpu.VMEM((DN_HEADS, PAIR_W, PAIR_W), BF16),
                        pltpu.VMEM((DN_HEADS, PAIR_W, CHUNK), BF16),
                        pltpu.VMEM((DN_HEADS, 8, PAIR_W), F32),
                        pltpu.VMEM((DN_HEADS, CHUNK, PAIR_W), BF16)],
        compiler_params=_cparams(1),
        name="gdn_scan",
    )(qkvn, u, us, alog_pad, dtb_pad, jnp.tile(o_gain.astype(F32), 2).reshape(1, PAIR_W))


def _ssd_scan_kernel(xbc_ref, z_ref, us_ref, alog_ref, dtb_ref, dskip_ref, og_ref, y_ref,
                     dt_s, la_s, lam_s, y_s, st_s, *pa_s, need_ctx):
    lt = xbc_ref.shape[1]
    n_chunks = lt // CHUNK
    n_ctx = CTX_LEN // CHUNK
    off = 0 if need_ctx else CTX_LEN
    hpg = SSD_HEADS // SSD_GROUPS
    gw = SSD_DI // SSD_GROUPS

    lo = lt - off
    ppg = hpg // 2

    dt = _softplus(us_ref[0] + dtb_ref[...])
    dt_s[...] = dt
    la_s[...] = dt * (-jnp.exp(alog_ref[...]))
    _fill_chunk_cumsums(la_s, lam_s)
    st_s[...] = jnp.zeros(st_s.shape, F32)
    y_s[...] = jnp.zeros(y_s.shape, F32)

    masks = [_tri_masks(False), _tri_masks(True)]
    eye = masks[0][2]
    last = [CHUNK - 1, 0]
    groups = [(d, gi) for d in range(2) for gi in range(SSD_GROUPS)]

    units = [(d, gi, gi * ppg + r) for d, gi in groups for r in range(ppg)]
    nu = range(len(units))
    ng = range(len(groups))
    grp_of = [groups.index((d, gi)) for d, gi, _ in units]

    def rows_of(s):
        cb = jnp.where(s < n_ctx, n_ctx - 1 - s, n_chunks - 1 + n_ctx - s)
        return [pl.multiple_of(s * CHUNK, CHUNK), pl.multiple_of(cb * CHUNK, CHUNK)]

    def load_c(r0):
        return [xbc_ref[0, pl.ds(r0[d], CHUNK), SSD_DI + SSD_BC + gi * SSD_N:SSD_DI + SSD_BC + (gi + 1) * SSD_N]
                for d, gi in groups]

    def stage_a(s):
        r0 = rows_of(s)
        lam = [lam_s[d, pl.ds(r0[d], CHUNK), :] for d in range(2)]
        dtc = [dt_s[pl.ds(r, CHUNK), :] for r in r0]
        bg = [xbc_ref[0, pl.ds(r0[d], CHUNK), SSD_DI + gi * SSD_N:SSD_DI + (gi + 1) * SSD_N] for d, gi in groups]
        cg = load_c(r0)
        bgt = [_transpose_bf16(b) for b in bg]
        cb2 = [_dot_nt(cg[g], jnp.concatenate([bg[g], bg[g]], axis=0)) for g in ng]
        yield
        x = [xbc_ref[0, pl.ds(r0[d], CHUNK), pp * PAIR_W:(pp + 1) * PAIR_W].astype(F32) for d, _, pp in units]
        ln = [S_DT + d * SSD_HEADS + 2 * pp for d, _, pp in units]
        gc = [_col_pair(lam[units[i][0]], ln[i]) for i in nu]
        xdt = [x[i] * _col_pair(dtc[units[i][0]], ln[i]) for i in nu]
        glast = [gc[i][last[units[i][0]]:last[units[i][0]] + 1, :] for i in nu]
        xdec = [(xdt[i] * jnp.exp(glast[i] - gc[i])).astype(BF16) for i in nu]
        gr = [_row_form(g_, eye) for g_ in gc]
        dec = [jnp.exp(jnp.where(masks[units[i][0]][0], gc[i] - gr[i], NEG)) for i in nu]
        y1 = [_dot((cb2[grp_of[i]] * dec[i]).astype(BF16), _bd(xdt[i].astype(BF16))) for i in nu]
        yield
        return bgt, xdec, y1, [jnp.exp(g_) for g_ in gc], [jnp.exp(g_) for g_ in glast]

    def stage_c(s, pa, with_out):
        bgt, xdec, y1, egc, eg = pa
        r0 = rows_of(s)
        ht = [st_s[d, pp] for d, _, pp in units]
        upd = [_dot(bgt[grp_of[i]], xdec[i]) for i in nu]
        if with_out:
            cg = load_c(r0)
            y2 = [_dot(cg[grp_of[i]], ht[i].astype(BF16)) for i in nu]
        yield
        for i, (d, _, pp) in enumerate(units):
            st_s[d, pp] = ht[i] * eg[i] + upd[i]
        if with_out:
            for g, (d, gi) in enumerate(groups):
                y_grp = jnp.concatenate([y1[i] + y2[i] * egc[i] for i in nu if grp_of[i] == g], axis=1)
                rows = pl.ds(pl.multiple_of(r0[d] - off, CHUNK), CHUNK)
                y_s[rows, gi * gw:(gi + 1) * gw] = y_s[rows, gi * gw:(gi + 1) * gw] + y_grp

    def put(vals):
        for ref, group in zip(pa_s, vals):
            for i, v in enumerate(group):
                ref[i] = v if v.shape[0] != 1 else jnp.broadcast_to(v, ref.shape[1:])

    def get():
        return tuple([ref[i] if nr is None else ref[i, 0:nr, :] for i in range(ref.shape[0])]
                     for ref, nr in zip(pa_s, (None, None, None, None, 1)))

    def pipelined(s_lo, s_hi, with_out):
        def step(s, carry):
            _, ra = _interleave(stage_c(s, get(), with_out), stage_a(s + 1))
            put(ra)
            return carry

        lax.fori_loop(s_lo, s_hi, step, 0)

    put(_interleave(stage_a(0))[0])
    if need_ctx:
        pipelined(0, n_chunks - 1, True)
    else:
        pipelined(0, n_ctx, False)
        pipelined(n_ctx, n_chunks - 1, True)
    _interleave(stage_c(n_chunks - 1, get(), True))

    def finalize(t, carry):
        ro = pl.multiple_of(t * CHUNK, CHUNK)
        ri = pl.multiple_of(ro + off, CHUNK)
        xs = xbc_ref[0, pl.ds(ri, CHUNK), 0:SSD_DI].astype(F32)
        y_all = (y_s[pl.ds(ro, CHUNK), :] + xs * dskip_ref[...]) * _silu(z_ref[0, pl.ds(ri, CHUNK), :].astype(F32))
        ys = []
        for gi in range(SSD_GROUPS):
            yg = y_all[:, gi * gw:(gi + 1) * gw]
            ms = jnp.mean(yg * yg, axis=-1, keepdims=True)
            ys.append(yg * lax.rsqrt(ms + EPS) * og_ref[:, gi * gw:(gi + 1) * gw])
        y_ref[0, pl.ds(ro, CHUNK), :] = jnp.concatenate(ys, axis=1).astype(BF16)
        return carry

    lax.fori_loop(0, lo // CHUNK, finalize, 0)


def _ssd_scan_call(xbcn, u, us, alog_pad, dtb_pad, d_skip, o_gain, need_ctx):
    bsz, lt, _ = xbcn.shape
    lo = lt if need_ctx else lt - CTX_LEN
    dskip = jnp.repeat(d_skip.astype(F32), SSD_P).reshape(1, SSD_DI)
    return pl.pallas_call(
        functools.partial(_ssd_scan_kernel, need_ctx=need_ctx),
        out_shape=jax.ShapeDtypeStruct((bsz, lo, SSD_DI), BF16),
        grid=(bsz,),
        in_specs=[pl.BlockSpec((1, lt, SSD_XBC), lambda b: (b, 0, 0)),
                  pl.BlockSpec((1, lt, SSD_DI), lambda b: (b, 0, U_SSDZ // SSD_DI)),
                  pl.BlockSpec((1, lt, SMALL_W), lambda b: (b, 0, 0)),
                  pl.BlockSpec((1, SMALL_W), lambda b: (0, 0)),
                  pl.BlockSpec((1, SMALL_W), lambda b: (0, 0)),
                  pl.BlockSpec((1, SSD_DI), lambda b: (0, 0)),
                  pl.BlockSpec((1, SSD_DI), lambda b: (0, 0))],
        out_specs=pl.BlockSpec((1, lo, SSD_DI), lambda b: (b, 0, 0)),
        scratch_shapes=[pltpu.VMEM((lt, SMALL_W), F32),
                        pltpu.VMEM((lt, SMALL_W), F32),
                        pltpu.VMEM((2, lt, SMALL_W), F32),
                        pltpu.VMEM((lo, SSD_DI), F32),
                        pltpu.VMEM((2, SSD_HEADS // 2, SSD_N, PAIR_W), F32),
                        pltpu.VMEM((2 * SSD_GROUPS, SSD_N, CHUNK), BF16),
                        pltpu.VMEM((SSD_HEADS, CHUNK, PAIR_W), BF16),
                        pltpu.VMEM((SSD_HEADS, CHUNK, PAIR_W), F32),
                        pltpu.VMEM((SSD_HEADS, CHUNK, PAIR_W), F32),
                        pltpu.VMEM((SSD_HEADS, 8, PAIR_W), F32)],
        compiler_params=_cparams(1),
        name="ssd_scan",
    )(xbcn, u, us, alog_pad, dtb_pad, dskip, o_gain.astype(F32).reshape(1, SSD_DI))


def _post_kernel(ya_ref, yb_ref, yc_ref, gt_ref, x_ref, ctx_ref, gate1_ref, shift_ref, scale_ref, gate2_ref,
                 g_ref, wpa_ref, wpb_ref, wpc_ref, wout_ref, w1_ref, w2_ref, o_ref, *, t0):
    x = _token_tile(x_ref, ctx_ref, pl.program_id(1) + t0)
    g = _sigmoid(gt_ref[0].astype(F32))
    m = (g[:, 0:D_MODEL] * _dot(ya_ref[0], wpa_ref[...])
         + g[:, D_MODEL:2 * D_MODEL] * _dot(yb_ref[0], wpb_ref[...])
         + g[:, 2 * D_MODEL:3 * D_MODEL] * _dot(yc_ref[0], wpc_ref[...]))
    x1 = x + gate1_ref[0] * _dot(m.astype(BF16), wout_ref[...])
    h = _norm_mod(x1, g_ref[...], shift_ref[0], scale_ref[0]).astype(BF16)
    fc = 1024
    acc = None
    for c in range(D_FF // fc):
        a = jnp.maximum(_dot(h, w1_ref[:, c * fc:(c + 1) * fc]), 0.0)
        part = _dot((a * a).astype(BF16), w2_ref[c * fc:(c + 1) * fc, :])
        acc = part if acc is None else acc + part
    o_ref[0] = x1 + gate2_ref[0] * acc


def _post_call(ya, yb, yc, u, stream, mod3, g, wpa, wpb, wpc, wout, w1, w2, need_ctx):
    bsz, lt, _ = u.shape
    nt = lt // TM
    t0 = 0 if need_ctx else 1
    x_specs, x_args = _stream_specs(stream, t0)

    def mod_spec(k):
        return pl.BlockSpec((1, 1, D_MODEL), lambda b, t: (_mod_row(b, t + t0, bsz), 0, k))

    return pl.pallas_call(
        functools.partial(_post_kernel, t0=t0),
        out_shape=jax.ShapeDtypeStruct((bsz, (nt - t0) * TM, D_MODEL), F32),
        grid=(bsz, nt - t0),
        in_specs=[pl.BlockSpec((1, TM, NA_W), lambda b, t: (b, t, 0)),
                  pl.BlockSpec((1, TM, DN_VW), lambda b, t: (b, t, 0)),
                  pl.BlockSpec((1, TM, SSD_DI), lambda b, t: (b, t, 0)),
                  pl.BlockSpec((1, TM, 3 * D_MODEL), lambda b, t: (b, t + t0, U_GATE // (3 * D_MODEL)))]
        + x_specs + [mod_spec(2), mod_spec(3), mod_spec(4), mod_spec(5),
                     _resident((1, D_MODEL)), _resident((NA_W, D_MODEL)), _resident((DN_VW, D_MODEL)),
                     _resident((SSD_DI, D_MODEL)), _resident((D_MODEL, D_MODEL)),
                     _resident((D_MODEL, D_FF)), _resident((D_FF, D_MODEL))],
        out_specs=pl.BlockSpec((1, TM, D_MODEL), lambda b, t: (b, t, 0)),
        compiler_params=_cparams(2),
        name="post",
    )(ya, yb, yc, u, *x_args, mod3, mod3, mod3, mod3, g, wpa, wpb, wpc, wout, w1, w2)


def _pad_lanes(off, v):
    flat = v.astype(F32).reshape(v.shape[0], 1, -1)
    return jnp.pad(flat, ((0, 0), (0, 0), (off, SMALL_W - off - flat.shape[-1])))


def _split_w_in(w):
    big = jnp.concatenate([w[..., _O_NA:_O_DNQKV], w[..., _O_DNQKV:_O_DNZ], w[..., _O_XBC:_O_SSDDT],
                           w[..., _O_DNZ:_O_DNB], w[..., _O_SSDZ:_O_XBC], w[..., _O_GATE:_O_END]], axis=-1)
    small = jnp.concatenate([w[..., _O_DNB:_O_SSDZ], w[..., _O_SSDDT:_O_GATE]], axis=-1)
    small = jnp.pad(small, ((0, 0), (0, 0), (0, SMALL_W - small.shape[-1])))
    return big.astype(BF16), small.astype(BF16)


def kernel(x, c, ctx, c_ctx, w_ada, b_ada, norm1_g, norm2_g, w_in, na_q_gain, na_k_gain, na_rpb,
           dn_conv_w, dn_a_log, dn_dt_bias, dn_o_gain, ssd_conv_w, ssd_conv_b, ssd_a_log,
           ssd_dt_bias, ssd_d, ssd_o_gain, w_pa, w_pb, w_pc, w_out, w_ff1, w_ff2):
    bsz, seq, _ = x.shape
    assert bsz < MOD_ROWS and seq % TM == 0 and ctx.shape[1] == CTX_LEN
    n_l = w_ada.shape[0]
    cs = jnp.concatenate([c, c_ctx[None, :], jnp.zeros((MOD_ROWS - bsz - 1, D_MODEL), F32)], axis=0)
    mod = _ada_call(cs, w_ada, b_ada)
    stream = (x, ctx)
    lt = CTX_LEN + seq
    rope = _rope_tables(seq)
    zeros_w = jnp.zeros((1, 1536), F32)

    mod4 = mod.reshape(n_l, MOD_ROWS, 1, 6 * D_MODEL)
    wb, ws = _split_w_in(w_in)
    bias_tab = _na_bias_table(na_rpb)
    alog_dn, dtb_dn = _pad_lanes(S_DECAY, dn_a_log), _pad_lanes(S_DECAY, dn_dt_bias)
    alog_ssd, dtb_ssd = _pad_lanes(S_DT, ssd_a_log), _pad_lanes(S_DT, ssd_dt_bias)
    g1, g2 = norm1_g.reshape(n_l, 1, D_MODEL), norm2_g.reshape(n_l, 1, D_MODEL)
    post_w = [w.astype(BF16) for w in (w_pa, w_pb, w_pc, w_out, w_ff1, w_ff2)]

    for l in range(n_l):
        need_ctx = l < n_l - 1
        u, us = _inproj_call(stream, bsz, lt, mod4[l], g1[l], wb[l], ws[l])
        ya = _na_call(u, na_q_gain[l], na_k_gain[l], bias_tab[l], need_ctx)
        qkvn = _conv_call(u, U_DN, dn_conv_w[l], zeros_w, rope, True)
        yb = _gdn_scan_call(qkvn, u, us, alog_dn[l], dtb_dn[l], dn_o_gain[l], need_ctx)
        xbcn = _conv_call(u, U_XBC, ssd_conv_w[l], ssd_conv_b[l], rope, False)
        yc = _ssd_scan_call(xbcn, u, us, alog_ssd[l], dtb_ssd[l], ssd_d[l], ssd_o_gain[l], need_ctx)
        stream = _post_call(ya, yb, yc, u, stream, mod4[l], g2[l], *[w[l] for w in post_w], need_ctx)
    return stream
```

```python
import functools
import math

import numpy as np
import jax
import jax.numpy as jnp
from jax import lax
from jax.experimental import pallas as pl
from jax.experimental.pallas import tpu as pltpu

F32 = jnp.float32
BF16 = jnp.bfloat16

D_MODEL = 1024
DEPTH = 2
GRID_W = 64
CTX_LEN = 256
NA_HEADS = 8
NA_DIM = 64
NA_W = NA_HEADS * NA_DIM
WIN_R = 8
WIN_C = 16
DN_HEADS = 8
DN_DK = 64
DN_KW = DN_HEADS * DN_DK
DN_VW = DN_HEADS * DN_DK
SSD_HEADS = 16
SSD_P = 64
SSD_GROUPS = 2
SSD_N = 128
SSD_DI = SSD_HEADS * SSD_P
SSD_BC = SSD_GROUPS * SSD_N
SSD_XBC = SSD_DI + 2 * SSD_BC
CONV_W = 5
D_FF = 4 * D_MODEL
ROPE_THETA = 10000.0
EPS = 1e-6
NEG = -1e30

CHUNK = 64
TM = 256
HALO = 16
MOD_ROWS = 16
SMALL_W = 128

_O_NA = 0
_O_DNQKV = _O_NA + 3 * NA_W
_O_DNZ = _O_DNQKV + 2 * DN_KW + DN_VW
_O_DNB = _O_DNZ + DN_VW
_O_DNA = _O_DNB + 2 * DN_HEADS
_O_SSDZ = _O_DNA + 2 * DN_HEADS
_O_XBC = _O_SSDZ + SSD_DI
_O_SSDDT = _O_XBC + SSD_XBC
_O_GATE = _O_SSDDT + 2 * SSD_HEADS
_O_END = _O_GATE + 3 * D_MODEL

U_NA, U_DN, U_XBC, U_DNZ, U_SSDZ, U_GATE = 0, 1536, 3072, 4608, 5120, 6144
U_W = 9216
S_BETA, S_DECAY, S_DT = 0, 16, 32

VMEM_LIMIT = 56 * 1024 * 1024


def _cparams(n_axes):
    return pltpu.CompilerParams(dimension_semantics=("arbitrary",) * n_axes,
                                vmem_limit_bytes=VMEM_LIMIT)


def _dot(a, b):
    return jnp.dot(a, b, preferred_element_type=F32)


def _dot_nt(a, b):
    return lax.dot_general(a, b, (((1,), (1,)), ((), ())), preferred_element_type=F32)


def _chunk_cumsum(x, rev):
    row = lax.broadcasted_iota(jnp.int32, x.shape, 0)
    n = x.shape[0]
    sh = 1
    while sh < n:
        if rev:
            x = x + jnp.where(row < n - sh, pltpu.roll(x, shift=n - sh, axis=0), 0.0)
        else:
            x = x + jnp.where(row >= sh, pltpu.roll(x, shift=sh, axis=0), 0.0)
        sh *= 2
    return x


def _fill_chunk_cumsums(src_s, dst_s):
    def body(c, carry):
        rows = pl.ds(pl.multiple_of(c * CHUNK, CHUNK), CHUNK)
        x = src_s[rows, :]
        dst_s[0, rows, :] = _chunk_cumsum(x, False)
        dst_s[1, rows, :] = _chunk_cumsum(x, True)
        return carry

    lax.fori_loop(0, src_s.shape[0] // CHUNK, body, 0)


def _transpose_bf16(a):
    m = a.shape[1]
    eye = (lax.broadcasted_iota(jnp.int32, (m, m), 0) == lax.broadcasted_iota(jnp.int32, (m, m), 1))
    return _dot_nt(eye.astype(BF16), a).astype(BF16)


def _softplus(x):
    return jnp.maximum(x, 0.0) + jnp.log(1.0 + jnp.exp(-jnp.abs(x)))


def _sigmoid(x):
    return 1.0 / (1.0 + jnp.exp(-x))


def _silu(x):
    return x * _sigmoid(x)


def _ada_kernel(c_ref, w_ref, b_ref, o_ref):
    c = c_ref[...]
    a = _silu(c).astype(BF16)
    o_ref[0] = _dot(a, w_ref[0].astype(BF16)) + b_ref[0]


def _ada_call(cs, w_ada, b_ada):
    n_l = w_ada.shape[0]
    tn = 1536
    return pl.pallas_call(
        _ada_kernel,
        out_shape=jax.ShapeDtypeStruct((n_l, MOD_ROWS, 6 * D_MODEL), F32),
        grid=(n_l, 6 * D_MODEL // tn),
        in_specs=[pl.BlockSpec((MOD_ROWS, D_MODEL), lambda l, n: (0, 0)),
                  pl.BlockSpec((1, D_MODEL, tn), lambda l, n: (l, 0, n)),
                  pl.BlockSpec((1, 1, tn), lambda l, n: (l, 0, n))],
        out_specs=pl.BlockSpec((1, MOD_ROWS, tn), lambda l, n: (l, 0, n)),
        compiler_params=_cparams(2),
        name="ada",
    )(cs, w_ada, b_ada.reshape(n_l, 1, 6 * D_MODEL))


def _mod_row(b, t, bsz):
    return jnp.where(t == 0, bsz, b)


def _norm_mod(x, g, shift, scale):
    ms = jnp.mean(x * x, axis=-1, keepdims=True)
    y = x * lax.rsqrt(ms + EPS) * g
    return y * (1.0 + scale) + shift


def _token_tile(x_ref, ctx_ref, t):
    return jnp.where(t == 0, ctx_ref[0], x_ref[0])


def _stream_specs(stream, t0):
    ctx_spec = pl.BlockSpec((1, TM, D_MODEL), lambda b, t: (b, 0, 0))
    if isinstance(stream, tuple):
        lat, ctx = stream
        return [pl.BlockSpec((1, TM, D_MODEL), lambda b, t: (b, jnp.maximum(t + t0 - 1, 0), 0)), ctx_spec], [lat, ctx]
    return [pl.BlockSpec((1, TM, D_MODEL), lambda b, t: (b, t + t0, 0)), ctx_spec], [stream, stream]


def _resident(shape):
    return pl.BlockSpec(shape, lambda b, t: (0,) * len(shape), pipeline_mode=pl.Buffered(1))


def _inproj_kernel(x_ref, ctx_ref, shift_ref, scale_ref, g_ref, wb_ref, ws_ref, u_ref, us_ref):
    x = _token_tile(x_ref, ctx_ref, pl.program_id(1))
    h = _norm_mod(x, g_ref[...], shift_ref[0], scale_ref[0]).astype(BF16)
    cw = 1536
    for c in range(U_W // cw):
        u_ref[0, :, c * cw:(c + 1) * cw] = _dot(h, wb_ref[:, c * cw:(c + 1) * cw]).astype(BF16)
    us_ref[0] = _dot(h, ws_ref[...])


def _inproj_call(stream, bsz, lt, mod3, g, wb, ws):
    nt = lt // TM
    x_specs, x_args = _stream_specs(stream, 0)
    return pl.pallas_call(
        _inproj_kernel,
        out_shape=(jax.ShapeDtypeStruct((bsz, lt, U_W), BF16),
                   jax.ShapeDtypeStruct((bsz, lt, SMALL_W), F32)),
        grid=(bsz, nt),
        in_specs=x_specs + [
            pl.BlockSpec((1, 1, D_MODEL), lambda b, t: (_mod_row(b, t, bsz), 0, 0)),
            pl.BlockSpec((1, 1, D_MODEL), lambda b, t: (_mod_row(b, t, bsz), 0, 1)),
            _resident((1, D_MODEL)), _resident((D_MODEL, U_W)), _resident((D_MODEL, SMALL_W))],
        out_specs=(pl.BlockSpec((1, TM, U_W), lambda b, t: (b, t, 0)),
                   pl.BlockSpec((1, TM, SMALL_W), lambda b, t: (b, t, 0))),
        compiler_params=_cparams(2),
        name="inproj",
    )(*x_args, mod3, mod3, g, wb, ws)


def _head_rms(x, gain, bd):
    xx = x * x
    hi = xx.astype(BF16)
    lo = (xx - hi.astype(F32)).astype(BF16)
    ss = _dot(hi, bd) + _dot(lo, bd)
    return x * lax.rsqrt(ss * (1.0 / NA_DIM) + EPS) * gain


def _na_kernel(q_ref, k_ref, v_ref, qg_ref, kg_ref, bias_ref, y_ref, qn_s, kn_s, *, need_ctx):
    lt = q_ref.shape[1]
    rows = (lt - CTX_LEN) // GRID_W
    wr = min(WIN_R, rows)
    off = CTX_LEN if need_ctx else 0
    li = lax.broadcasted_iota(jnp.int32, (128, 128), 0) // NA_DIM
    lj = lax.broadcasted_iota(jnp.int32, (128, 128), 1) // NA_DIM
    bd = (li == lj).astype(BF16)
    scale = NA_DIM ** -0.5

    tiles_per_step = 3
    assert (lt // TM) % tiles_per_step == 0

    def norm_tiles(i, _):
        r0 = [pl.multiple_of((i * tiles_per_step + j) * TM, TM) for j in range(tiles_per_step)]
        xs = [ref[0, pl.ds(r, TM), :].astype(F32) for r in r0 for ref in (k_ref, q_ref)]
        sq = [x * x for x in xs]
        hi = [x.astype(BF16) for x in sq]
        lo = [(x - h.astype(F32)).astype(BF16) for x, h in zip(sq, hi)]
        ss = [_dot(h, bd) + _dot(l_, bd) for h, l_ in zip(hi, lo)]
        nrm = [x * lax.rsqrt(s * (1.0 / NA_DIM) + EPS) for x, s in zip(xs, ss)]
        for j, r in enumerate(r0):
            kn_s[pl.ds(r, TM), :] = (nrm[2 * j] * kg_ref[...]).astype(BF16)
            qn_s[pl.ds(r, TM), :] = (nrm[2 * j + 1] * qg_ref[...] * scale).astype(BF16)
        return 0

    lax.fori_loop(0, lt // TM // tiles_per_step, norm_tiles, 0)

    lane = lax.broadcasted_iota(jnp.int32, (1, 128), 1)
    head_mask = [lane < NA_DIM, lane >= NA_DIM]
    def stack_heads(q):
        z = jnp.zeros_like(q)
        return jnp.concatenate([jnp.where(head_mask[0], q, z), jnp.where(head_mask[1], q, z)], axis=0)

    def unstack_heads(o):
        r = o.shape[0] // 2
        return jnp.where(head_mask[0], o[0:r], o[r:])

    def attend(probs):
        ss = [[_dot_nt(p[0], kk) for kk in p[1]] for p in probs]
        ss = [[s if b is None else s + b for s, b in zip(sl, p[3])] for sl, p in zip(ss, probs)]
        ms = [functools.reduce(jnp.maximum, [s.max(axis=-1, keepdims=True) for s in sl]) for sl in ss]
        es = [[jnp.exp(s - m) for s in sl] for sl, m in zip(ss, ms)]
        dens = [functools.reduce(jnp.add, [e.sum(axis=-1, keepdims=True) for e in el]) for el in es]
        accs = [functools.reduce(jnp.add, [_dot(e.astype(BF16), vv) for e, vv in zip(el, p[2])])
                for el, p in zip(es, probs)]
        return [unstack_heads(a * (1.0 / d)) for a, d in zip(accs, dens)]

    rows_per_step = 4

    def row_body(i, _):
        probs = []
        for j in range(rows_per_step):
            r = i * rows_per_step + j
            ws = jnp.clip(r - wr // 2, 0, rows - wr)
            cls = ws - r + (WIN_R - 1)
            q_r = qn_s[pl.ds(pl.multiple_of(CTX_LEN + r * GRID_W, GRID_W), GRID_W), :]
            k0 = pl.multiple_of(CTX_LEN + ws * GRID_W, GRID_W)
            kw = kn_s[pl.ds(k0, wr * GRID_W), :]
            vw = v_ref[0, pl.ds(k0, wr * GRID_W), :]
            bias = jnp.concatenate([bias_ref[0, 0, cls], bias_ref[0, 1, cls]], axis=0)
            probs.append((stack_heads(q_r), [kw, kn_s[0:CTX_LEN, :]], [vw, v_ref[0, 0:CTX_LEN, :]],
                          [bias, None]))
        for j, y in enumerate(attend(probs)):
            r = i * rows_per_step + j
            y_ref[0, pl.ds(pl.multiple_of(off + r * GRID_W, GRID_W), GRID_W), :] = y.astype(BF16)
        return 0

    lax.fori_loop(0, rows // rows_per_step, row_body, 0)

    if need_ctx:
        half = CTX_LEN // 2
        probs = [(stack_heads(qn_s[j * half:(j + 1) * half, :]), [kn_s[0:CTX_LEN, :]], [v_ref[0, 0:CTX_LEN, :]],
                  [None]) for j in range(2)]
        for j, y in enumerate(attend(probs)):
            y_ref[0, j * half:(j + 1) * half, :] = y.astype(BF16)


def _na_call(u, q_gain, k_gain, bias_tab, need_ctx):
    bsz, lt, _ = u.shape
    lo = lt if need_ctx else lt - CTX_LEN
    npair = NA_HEADS // 2
    c0 = U_NA // 128
    qg = jnp.tile(q_gain.astype(F32), 2).reshape(1, 128)
    kg = jnp.tile(k_gain.astype(F32), 2).reshape(1, 128)
    return pl.pallas_call(
        functools.partial(_na_kernel, need_ctx=need_ctx),
        out_shape=jax.ShapeDtypeStruct((bsz, lo, NA_W), BF16),
        grid=(npair, bsz),
        in_specs=[pl.BlockSpec((1, lt, 128), lambda p, b: (b, 0, c0 + p)),
                  pl.BlockSpec((1, lt, 128), lambda p, b: (b, 0, c0 + npair + p)),
                  pl.BlockSpec((1, lt, 128), lambda p, b: (b, 0, c0 + 2 * npair + p)),
                  pl.BlockSpec((1, 128), lambda p, b: (0, 0)),
                  pl.BlockSpec((1, 128), lambda p, b: (0, 0)),
                  pl.BlockSpec((1, 2, WIN_R, GRID_W, WIN_R * GRID_W), lambda p, b: (p, 0, 0, 0, 0))],
        out_specs=pl.BlockSpec((1, lo, 128), lambda p, b: (b, 0, p)),
        scratch_shapes=[pltpu.VMEM((lt, 128), BF16), pltpu.VMEM((lt, 128), BF16)],
        compiler_params=_cparams(2),
        name="na",
    )(u, u, u, qg, kg, bias_tab)


def _na_bias_table(rpb):
    q = np.arange(GRID_W)[:, None]
    kc = np.arange(GRID_W)[None, :]
    wstart = np.clip(q - WIN_C // 2, 0, GRID_W - WIN_C)
    ok = (kc >= wstart) & (kc < wstart + WIN_C)
    dc = np.clip(kc - q, 1 - WIN_C, WIN_C - 1) + (WIN_C - 1)
    onehot = (np.arange(2 * WIN_C - 1)[:, None, None] == dc[None]) & ok[None]
    toep = jnp.einsum('lhdc,cqk->lhdqk', rpb.astype(F32), jnp.asarray(onehot, F32),
                      precision=lax.Precision.HIGHEST)
    toep = jnp.where(ok, toep, NEG)
    t = jnp.stack([toep[:, :, cls:cls + WIN_R] for cls in range(WIN_R)], axis=2)
    t = jnp.transpose(t, (0, 1, 2, 4, 3, 5))
    return t.reshape(rpb.shape[0], NA_HEADS // 2, 2, WIN_R, GRID_W, WIN_R * GRID_W)


def _rope_tables(n_lat):
    pos = np.arange(n_lat)
    quarter = DN_DK // 4
    inv = ROPE_THETA ** (-np.arange(quarter, dtype=np.float64) / quarter)
    lane = np.arange(128)
    d = lane % DN_DK
    p = np.where((d < DN_DK // 2)[None, :], (pos // GRID_W)[:, None], (pos % GRID_W)[:, None])
    ang = p * inv[d % quarter][None, :]
    first = ((d % (DN_DK // 2)) < quarter)[None, :]
    cos = np.cos(ang)
    sin_up = np.where(first, -np.sin(ang), 0.0)
    sin_dn = np.where(first, 0.0, np.sin(ang))
    return (jnp.asarray(cos, F32), jnp.asarray(sin_up, F32), jnp.asarray(sin_dn, F32))


def _conv_kernel(prev_ref, cur_ref, next_ref, w_ref, b_ref, cos_ref, sup_ref, sdn_ref, o_ref, win_s, *, gdn):
    t = pl.program_id(1)
    nt = pl.num_programs(1)
    n_c = cur_ref.shape[2]
    prev_ok = t >= 2
    next_ok = jnp.logical_and(t >= 1, t < nt - 1)
    pv = prev_ref[0]
    nx = next_ref[0]
    win_s[0:HALO, :] = jnp.where(prev_ok, pv, jnp.zeros_like(pv))
    win_s[HALO + TM:HALO + TM + HALO, :] = jnp.where(next_ok, nx, jnp.zeros_like(nx))
    win_s[HALO:HALO + TM, :] = cur_ref[0]

    sub = 64
    cb = 512
    wrows = sub + 2 * HALO
    taps = [k for k in range(CONV_W) if k != CONV_W // 2]
    ri = lax.broadcasted_iota(jnp.int32, (len(taps) * sub, wrows), 0)
    ci = lax.broadcasted_iota(jnp.int32, (len(taps) * sub, wrows), 1)
    shift = functools.reduce(lambda acc, it: jnp.where(ri // sub == it[0], it[1] - CONV_W // 2, acc),
                             list(enumerate(taps)), jnp.zeros_like(ri))
    sel = (ci == (ri % sub) + HALO + shift).astype(BF16)
    li = lax.broadcasted_iota(jnp.int32, (128, 128), 0) // DN_DK
    lj = lax.broadcasted_iota(jnp.int32, (128, 128), 1) // DN_DK
    bd = (li == lj).astype(BF16)
    is_lat = t >= 1
    def block(s, c):
        cols = slice(c * cb, (c + 1) * cb)
        shifted = _dot(sel, win_s[s * sub:s * sub + wrows, cols])
        yield
        mid = CONV_W // 2
        acc = win_s[s * sub + HALO:s * sub + HALO + sub, cols].astype(F32) * w_ref[mid:mid + 1, cols]
        for i, k in enumerate(taps):
            acc = acc + shifted[i * sub:(i + 1) * sub, :] * w_ref[k:k + 1, cols]
        acc = _silu(acc + b_ref[:, cols])
        if gdn and c * cb < 2 * DN_KW:
            xs = [acc[:, j * 128:(j + 1) * 128] for j in range(cb // 128)]
            sq = [x * x for x in xs]
            hi = [x.astype(BF16) for x in sq]
            lo = [(x - h.astype(F32)).astype(BF16) for x, h in zip(sq, hi)]
            ss = [_dot(h, bd) + _dot(l_, bd) for h, l_ in zip(hi, lo)]
            yield
            cos = cos_ref[s * sub:(s + 1) * sub, :]
            sup = sup_ref[s * sub:(s + 1) * sub, :]
            sdn = sdn_ref[s * sub:(s + 1) * sub, :]
            quarter = DN_DK // 4
            pieces = []
            for x, s2 in zip(xs, ss):
                x = x * lax.rsqrt(s2 + EPS)
                xr = (x * cos + pltpu.roll(x, shift=128 - quarter, axis=1) * sup
                      + pltpu.roll(x, shift=quarter, axis=1) * sdn)
                x = jnp.where(is_lat, xr, x)
                pieces.append(x * (DN_DK ** -0.5) if c * cb < DN_KW else x)
            acc = jnp.concatenate(pieces, axis=1)
        o_ref[0, s * sub:(s + 1) * sub, cols] = acc.astype(BF16)

    for s in range(0, TM // sub, 2):
        _interleave(*[block(s + i, c) for i in range(2) for c in range(n_c // cb)])


def _conv_call(u, col0, conv_w, conv_b, rope, gdn):
    bsz, lt, _ = u.shape
    nt = lt // TM
    width = 1536
    cblk = col0 // width
    hb = TM // HALO
    nhb = lt // HALO
    cos, sup, sdn = rope
    tab_spec = pl.BlockSpec((TM, 128), lambda b, t: (jnp.maximum(t - 1, 0), 0))
    return pl.pallas_call(
        functools.partial(_conv_kernel, gdn=gdn),
        out_shape=jax.ShapeDtypeStruct((bsz, lt, width), BF16),
        grid=(bsz, nt),
        in_specs=[pl.BlockSpec((1, HALO, width), lambda b, t: (b, jnp.maximum(t * hb - 1, 0), cblk)),
                  pl.BlockSpec((1, TM, width), lambda b, t: (b, t, cblk)),
                  pl.BlockSpec((1, HALO, width), lambda b, t: (b, jnp.minimum((t + 1) * hb, nhb - 1), cblk)),
                  pl.BlockSpec((CONV_W, width), lambda b, t: (0, 0)),
                  pl.BlockSpec((1, width), lambda b, t: (0, 0)),
                  tab_spec, tab_spec, tab_spec],
        out_specs=pl.BlockSpec((1, TM, width), lambda b, t: (b, t, 0)),
        scratch_shapes=[pltpu.VMEM((TM + 2 * HALO, width), BF16)],
        compiler_params=_cparams(2),
        name="conv_gdn" if gdn else "conv_ssd",
    )(u, u, u, conv_w.astype(F32), conv_b.astype(F32).reshape(1, width), cos, sup, sdn)


PAIR_W = 2 * CHUNK


def _pair_iotas():
    ii = lax.broadcasted_iota(jnp.int32, (CHUNK, PAIR_W), 0)
    jj = lax.broadcasted_iota(jnp.int32, (CHUNK, PAIR_W), 1) & (CHUNK - 1)
    return ii, jj


def _tri_masks(rev):
    ii, jj = _pair_iotas()
    if rev:
        return ii <= jj, ii < jj, ii == jj
    return ii >= jj, ii > jj, ii == jj


def _col_pair(x, lane):
    idx = lane + lax.broadcasted_iota(jnp.int32, (CHUNK, PAIR_W), 1) // CHUNK
    return jnp.take_along_axis(x, idx, axis=1)


def _row_form(gc, eye):
    return jnp.sum(jnp.where(eye, gc, 0.0), axis=0, keepdims=True)


def _bd(y):
    r = lax.broadcasted_iota(jnp.int32, (PAIR_W, PAIR_W), 0) // CHUNK
    c = lax.broadcasted_iota(jnp.int32, (PAIR_W, PAIR_W), 1) // CHUNK
    return jnp.where(r == c, jnp.concatenate([y, y], axis=0), jnp.zeros((PAIR_W, PAIR_W), y.dtype))


def _bd_mask(x):
    r = lax.broadcasted_iota(jnp.int32, (PAIR_W, PAIR_W), 0) // CHUNK
    c = lax.broadcasted_iota(jnp.int32, (PAIR_W, PAIR_W), 1) // CHUNK
    return jnp.where(r == c, x, 0.0)


TRI_LEVELS = CHUNK.bit_length() - 1


def _tri_inverse_levels(a_list, t_list, eye_f, lev_lo, lev_hi):
    n = range(len(a_list))
    t = t_list
    if lev_lo == 0:
        ii, jj = _pair_iotas()
        base = (ii >> 1) == (jj >> 1)
        t = [eye_f - jnp.where(base, a[0:CHUNK, :].astype(F32), 0.0) for a in a_list]
    ri = lax.broadcasted_iota(jnp.int32, (PAIR_W, PAIR_W), 0)
    ci = lax.broadcasted_iota(jnp.int32, (PAIR_W, PAIR_W), 1)
    same_head = (ri // CHUNK) == (ci // CHUNK)
    ii, jj = ri & (CHUNK - 1), ci & (CHUNK - 1)
    for lev in range(max(lev_lo, 1), lev_hi):
        pair = jnp.logical_and((ii >> lev) != (jj >> lev), (ii >> (lev + 1)) == (jj >> (lev + 1)))
        pair = jnp.logical_and(pair, same_head)
        lo = [jnp.where(pair, a, jnp.zeros_like(a)) for a in a_list]
        tb = [x.astype(BF16) for x in t]
        x1 = [_dot(tb[i], lo[i]).astype(BF16) for i in n]
        yield
        x2 = [_dot(x1[i], _bd(tb[i])) for i in n]
        yield
        t = [t[i] - x2[i] for i in n]
    return t


def _interleave(*gens):
    results = [None] * len(gens)
    active = list(enumerate(gens))
    while active:
        still = []
        for idx, g in active:
            try:
                next(g)
                still.append((idx, g))
            except StopIteration as e:
                results[idx] = e.value
        active = still
    return results


def _gdn_scan_kernel(qkv_ref, z_ref, us_ref, alog_ref, dtb_ref, og_ref, y_ref, feat_s, gam_s, o_s, st_s,
                     *stage_s, need_ctx):
    pa_s, pb_s = stage_s[:3], stage_s[3:]
    lt = qkv_ref.shape[1]
    n_chunks = lt // CHUNK
    n_ctx = CTX_LEN // CHUNK
    off = 0 if need_ctx else CTX_LEN
    lo = lt - off
    n_pairs = DN_HEADS // 2

    u = us_ref[0]
    lane = lax.broadcasted_iota(jnp.int32, (1, SMALL_W), 1)
    beta = _sigmoid(u)
    g = -jnp.exp(alog_ref[...]) * _softplus(u + dtb_ref[...])
    feat_s[...] = jnp.where(lane < S_DECAY, beta, g)
    _fill_chunk_cumsums(feat_s, gam_s)
    st_s[...] = jnp.zeros(st_s.shape, F32)
    o_s[...] = jnp.zeros(o_s.shape, F32)

    masks = [_tri_masks(False), _tri_masks(True)]
    eye = masks[0][2]
    eye_f = eye.astype(F32)
    last = [CHUNK - 1, 0]
    units = [(d, p) for d in range(2) for p in range(n_pairs)]
    nu = range(len(units))

    def pair(x, p):
        return x[:, p * PAIR_W:(p + 1) * PAIR_W]

    split = TRI_LEVELS // 2

    def rows_of(s):
        cb = jnp.where(s < n_ctx, n_ctx - 1 - s, n_chunks - 1 + n_ctx - s)
        return [pl.multiple_of(s * CHUNK, CHUNK), pl.multiple_of(cb * CHUNK, CHUNK)]

    def load_pairs(r0, col0, dtype):
        blk = [qkv_ref[0, pl.ds(r, CHUNK), col0:col0 + DN_KW].astype(dtype) for r in r0]
        return [pair(blk[d], p) for d, p in units]

    def stage_a(s):
        r0 = rows_of(s)
        f = [feat_s[pl.ds(r, CHUNK), :] for r in r0]
        gam = [gam_s[d, pl.ds(r0[d], CHUNK), :] for d in range(2)]
        k = load_pairs(r0, DN_KW, F32)
        q = load_pairs(r0, 0, BF16)
        bcol = [_col_pair(f[d], S_BETA + d * DN_HEADS + 2 * p) for d, p in units]
        kbq = [jnp.concatenate([(k[i] * bcol[i]).astype(BF16), q[i]], axis=0) for i in nu]
        kbd = [_bd(x.astype(BF16)) for x in k]
        gc = [_col_pair(gam[d], S_DECAY + d * DN_HEADS + 2 * p) for d, p in units]
        gr = [_row_form(x, eye) for x in gc]
        dec = [jnp.exp(jnp.where(masks[units[i][0]][0], gc[i] - gr[i], NEG)) for i in nu]
        kkq = [_dot_nt(kbq[i], kbd[i]) for i in nu]
        yield
        a = [jnp.where(masks[units[i][0]][1], kkq[i][0:CHUNK] * dec[i], 0.0).astype(BF16) for i in nu]
        a = [jnp.concatenate([x, x], axis=0) for x in a]
        qk = [(kkq[i][CHUNK:] * dec[i]).astype(BF16) for i in nu]
        t = yield from _tri_inverse_levels(a, None, eye_f, 0, split)
        return a, t, qk

    def stage_b(s, pa):
        a, t, qk = pa
        r0 = rows_of(s)
        f = [feat_s[pl.ds(r, CHUNK), :] for r in r0]
        gam = [gam_s[d, pl.ds(r0[d], CHUNK), :] for d in range(2)]
        k = load_pairs(r0, DN_KW, F32)
        v = load_pairs(r0, 2 * DN_KW, F32)
        q = load_pairs(r0, 0, BF16)
        bcol = [_col_pair(f[d], S_BETA + d * DN_HEADS + 2 * p) for d, p in units]
        gc = [_col_pair(gam[d], S_DECAY + d * DN_HEADS + 2 * p) for d, p in units]
        egam = [jnp.exp(x) for x in gc]
        glast = [gc[i][last[units[i][0]]:last[units[i][0]] + 1, :] for i in nu]
        kdt = [jnp.transpose(k[i] * jnp.exp(glast[i] - gc[i])).astype(BF16) for i in nu]
        qg =[(q[i].astype(F32) * egam[i]).astype(BF16) for i in nu]
        eg = [jnp.exp(x) for x in glast]
        t = yield from _tri_inverse_levels(a, t, eye_f, split, TRI_LEVELS)
        tb = [x.astype(BF16) for x in t]
        rhs = [jnp.concatenate([_bd((v[i] * bcol[i]).astype(BF16)),
                                _bd((k[i] * bcol[i] * egam[i]).astype(BF16))], axis=1) for i in nu]
        uw = [_dot(tb[i], rhs[i]) for i in nu]
        yield
        uu = [x[:, 0:PAIR_W] for x in uw]
        wq = [jnp.concatenate([uw[i][:, PAIR_W:].astype(BF16), qg[i]], axis=0) for i in nu]
        return uu, wq, kdt, eg, qk

    def stage_c(s, pb, with_out):
        uu, wq, kdt, eg, qk = pb
        r0 = rows_of(s)
        st = [st_s[d, p] for d, p in units]
        sb = [x.astype(BF16) for x in st]
        if with_out:
            po = [_dot(wq[i], sb[i]) for i in nu]
            pred = [x[0:CHUNK] for x in po]
            o1 = [x[CHUNK:] for x in po]
        else:
            pred = [_dot(wq[i][0:CHUNK], sb[i]) for i in nu]
        yield
        vnb = [(uu[i] - pred[i]).astype(BF16) for i in nu]
        upd = [_bd_mask(_dot(kdt[i], vnb[i])) for i in nu]
        if with_out:
            o2 = [_dot(qk[i], _bd(vnb[i])) for i in nu]
        yield
        for i, (d, p) in enumerate(units):
            st_s[d, p] = st[i] * eg[i] + upd[i]
        if with_out:
            for d in range(2):
                o_all = jnp.concatenate([o1[i] + o2[i] for i in nu if units[i][0] == d], axis=1)
                rows = pl.ds(pl.multiple_of(r0[d] - off, CHUNK), CHUNK)
                o_s[rows, :] = o_s[rows, :] + o_all

    def put(refs, vals):
        for ref, group in zip(refs, vals):
            for i, x in enumerate(group):
                ref[i] = x if x.shape[0] != 1 else jnp.broadcast_to(x, ref.shape[1:])

    def get(refs, n_rows):
        return tuple([ref[i] if nr is None else ref[i, 0:nr, :] for i in range(ref.shape[0])]
                     for ref, nr in zip(refs, n_rows))

    a_rows = (None, None, None)
    b_rows = (None, None, None, 1, None)

    def pipelined(s_lo, s_hi, with_out):
        def step(s, carry):
            pa, pb = get(pa_s, a_rows), get(pb_s, b_rows)
            _, rb, ra = _interleave(stage_c(s, pb, with_out), stage_b(s + 1, pa), stage_a(s + 2))
            put(pb_s, rb)
            put(pa_s, ra)
            return carry

        lax.fori_loop(s_lo, s_hi, step, 0)

    (ra,) = _interleave(stage_a(0))
    rb, ra = _interleave(stage_b(0, ra), stage_a(1))
    put(pb_s, rb)
    put(pa_s, ra)
    if need_ctx:
        pipelined(0, n_chunks - 2, True)
    else:
        pipelined(0, n_ctx, False)
        pipelined(n_ctx, n_chunks - 2, True)
    _, rb = _interleave(stage_c(n_chunks - 2, get(pb_s, b_rows), True),
                        stage_b(n_chunks - 1, get(pa_s, a_rows)))
    _interleave(stage_c(n_chunks - 1, rb, True))

    li = lax.broadcasted_iota(jnp.int32, (PAIR_W, PAIR_W), 0) // DN_DK
    lj = lax.broadcasted_iota(jnp.int32, (PAIR_W, PAIR_W), 1) // DN_DK
    ones_bd = (li == lj).astype(BF16)

    def finalize(t, carry):
        ro = pl.multiple_of(t * TM, TM)
        o_all = o_s[pl.ds(ro, TM), :]
        z = z_ref[0, pl.ds(pl.multiple_of(ro + off, TM), TM), :].astype(F32)
        ys = [_head_rms(pair(o_all, p), og_ref[...], ones_bd) for p in range(n_pairs)]
        y_ref[0, pl.ds(ro, TM), :] = (jnp.concatenate(ys, axis=1) * _silu(z)).astype(BF16)
        return carry

    lax.fori_loop(0, lo // TM, finalize, 0)


def _gdn_scan_call(qkvn, u, us, alog_pad, dtb_pad, o_gain, need_ctx):
    bsz, lt, _ = qkvn.shape
    lo = lt if need_ctx else lt - CTX_LEN
    return pl.pallas_call(
        functools.partial(_gdn_scan_kernel, need_ctx=need_ctx),
        out_shape=jax.ShapeDtypeStruct((bsz, lo, DN_VW), BF16),
        grid=(bsz,),
        in_specs=[pl.BlockSpec((1, lt, 3 * DN_KW), lambda b: (b, 0, 0)),
                  pl.BlockSpec((1, lt, DN_VW), lambda b: (b, 0, U_DNZ // DN_VW)),
                  pl.BlockSpec((1, lt, SMALL_W), lambda b: (b, 0, 0)),
                  pl.BlockSpec((1, SMALL_W), lambda b: (0, 0)),
                  pl.BlockSpec((1, SMALL_W), lambda b: (0, 0)),
                  pl.BlockSpec((1, PAIR_W), lambda b: (0, 0))],
        out_specs=pl.BlockSpec((1, lo, DN_VW), lambda b: (b, 0, 0)),
        scratch_shapes=[pltpu.VMEM((lt, SMALL_W), F32),
                        pltpu.VMEM((2, lt, SMALL_W), F32),
                        pltpu.VMEM((lo, DN_VW), F32),
                        pltpu.VMEM((2, DN_HEADS // 2, PAIR_W, PAIR_W), F32),
                        pltpu.VMEM((DN_HEADS, PAIR_W, PAIR_W), BF16),
                        pltpu.VMEM((DN_HEADS, CHUNK, PAIR_W), F32),
                        pltpu.VMEM((DN_HEADS, CHUNK, PAIR_W), BF16),
                        pltpu.VMEM((DN_HEADS, CHUNK, PAIR_W), F32),
                        pltpu.VMEM((DN_HEADS, PAIR_W, PAIR_W), BF16),
                        pltpu.VMEM((DN_HEADS, PAIR_W, CHUNK), BF16),
                        pltpu.VMEM((DN_HEADS, 8, PAIR_W), F32),
                        pltpu.VMEM((DN_HEADS, CHUNK, PAIR_W), BF16)],
        compiler_params=_cparams(1),
        name="gdn_scan",
    )(qkvn, u, us, alog_pad, dtb_pad, jnp.tile(o_gain.astype(F32), 2).reshape(1, PAIR_W))


def _ssd_scan_kernel(xbc_ref, z_ref, us_ref, alog_ref, dtb_ref, dskip_ref, og_ref, y_ref,
                     dt_s, la_s, lam_s, y_s, st_s, *pa_s, need_ctx):
    lt = xbc_ref.shape[1]
    n_chunks = lt // CHUNK
    n_ctx = CTX_LEN // CHUNK
    off = 0 if need_ctx else CTX_LEN
    hpg = SSD_HEADS // SSD_GROUPS
    gw = SSD_DI // SSD_GROUPS

    lo = lt - off
    ppg = hpg // 2

    dt = _softplus(us_ref[0] + dtb_ref[...])
    dt_s[...] = dt
    la_s[...] = dt * (-jnp.exp(alog_ref[...]))
    _fill_chunk_cumsums(la_s, lam_s)
    st_s[...] = jnp.zeros(st_s.shape, F32)
    y_s[...] = jnp.zeros(y_s.shape, F32)

    masks = [_tri_masks(False), _tri_masks(True)]
    eye = masks[0][2]
    last = [CHUNK - 1, 0]
    groups = [(d, gi) for d in range(2) for gi in range(SSD_GROUPS)]

    units = [(d, gi, gi * ppg + r) for d, gi in groups for r in range(ppg)]
    nu = range(len(units))
    ng = range(len(groups))
    grp_of = [groups.index((d, gi)) for d, gi, _ in units]

    def rows_of(s):
        cb = jnp.where(s < n_ctx, n_ctx - 1 - s, n_chunks - 1 + n_ctx - s)
        return [pl.multiple_of(s * CHUNK, CHUNK), pl.multiple_of(cb * CHUNK, CHUNK)]

    def load_c(r0):
        return [xbc_ref[0, pl.ds(r0[d], CHUNK), SSD_DI + SSD_BC + gi * SSD_N:SSD_DI + SSD_BC + (gi + 1) * SSD_N]
                for d, gi in groups]

    def stage_a(s):
        r0 = rows_of(s)
        lam = [lam_s[d, pl.ds(r0[d], CHUNK), :] for d in range(2)]
        dtc = [dt_s[pl.ds(r, CHUNK), :] for r in r0]
        bg = [xbc_ref[0, pl.ds(r0[d], CHUNK), SSD_DI + gi * SSD_N:SSD_DI + (gi + 1) * SSD_N] for d, gi in groups]
        cg = load_c(r0)
        bgt = [_transpose_bf16(b) for b in bg]
        cb2 = [_dot_nt(cg[g], jnp.concatenate([bg[g], bg[g]], axis=0)) for g in ng]
        yield
        x = [xbc_ref[0, pl.ds(r0[d], CHUNK), pp * PAIR_W:(pp + 1) * PAIR_W].astype(F32) for d, _, pp in units]
        ln = [S_DT + d * SSD_HEADS + 2 * pp for d, _, pp in units]
        gc = [_col_pair(lam[units[i][0]], ln[i]) for i in nu]
        xdt = [x[i] * _col_pair(dtc[units[i][0]], ln[i]) for i in nu]
        glast = [gc[i][last[units[i][0]]:last[units[i][0]] + 1, :] for i in nu]
        xdec = [(xdt[i] * jnp.exp(glast[i] - gc[i])).astype(BF16) for i in nu]
        gr = [_row_form(g_, eye) for g_ in gc]
        dec = [jnp.exp(jnp.where(masks[units[i][0]][0], gc[i] - gr[i], NEG)) for i in nu]
        y1 = [_dot((cb2[grp_of[i]] * dec[i]).astype(BF16), _bd(xdt[i].astype(BF16))) for i in nu]
        yield
        return bgt, xdec, y1, [jnp.exp(g_) for g_ in gc], [jnp.exp(g_) for g_ in glast]

    def stage_c(s, pa, with_out):
        bgt, xdec, y1, egc, eg = pa
        r0 = rows_of(s)
        ht = [st_s[d, pp] for d, _, pp in units]
        upd = [_dot(bgt[grp_of[i]], xdec[i]) for i in nu]
        if with_out:
            cg = load_c(r0)
            y2 = [_dot(cg[grp_of[i]], ht[i].astype(BF16)) for i in nu]
        yield
        for i, (d, _, pp) in enumerate(units):
            st_s[d, pp] = ht[i] * eg[i] + upd[i]
        if with_out:
            for g, (d, gi) in enumerate(groups):
                y_grp = jnp.concatenate([y1[i] + y2[i] * egc[i] for i in nu if grp_of[i] == g], axis=1)
                rows = pl.ds(pl.multiple_of(r0[d] - off, CHUNK), CHUNK)
                y_s[rows, gi * gw:(gi + 1) * gw] = y_s[rows, gi * gw:(gi + 1) * gw] + y_grp

    def put(vals):
        for ref, group in zip(pa_s, vals):
            for i, v in enumerate(group):
                ref[i] = v if v.shape[0] != 1 else jnp.broadcast_to(v, ref.shape[1:])

    def get():
        return tuple([ref[i] if nr is None else ref[i, 0:nr, :] for i in range(ref.shape[0])]
                     for ref, nr in zip(pa_s, (None, None, None, None, 1)))

    def pipelined(s_lo, s_hi, with_out):
        def step(s, carry):
            _, ra = _interleave(stage_c(s, get(), with_out), stage_a(s + 1))
            put(ra)
            return carry

        lax.fori_loop(s_lo, s_hi, step, 0)

    put(_interleave(stage_a(0))[0])
    if need_ctx:
        pipelined(0, n_chunks - 1, True)
    else:
        pipelined(0, n_ctx, False)
        pipelined(n_ctx, n_chunks - 1, True)
    _interleave(stage_c(n_chunks - 1, get(), True))

    fin = 2 * CHUNK

    def finalize(t, carry):
        ro = pl.multiple_of(t * fin, fin)
        ri = pl.multiple_of(ro + off, fin)
        xs = xbc_ref[0, pl.ds(ri, fin), 0:SSD_DI].astype(F32)
        y_all = (y_s[pl.ds(ro, fin), :] + xs * dskip_ref[...]) * _silu(z_ref[0, pl.ds(ri, fin), :].astype(F32))
        ys = []
        for gi in range(SSD_GROUPS):
            yg = y_all[:, gi * gw:(gi + 1) * gw]
            ms = jnp.mean(yg * yg, axis=-1, keepdims=True)
            ys.append(yg * lax.rsqrt(ms + EPS) * og_ref[:, gi * gw:(gi + 1) * gw])
        y_ref[0, pl.ds(ro, fin), :] = jnp.concatenate(ys, axis=1).astype(BF16)
        return carry

    lax.fori_loop(0, lo // fin, finalize, 0)


def _ssd_scan_call(xbcn, u, us, alog_pad, dtb_pad, d_skip, o_gain, need_ctx):
    bsz, lt, _ = xbcn.shape
    lo = lt if need_ctx else lt - CTX_LEN
    dskip = jnp.repeat(d_skip.astype(F32), SSD_P).reshape(1, SSD_DI)
    return pl.pallas_call(
        functools.partial(_ssd_scan_kernel, need_ctx=need_ctx),
        out_shape=jax.ShapeDtypeStruct((bsz, lo, SSD_DI), BF16),
        grid=(bsz,),
        in_specs=[pl.BlockSpec((1, lt, SSD_XBC), lambda b: (b, 0, 0)),
                  pl.BlockSpec((1, lt, SSD_DI), lambda b: (b, 0, U_SSDZ // SSD_DI)),
                  pl.BlockSpec((1, lt, SMALL_W), lambda b: (b, 0, 0)),
                  pl.BlockSpec((1, SMALL_W), lambda b: (0, 0)),
                  pl.BlockSpec((1, SMALL_W), lambda b: (0, 0)),
                  pl.BlockSpec((1, SSD_DI), lambda b: (0, 0)),
                  pl.BlockSpec((1, SSD_DI), lambda b: (0, 0))],
        out_specs=pl.BlockSpec((1, lo, SSD_DI), lambda b: (b, 0, 0)),
        scratch_shapes=[pltpu.VMEM((lt, SMALL_W), F32),
                        pltpu.VMEM((lt, SMALL_W), F32),
                        pltpu.VMEM((2, lt, SMALL_W), F32),
                        pltpu.VMEM((lo, SSD_DI), F32),
                        pltpu.VMEM((2, SSD_HEADS // 2, SSD_N, PAIR_W), F32),
                        pltpu.VMEM((2 * SSD_GROUPS, SSD_N, CHUNK), BF16),
                        pltpu.VMEM((SSD_HEADS, CHUNK, PAIR_W), BF16),
                        pltpu.VMEM((SSD_HEADS, CHUNK, PAIR_W), F32),
                        pltpu.VMEM((SSD_HEADS, CHUNK, PAIR_W), F32),
                        pltpu.VMEM((SSD_HEADS, 8, PAIR_W), F32)],
        compiler_params=_cparams(1),
        name="ssd_scan",
    )(xbcn, u, us, alog_pad, dtb_pad, dskip, o_gain.astype(F32).reshape(1, SSD_DI))


def _post_kernel(ya_ref, yb_ref, yc_ref, gt_ref, x_ref, ctx_ref, gate1_ref, shift_ref, scale_ref, gate2_ref,
                 g_ref, wpa_ref, wpb_ref, wpc_ref, wout_ref, w1_ref, w2_ref, o_ref, *, t0):
    x = _token_tile(x_ref, ctx_ref, pl.program_id(1) + t0)
    g = _sigmoid(gt_ref[0].astype(F32))
    m = (g[:, 0:D_MODEL] * _dot(ya_ref[0], wpa_ref[...])
         + g[:, D_MODEL:2 * D_MODEL] * _dot(yb_ref[0], wpb_ref[...])
         + g[:, 2 * D_MODEL:3 * D_MODEL] * _dot(yc_ref[0], wpc_ref[...]))
    x1 = x + gate1_ref[0] * _dot(m.astype(BF16), wout_ref[...])
    h = _norm_mod(x1, g_ref[...], shift_ref[0], scale_ref[0]).astype(BF16)
    fc = 1024
    acc = None
    for c in range(D_FF // fc):
        a = jnp.maximum(_dot(h, w1_ref[:, c * fc:(c + 1) * fc]), 0.0)
        part = _dot((a * a).astype(BF16), w2_ref[c * fc:(c + 1) * fc, :])
        acc = part if acc is None else acc + part
    o_ref[0] = x1 + gate2_ref[0] * acc


def _post_call(ya, yb, yc, u, stream, mod3, g, wpa, wpb, wpc, wout, w1, w2, need_ctx):
    bsz, lt, _ = u.shape
    nt = lt // TM
    t0 = 0 if need_ctx else 1
    x_specs, x_args = _stream_specs(stream, t0)

    def mod_spec(k):
        return pl.BlockSpec((1, 1, D_MODEL), lambda b, t: (_mod_row(b, t + t0, bsz), 0, k))

    return pl.pallas_call(
        functools.partial(_post_kernel, t0=t0),
        out_shape=jax.ShapeDtypeStruct((bsz, (nt - t0) * TM, D_MODEL), F32),
        grid=(bsz, nt - t0),
        in_specs=[pl.BlockSpec((1, TM, NA_W), lambda b, t: (b, t, 0)),
                  pl.BlockSpec((1, TM, DN_VW), lambda b, t: (b, t, 0)),
                  pl.BlockSpec((1, TM, SSD_DI), lambda b, t: (b, t, 0)),
                  pl.BlockSpec((1, TM, 3 * D_MODEL), lambda b, t: (b, t + t0, U_GATE // (3 * D_MODEL)))]
        + x_specs + [mod_spec(2), mod_spec(3), mod_spec(4), mod_spec(5),
                     _resident((1, D_MODEL)), _resident((NA_W, D_MODEL)), _resident((DN_VW, D_MODEL)),
                     _resident((SSD_DI, D_MODEL)), _resident((D_MODEL, D_MODEL)),
                     _resident((D_MODEL, D_FF)), _resident((D_FF, D_MODEL))],
        out_specs=pl.BlockSpec((1, TM, D_MODEL), lambda b, t: (b, t, 0)),
        compiler_params=_cparams(2),
        name="post",
    )(ya, yb, yc, u, *x_args, mod3, mod3, mod3, mod3, g, wpa, wpb, wpc, wout, w1, w2)


def _pad_lanes(off, v):
    flat = v.astype(F32).reshape(v.shape[0], 1, -1)
    return jnp.pad(flat, ((0, 0), (0, 0), (off, SMALL_W - off - flat.shape[-1])))


def _split_w_in(w):
    big = jnp.concatenate([w[:, _O_NA:_O_DNQKV], w[:, _O_DNQKV:_O_DNZ], w[:, _O_XBC:_O_SSDDT],
                           w[:, _O_DNZ:_O_DNB], w[:, _O_SSDZ:_O_XBC], w[:, _O_GATE:_O_END]], axis=1)
    small = jnp.concatenate([w[:, _O_DNB:_O_SSDZ], w[:, _O_SSDDT:_O_GATE]], axis=1)
    small = jnp.pad(small, ((0, 0), (0, SMALL_W - small.shape[1])))
    return big.astype(BF16), small.astype(BF16)


def kernel(x, c, ctx, c_ctx, w_ada, b_ada, norm1_g, norm2_g, w_in, na_q_gain, na_k_gain, na_rpb,
           dn_conv_w, dn_a_log, dn_dt_bias, dn_o_gain, ssd_conv_w, ssd_conv_b, ssd_a_log,
           ssd_dt_bias, ssd_d, ssd_o_gain, w_pa, w_pb, w_pc, w_out, w_ff1, w_ff2):
    bsz, seq, _ = x.shape
    assert bsz < MOD_ROWS and seq % TM == 0 and ctx.shape[1] == CTX_LEN
    n_l = w_ada.shape[0]
    cs = jnp.concatenate([c, c_ctx[None, :], jnp.zeros((MOD_ROWS - bsz - 1, D_MODEL), F32)], axis=0)
    mod = _ada_call(cs, w_ada, b_ada)
    stream = (x, ctx)
    lt = CTX_LEN + seq
    rope = _rope_tables(seq)
    zeros_w = jnp.zeros((1, 1536), F32)

    mod4 = mod.reshape(n_l, MOD_ROWS, 1, 6 * D_MODEL)
    alog_dn, dtb_dn = _pad_lanes(S_DECAY, dn_a_log), _pad_lanes(S_DECAY, dn_dt_bias)
    alog_ssd, dtb_ssd = _pad_lanes(S_DT, ssd_a_log), _pad_lanes(S_DT, ssd_dt_bias)
    g1, g2 = norm1_g.reshape(n_l, 1, D_MODEL), norm2_g.reshape(n_l, 1, D_MODEL)

    for l in range(n_l):
        need_ctx = l < n_l - 1
        wb, ws = _split_w_in(w_in[l])
        u, us = _inproj_call(stream, bsz, lt, mod4[l], g1[l], wb, ws)
        ya = _na_call(u, na_q_gain[l], na_k_gain[l], _na_bias_table(na_rpb[l:l + 1])[0], need_ctx)
        qkvn = _conv_call(u, U_DN, dn_conv_w[l], zeros_w, rope, True)
        yb = _gdn_scan_call(qkvn, u, us, alog_dn[l], dtb_dn[l], dn_o_gain[l], need_ctx)
        xbcn = _conv_call(u, U_XBC, ssd_conv_w[l], ssd_conv_b[l], rope, False)
        yc = _ssd_scan_call(xbcn, u, us, alog_ssd[l], dtb_ssd[l], ssd_d[l], ssd_o_gain[l], need_ctx)
        post_w = [w[l].astype(BF16) for w in (w_pa, w_pb, w_pc, w_out, w_ff1, w_ff2)]
        stream = _post_call(ya, yb, yc, u, stream, mod4[l], g2[l], *post_w, need_ctx)
    return stream
```

```python
import functools

import numpy as np
import jax
import jax.numpy as jnp
from jax import lax
from jax.experimental import pallas as pl
from jax.experimental.pallas import tpu as pltpu

F32 = jnp.float32
BF16 = jnp.bfloat16

D_MODEL = 1024
DEPTH = 2
GRID_W = 64
CTX_LEN = 256
NA_HEADS = 8
NA_DIM = 64
NA_W = NA_HEADS * NA_DIM
WIN_R = 8
WIN_C = 16
DN_HEADS = 8
DN_DK = 64
DN_KW = DN_HEADS * DN_DK
DN_VW = DN_HEADS * DN_DK
SSD_HEADS = 16
SSD_P = 64
SSD_GROUPS = 2
SSD_N = 128
SSD_DI = SSD_HEADS * SSD_P
SSD_BC = SSD_GROUPS * SSD_N
SSD_XBC = SSD_DI + 2 * SSD_BC
CONV_W = 5
D_FF = 4 * D_MODEL
ROPE_THETA = 10000.0
EPS = 1e-6
NEG = -1e30

CHUNK = 64
TM = 256
HALO = 16
MOD_ROWS = 16
SMALL_W = 128

_O_NA = 0
_O_DNQKV = _O_NA + 3 * NA_W
_O_DNZ = _O_DNQKV + 2 * DN_KW + DN_VW
_O_DNB = _O_DNZ + DN_VW
_O_DNA = _O_DNB + 2 * DN_HEADS
_O_SSDZ = _O_DNA + 2 * DN_HEADS
_O_XBC = _O_SSDZ + SSD_DI
_O_SSDDT = _O_XBC + SSD_XBC
_O_GATE = _O_SSDDT + 2 * SSD_HEADS
_O_END = _O_GATE + 3 * D_MODEL

U_NA, U_DN, U_XBC, U_DNZ, U_SSDZ, U_GATE = 0, 1536, 3072, 4608, 5120, 6144
U_W = 9216
S_BETA, S_DECAY, S_DT = 0, 16, 32

V7X_VMEM_BYTES = 64 * 1024 * 1024
VMEM_LIMIT = V7X_VMEM_BYTES * 7 // 8


def _cparams(n_axes):
    return pltpu.CompilerParams(dimension_semantics=("arbitrary",) * n_axes,
                                vmem_limit_bytes=VMEM_LIMIT)


def _dot(a, b):
    return jnp.dot(a, b, preferred_element_type=F32)


def _dot_nt(a, b):
    return lax.dot_general(a, b, (((1,), (1,)), ((), ())), preferred_element_type=F32)


def _chunk_cumsum(x, rev):
    row = lax.broadcasted_iota(jnp.int32, x.shape, 0)
    n = x.shape[0]
    sh = 1
    while sh < n:
        if rev:
            x = x + jnp.where(row < n - sh, pltpu.roll(x, shift=n - sh, axis=0), 0.0)
        else:
            x = x + jnp.where(row >= sh, pltpu.roll(x, shift=sh, axis=0), 0.0)
        sh *= 2
    return x


def _fill_chunk_cumsums(src_s, dst_s):
    def body(c, carry):
        rows = pl.ds(pl.multiple_of(c * CHUNK, CHUNK), CHUNK)
        x = src_s[rows, :]
        dst_s[0, rows, :] = _chunk_cumsum(x, False)
        dst_s[1, rows, :] = _chunk_cumsum(x, True)
        return carry

    lax.fori_loop(0, src_s.shape[0] // CHUNK, body, 0)


def _transpose_bf16(a):
    m = a.shape[1]
    eye = (lax.broadcasted_iota(jnp.int32, (m, m), 0) == lax.broadcasted_iota(jnp.int32, (m, m), 1))
    return _dot_nt(eye.astype(BF16), a).astype(BF16)


def _softplus(x):
    return jnp.maximum(x, 0.0) + jnp.log(1.0 + jnp.exp(-jnp.abs(x)))


def _sigmoid(x):
    return 1.0 / (1.0 + jnp.exp(-x))


def _silu(x):
    return x * _sigmoid(x)


def _ada_kernel(c_ref, w_ref, b_ref, o_ref):
    c = c_ref[...]
    a = _silu(c).astype(BF16)
    o_ref[0] = _dot(a, w_ref[0].astype(BF16)) + b_ref[0]


def _ada_call(cs, w_ada, b_ada):
    n_l = w_ada.shape[0]
    tn = 1536
    return pl.pallas_call(
        _ada_kernel,
        out_shape=jax.ShapeDtypeStruct((n_l, MOD_ROWS, 6 * D_MODEL), F32),
        grid=(n_l, 6 * D_MODEL // tn),
        in_specs=[pl.BlockSpec((MOD_ROWS, D_MODEL), lambda l, n: (0, 0)),
                  pl.BlockSpec((1, D_MODEL, tn), lambda l, n: (l, 0, n)),
                  pl.BlockSpec((1, 1, tn), lambda l, n: (l, 0, n))],
        out_specs=pl.BlockSpec((1, MOD_ROWS, tn), lambda l, n: (l, 0, n)),
        compiler_params=_cparams(2),
        name="ada",
    )(cs, w_ada, b_ada.reshape(n_l, 1, 6 * D_MODEL))


def _mod_row(b, t, bsz):
    return jnp.where(t == 0, bsz, b)


def _norm_mod(x, g, shift, scale):
    ms = jnp.mean(x * x, axis=-1, keepdims=True)
    y = x * lax.rsqrt(ms + EPS) * g
    return y * (1.0 + scale) + shift


def _token_tile(x_ref, ctx_ref, t):
    return jnp.where(t == 0, ctx_ref[0], x_ref[0])


def _stream_specs(stream, t0):
    ctx_spec = pl.BlockSpec((1, TM, D_MODEL), lambda b, t: (b, 0, 0))
    if isinstance(stream, tuple):
        lat, ctx = stream
        return [pl.BlockSpec((1, TM, D_MODEL), lambda b, t: (b, jnp.maximum(t + t0 - 1, 0), 0)), ctx_spec], [lat, ctx]
    return [pl.BlockSpec((1, TM, D_MODEL), lambda b, t: (b, t + t0, 0)), ctx_spec], [stream, stream]


def _resident(shape):
    return pl.BlockSpec(shape, lambda b, t: (0,) * len(shape), pipeline_mode=pl.Buffered(1))


def _inproj_kernel(x_ref, ctx_ref, shift_ref, scale_ref, g_ref, wb_ref, ws_ref, u_ref, us_ref):
    x = _token_tile(x_ref, ctx_ref, pl.program_id(1))
    h = _norm_mod(x, g_ref[...], shift_ref[0], scale_ref[0]).astype(BF16)
    cw = 1536
    for c in range(U_W // cw):
        u_ref[0, :, c * cw:(c + 1) * cw] = _dot(h, wb_ref[:, c * cw:(c + 1) * cw]).astype(BF16)
    us_ref[0] = _dot(h, ws_ref[...])


def _inproj_call(stream, bsz, lt, mod3, g, wb, ws):
    nt = lt // TM
    x_specs, x_args = _stream_specs(stream, 0)
    return pl.pallas_call(
        _inproj_kernel,
        out_shape=(jax.ShapeDtypeStruct((bsz, lt, U_W), BF16),
                   jax.ShapeDtypeStruct((bsz, lt, SMALL_W), F32)),
        grid=(bsz, nt),
        in_specs=x_specs + [
            pl.BlockSpec((1, 1, D_MODEL), lambda b, t: (_mod_row(b, t, bsz), 0, 0)),
            pl.BlockSpec((1, 1, D_MODEL), lambda b, t: (_mod_row(b, t, bsz), 0, 1)),
            _resident((1, D_MODEL)), _resident((D_MODEL, U_W)), _resident((D_MODEL, SMALL_W))],
        out_specs=(pl.BlockSpec((1, TM, U_W), lambda b, t: (b, t, 0)),
                   pl.BlockSpec((1, TM, SMALL_W), lambda b, t: (b, t, 0))),
        compiler_params=_cparams(2),
        name="inproj",
    )(*x_args, mod3, mod3, g, wb, ws)


def _head_rms(x, gain, bd):
    xx = x * x
    hi = xx.astype(BF16)
    lo = (xx - hi.astype(F32)).astype(BF16)
    ss = _dot(hi, bd) + _dot(lo, bd)
    return x * lax.rsqrt(ss * (1.0 / NA_DIM) + EPS) * gain


def _na_kernel(q_ref, k_ref, v_ref, qg_ref, kg_ref, bias_ref, y_ref, qn_s, kn_s, *, need_ctx):
    lt = q_ref.shape[1]
    rows = (lt - CTX_LEN) // GRID_W
    wr = min(WIN_R, rows)
    off = CTX_LEN if need_ctx else 0
    li = lax.broadcasted_iota(jnp.int32, (128, 128), 0) // NA_DIM
    lj = lax.broadcasted_iota(jnp.int32, (128, 128), 1) // NA_DIM
    bd = (li == lj).astype(BF16)
    scale = NA_DIM ** -0.5

    tiles_per_step = 3
    assert (lt // TM) % tiles_per_step == 0

    def norm_tiles(i, _):
        r0 = [pl.multiple_of((i * tiles_per_step + j) * TM, TM) for j in range(tiles_per_step)]
        xs = [ref[0, pl.ds(r, TM), :].astype(F32) for r in r0 for ref in (k_ref, q_ref)]
        sq = [x * x for x in xs]
        hi = [x.astype(BF16) for x in sq]
        lo = [(x - h.astype(F32)).astype(BF16) for x, h in zip(sq, hi)]
        ss = [_dot(h, bd) + _dot(l_, bd) for h, l_ in zip(hi, lo)]
        nrm = [x * lax.rsqrt(s * (1.0 / NA_DIM) + EPS) for x, s in zip(xs, ss)]
        for j, r in enumerate(r0):
            kn_s[pl.ds(r, TM), :] = (nrm[2 * j] * kg_ref[...]).astype(BF16)
            qn_s[pl.ds(r, TM), :] = (nrm[2 * j + 1] * qg_ref[...] * scale).astype(BF16)
        return 0

    lax.fori_loop(0, lt // TM // tiles_per_step, norm_tiles, 0)

    lane = lax.broadcasted_iota(jnp.int32, (1, 128), 1)
    head_mask = [lane < NA_DIM, lane >= NA_DIM]
    def stack_heads(q):
        z = jnp.zeros_like(q)
        return jnp.concatenate([jnp.where(head_mask[0], q, z), jnp.where(head_mask[1], q, z)], axis=0)

    def unstack_heads(o):
        r = o.shape[0] // 2
        return jnp.where(head_mask[0], o[0:r], o[r:])

    def attend(probs):
        ss = [[_dot_nt(p[0], kk) for kk in p[1]] for p in probs]
        ss = [[s if b is None else s + b for s, b in zip(sl, p[3])] for sl, p in zip(ss, probs)]
        ms = [functools.reduce(jnp.maximum, [s.max(axis=-1, keepdims=True) for s in sl]) for sl in ss]
        es = [[jnp.exp(s - m) for s in sl] for sl, m in zip(ss, ms)]
        dens = [functools.reduce(jnp.add, [e.sum(axis=-1, keepdims=True) for e in el]) for el in es]
        accs = [functools.reduce(jnp.add, [_dot(e.astype(BF16), vv) for e, vv in zip(el, p[2])])
                for el, p in zip(es, probs)]
        return [unstack_heads(a * (1.0 / d)) for a, d in zip(accs, dens)]

    rows_per_step = 8

    def row_body(i, _):
        probs = []
        for j in range(rows_per_step):
            r = i * rows_per_step + j
            ws = jnp.clip(r - wr // 2, 0, rows - wr)
            cls = ws - r + (WIN_R - 1)
            q_r = qn_s[pl.ds(pl.multiple_of(CTX_LEN + r * GRID_W, GRID_W), GRID_W), :]
            k0 = pl.multiple_of(CTX_LEN + ws * GRID_W, GRID_W)
            kw = kn_s[pl.ds(k0, wr * GRID_W), :]
            vw = v_ref[0, pl.ds(k0, wr * GRID_W), :]
            bias = jnp.concatenate([bias_ref[0, 0, cls], bias_ref[0, 1, cls]], axis=0)
            probs.append((stack_heads(q_r), [kw, kn_s[0:CTX_LEN, :]], [vw, v_ref[0, 0:CTX_LEN, :]],
                          [bias, None]))
        for j, y in enumerate(attend(probs)):
            r = i * rows_per_step + j
            y_ref[0, pl.ds(pl.multiple_of(off + r * GRID_W, GRID_W), GRID_W), :] = y.astype(BF16)
        return 0

    lax.fori_loop(0, rows // rows_per_step, row_body, 0)

    if need_ctx:
        half = CTX_LEN // 2
        probs = [(stack_heads(qn_s[j * half:(j + 1) * half, :]), [kn_s[0:CTX_LEN, :]], [v_ref[0, 0:CTX_LEN, :]],
                  [None]) for j in range(2)]
        for j, y in enumerate(attend(probs)):
            y_ref[0, j * half:(j + 1) * half, :] = y.astype(BF16)


def _na_call(u, q_gain, k_gain, bias_tab, need_ctx):
    bsz, lt, _ = u.shape
    lo = lt if need_ctx else lt - CTX_LEN
    npair = NA_HEADS // 2
    c0 = U_NA // 128
    qg = jnp.tile(q_gain.astype(F32), 2).reshape(1, 128)
    kg = jnp.tile(k_gain.astype(F32), 2).reshape(1, 128)
    return pl.pallas_call(
        functools.partial(_na_kernel, need_ctx=need_ctx),
        out_shape=jax.ShapeDtypeStruct((bsz, lo, NA_W), BF16),
        grid=(npair, bsz),
        in_specs=[pl.BlockSpec((1, lt, 128), lambda p, b: (b, 0, c0 + p)),
                  pl.BlockSpec((1, lt, 128), lambda p, b: (b, 0, c0 + npair + p)),
                  pl.BlockSpec((1, lt, 128), lambda p, b: (b, 0, c0 + 2 * npair + p)),
                  pl.BlockSpec((1, 128), lambda p, b: (0, 0)),
                  pl.BlockSpec((1, 128), lambda p, b: (0, 0)),
                  pl.BlockSpec((1, 2, WIN_R, GRID_W, WIN_R * GRID_W), lambda p, b: (p, 0, 0, 0, 0))],
        out_specs=pl.BlockSpec((1, lo, 128), lambda p, b: (b, 0, p)),
        scratch_shapes=[pltpu.VMEM((lt, 128), BF16), pltpu.VMEM((lt, 128), BF16)],
        compiler_params=_cparams(2),
        name="na",
    )(u, u, u, qg, kg, bias_tab)


def _na_bias_table(rpb):
    q = np.arange(GRID_W)[:, None]
    kc = np.arange(GRID_W)[None, :]
    wstart = np.clip(q - WIN_C // 2, 0, GRID_W - WIN_C)
    ok = (kc >= wstart) & (kc < wstart + WIN_C)
    dc = np.clip(kc - q, 1 - WIN_C, WIN_C - 1) + (WIN_C - 1)
    onehot = (np.arange(2 * WIN_C - 1)[:, None, None] == dc[None]) & ok[None]
    rows = jnp.stack([rpb.astype(F32)[:, :, cls:cls + WIN_R] for cls in range(WIN_R)], axis=2)
    t = jnp.einsum('lhswc,cqk->lhsqwk', rows, jnp.asarray(onehot, F32), precision=lax.Precision.HIGHEST)
    t = jnp.where(ok[:, None, :], t, NEG)
    return t.reshape(rpb.shape[0], NA_HEADS // 2, 2, WIN_R, GRID_W, WIN_R * GRID_W)


def _rope_tables(n_lat):
    pos = np.arange(n_lat)
    quarter = DN_DK // 4
    inv = ROPE_THETA ** (-np.arange(quarter, dtype=np.float64) / quarter)
    lane = np.arange(128)
    d = lane % DN_DK
    p = np.where((d < DN_DK // 2)[None, :], (pos // GRID_W)[:, None], (pos % GRID_W)[:, None])
    ang = p * inv[d % quarter][None, :]
    first = ((d % (DN_DK // 2)) < quarter)[None, :]
    cos = np.cos(ang)
    sin_up = np.where(first, -np.sin(ang), 0.0)
    sin_dn = np.where(first, 0.0, np.sin(ang))
    return (jnp.asarray(cos, F32), jnp.asarray(sin_up, F32), jnp.asarray(sin_dn, F32))


def _conv_kernel(prev_ref, cur_ref, next_ref, w_ref, b_ref, cos_ref, sup_ref, sdn_ref, o_ref, win_s, *, gdn):
    t = pl.program_id(1)
    nt = pl.num_programs(1)
    n_c = cur_ref.shape[2]
    prev_ok = t >= 2
    next_ok = jnp.logical_and(t >= 1, t < nt - 1)
    pv = prev_ref[0]
    nx = next_ref[0]
    win_s[0:HALO, :] = jnp.where(prev_ok, pv, jnp.zeros_like(pv))
    win_s[HALO + TM:HALO + TM + HALO, :] = jnp.where(next_ok, nx, jnp.zeros_like(nx))
    win_s[HALO:HALO + TM, :] = cur_ref[0]

    sub = 64
    cb = 512
    wrows = sub + 2 * HALO
    taps = [k for k in range(CONV_W) if k != CONV_W // 2]
    ri = lax.broadcasted_iota(jnp.int32, (len(taps) * sub, wrows), 0)
    ci = lax.broadcasted_iota(jnp.int32, (len(taps) * sub, wrows), 1)
    shift = functools.reduce(lambda acc, it: jnp.where(ri // sub == it[0], it[1] - CONV_W // 2, acc),
                             list(enumerate(taps)), jnp.zeros_like(ri))
    sel = (ci == (ri % sub) + HALO + shift).astype(BF16)
    li = lax.broadcasted_iota(jnp.int32, (128, 128), 0) // DN_DK
    lj = lax.broadcasted_iota(jnp.int32, (128, 128), 1) // DN_DK
    bd = (li == lj).astype(BF16)
    is_lat = t >= 1
    def block(s, c):
        cols = slice(c * cb, (c + 1) * cb)
        shifted = _dot(sel, win_s[s * sub:s * sub + wrows, cols])
        yield
        mid = CONV_W // 2
        acc = win_s[s * sub + HALO:s * sub + HALO + sub, cols].astype(F32) * w_ref[mid:mid + 1, cols]
        for i, k in enumerate(taps):
            acc = acc + shifted[i * sub:(i + 1) * sub, :] * w_ref[k:k + 1, cols]
        acc = _silu(acc + b_ref[:, cols])
        if gdn and c * cb < 2 * DN_KW:
            xs = [acc[:, j * 128:(j + 1) * 128] for j in range(cb // 128)]
            sq = [x * x for x in xs]
            hi = [x.astype(BF16) for x in sq]
            lo = [(x - h.astype(F32)).astype(BF16) for x, h in zip(sq, hi)]
            ss = [_dot(h, bd) + _dot(l_, bd) for h, l_ in zip(hi, lo)]
            yield
            cos = cos_ref[s * sub:(s + 1) * sub, :]
            sup = sup_ref[s * sub:(s + 1) * sub, :]
            sdn = sdn_ref[s * sub:(s + 1) * sub, :]
            quarter = DN_DK // 4
            pieces = []
            for x, s2 in zip(xs, ss):
                x = x * lax.rsqrt(s2 + EPS)
                xr = (x * cos + pltpu.roll(x, shift=128 - quarter, axis=1) * sup
                      + pltpu.roll(x, shift=quarter, axis=1) * sdn)
                x = jnp.where(is_lat, xr, x)
                pieces.append(x * (DN_DK ** -0.5) if c * cb < DN_KW else x)
            acc = jnp.concatenate(pieces, axis=1)
        o_ref[0, s * sub:(s + 1) * sub, cols] = acc.astype(BF16)

    for s in range(0, TM // sub, 2):
        _interleave(*[block(s + i, c) for i in range(2) for c in range(n_c // cb)])


def _conv_call(u, col0, conv_w, conv_b, rope, gdn):
    bsz, lt, _ = u.shape
    nt = lt // TM
    width = 1536
    cblk = col0 // width
    hb = TM // HALO
    nhb = lt // HALO
    cos, sup, sdn = rope
    tab_spec = pl.BlockSpec((TM, 128), lambda b, t: (jnp.maximum(t - 1, 0), 0))
    return pl.pallas_call(
        functools.partial(_conv_kernel, gdn=gdn),
        out_shape=jax.ShapeDtypeStruct((bsz, lt, width), BF16),
        grid=(bsz, nt),
        in_specs=[pl.BlockSpec((1, HALO, width), lambda b, t: (b, jnp.maximum(t * hb - 1, 0), cblk)),
                  pl.BlockSpec((1, TM, width), lambda b, t: (b, t, cblk)),
                  pl.BlockSpec((1, HALO, width), lambda b, t: (b, jnp.minimum((t + 1) * hb, nhb - 1), cblk)),
                  pl.BlockSpec((CONV_W, width), lambda b, t: (0, 0)),
                  pl.BlockSpec((1, width), lambda b, t: (0, 0)),
                  tab_spec, tab_spec, tab_spec],
        out_specs=pl.BlockSpec((1, TM, width), lambda b, t: (b, t, 0)),
        scratch_shapes=[pltpu.VMEM((TM + 2 * HALO, width), BF16)],
        compiler_params=_cparams(2),
        name="conv_gdn" if gdn else "conv_ssd",
    )(u, u, u, conv_w.astype(F32), conv_b.astype(F32).reshape(1, width), cos, sup, sdn)


PAIR_W = 2 * CHUNK


def _pair_iotas():
    ii = lax.broadcasted_iota(jnp.int32, (CHUNK, PAIR_W), 0)
    jj = lax.broadcasted_iota(jnp.int32, (CHUNK, PAIR_W), 1) & (CHUNK - 1)
    return ii, jj


def _tri_masks(rev):
    ii, jj = _pair_iotas()
    if rev:
        return ii <= jj, ii < jj, ii == jj
    return ii >= jj, ii > jj, ii == jj


def _col_pair(x, lane):
    idx = lane + lax.broadcasted_iota(jnp.int32, (CHUNK, PAIR_W), 1) // CHUNK
    return jnp.take_along_axis(x, idx, axis=1)


def _row_form(gc, eye):
    return jnp.sum(jnp.where(eye, gc, 0.0), axis=0, keepdims=True)


def _bd(y):
    r = lax.broadcasted_iota(jnp.int32, (PAIR_W, PAIR_W), 0) // CHUNK
    c = lax.broadcasted_iota(jnp.int32, (PAIR_W, PAIR_W), 1) // CHUNK
    return jnp.where(r == c, jnp.concatenate([y, y], axis=0), jnp.zeros((PAIR_W, PAIR_W), y.dtype))


def _bd_mask(x):
    r = lax.broadcasted_iota(jnp.int32, (PAIR_W, PAIR_W), 0) // CHUNK
    c = lax.broadcasted_iota(jnp.int32, (PAIR_W, PAIR_W), 1) // CHUNK
    return jnp.where(r == c, x, 0.0)


TRI_LEVELS = CHUNK.bit_length() - 1


def _tri_inverse_levels(a_list, t_list, eye_f, lev_lo, lev_hi):
    n = range(len(a_list))
    t = t_list
    if lev_lo == 0:
        ii, jj = _pair_iotas()
        base = (ii >> 1) == (jj >> 1)
        t = [eye_f - jnp.where(base, a[0:CHUNK, :].astype(F32), 0.0) for a in a_list]
    ri = lax.broadcasted_iota(jnp.int32, (PAIR_W, PAIR_W), 0)
    ci = lax.broadcasted_iota(jnp.int32, (PAIR_W, PAIR_W), 1)
    same_head = (ri // CHUNK) == (ci // CHUNK)
    ii, jj = ri & (CHUNK - 1), ci & (CHUNK - 1)
    for lev in range(max(lev_lo, 1), lev_hi):
        pair = jnp.logical_and((ii >> lev) != (jj >> lev), (ii >> (lev + 1)) == (jj >> (lev + 1)))
        pair = jnp.logical_and(pair, same_head)
        lo = [jnp.where(pair, a, jnp.zeros_like(a)) for a in a_list]
        tb = [x.astype(BF16) for x in t]
        x1 = [_dot(tb[i], lo[i]).astype(BF16) for i in n]
        yield
        x2 = [_dot(x1[i], _bd(tb[i])) for i in n]
        yield
        t = [t[i] - x2[i] for i in n]
    return t


def _interleave(*gens):
    results = [None] * len(gens)
    active = list(enumerate(gens))
    while active:
        still = []
        for idx, g in active:
            try:
                next(g)
                still.append((idx, g))
            except StopIteration as e:
                results[idx] = e.value
        active = still
    return results


def _gdn_scan_kernel(qkv_ref, z_ref, us_ref, alog_ref, dtb_ref, og_ref, y_ref, feat_s, gam_s, o_s, st_s,
                     *stage_s, need_ctx):
    pa_s, pb_s = stage_s[:3], stage_s[3:]
    lt = qkv_ref.shape[1]
    n_chunks = lt // CHUNK
    n_ctx = CTX_LEN // CHUNK
    off = 0 if need_ctx else CTX_LEN
    lo = lt - off
    n_pairs = DN_HEADS // 2

    u = us_ref[0]
    lane = lax.broadcasted_iota(jnp.int32, (1, SMALL_W), 1)
    beta = _sigmoid(u)
    g = -jnp.exp(alog_ref[...]) * _softplus(u + dtb_ref[...])
    feat_s[...] = jnp.where(lane < S_DECAY, beta, g)
    _fill_chunk_cumsums(feat_s, gam_s)
    st_s[...] = jnp.zeros(st_s.shape, F32)
    o_s[...] = jnp.zeros(o_s.shape, F32)

    masks = [_tri_masks(False), _tri_masks(True)]
    eye = masks[0][2]
    eye_f = eye.astype(F32)
    last = [CHUNK - 1, 0]
    units = [(d, p) for d in range(2) for p in range(n_pairs)]
    nu = range(len(units))

    def pair(x, p):
        return x[:, p * PAIR_W:(p + 1) * PAIR_W]

    split = TRI_LEVELS // 2

    def rows_of(s):
        cb = jnp.where(s < n_ctx, n_ctx - 1 - s, n_chunks - 1 + n_ctx - s)
        return [pl.multiple_of(s * CHUNK, CHUNK), pl.multiple_of(cb * CHUNK, CHUNK)]

    def load_pairs(r0, col0, dtype):
        blk = [qkv_ref[0, pl.ds(r, CHUNK), col0:col0 + DN_KW].astype(dtype) for r in r0]
        return [pair(blk[d], p) for d, p in units]

    def stage_a(s):
        r0 = rows_of(s)
        f = [feat_s[pl.ds(r, CHUNK), :] for r in r0]
        gam = [gam_s[d, pl.ds(r0[d], CHUNK), :] for d in range(2)]
        k = load_pairs(r0, DN_KW, F32)
        q = load_pairs(r0, 0, BF16)
        bcol = [_col_pair(f[d], S_BETA + d * DN_HEADS + 2 * p) for d, p in units]
        kbq = [jnp.concatenate([(k[i] * bcol[i]).astype(BF16), q[i]], axis=0) for i in nu]
        kbd = [_bd(x.astype(BF16)) for x in k]
        gc = [_col_pair(gam[d], S_DECAY + d * DN_HEADS + 2 * p) for d, p in units]
        gr = [_row_form(x, eye) for x in gc]
        dec = [jnp.exp(jnp.where(masks[units[i][0]][0], gc[i] - gr[i], NEG)) for i in nu]
        kkq = [_dot_nt(kbq[i], kbd[i]) for i in nu]
        yield
        a = [jnp.where(masks[units[i][0]][1], kkq[i][0:CHUNK] * dec[i], 0.0).astype(BF16) for i in nu]
        a = [jnp.concatenate([x, x], axis=0) for x in a]
        qk = [(kkq[i][CHUNK:] * dec[i]).astype(BF16) for i in nu]
        t = yield from _tri_inverse_levels(a, None, eye_f, 0, split)
        return a, t, qk

    def stage_b(s, pa):
        a, t, qk = pa
        r0 = rows_of(s)
        f = [feat_s[pl.ds(r, CHUNK), :] for r in r0]
        gam = [gam_s[d, pl.ds(r0[d], CHUNK), :] for d in range(2)]
        k = load_pairs(r0, DN_KW, F32)
        v = load_pairs(r0, 2 * DN_KW, F32)
        q = load_pairs(r0, 0, BF16)
        bcol = [_col_pair(f[d], S_BETA + d * DN_HEADS + 2 * p) for d, p in units]
        gc = [_col_pair(gam[d], S_DECAY + d * DN_HEADS + 2 * p) for d, p in units]
        egam = [jnp.exp(x) for x in gc]
        glast = [gc[i][last[units[i][0]]:last[units[i][0]] + 1, :] for i in nu]
        kdt = [jnp.transpose(k[i] * jnp.exp(glast[i] - gc[i])).astype(BF16) for i in nu]
        qg =[(q[i].astype(F32) * egam[i]).astype(BF16) for i in nu]
        eg = [jnp.exp(x) for x in glast]
        t = yield from _tri_inverse_levels(a, t, eye_f, split, TRI_LEVELS)
        tb = [x.astype(BF16) for x in t]
        rhs = [jnp.concatenate([_bd((v[i] * bcol[i]).astype(BF16)),
                                _bd((k[i] * bcol[i] * egam[i]).astype(BF16))], axis=1) for i in nu]
        uw = [_dot(tb[i], rhs[i]) for i in nu]
        yield
        uu = [x[:, 0:PAIR_W] for x in uw]
        wq = [jnp.concatenate([uw[i][:, PAIR_W:].astype(BF16), qg[i]], axis=0) for i in nu]
        return uu, wq, kdt, eg, qk

    def stage_c(s, pb, with_out):
        uu, wq, kdt, eg, qk = pb
        r0 = rows_of(s)
        st = [st_s[d, p] for d, p in units]
        sb = [x.astype(BF16) for x in st]
        if with_out:
            po = [_dot(wq[i], sb[i]) for i in nu]
            pred = [x[0:CHUNK] for x in po]
            o1 = [x[CHUNK:] for x in po]
        else:
            pred = [_dot(wq[i][0:CHUNK], sb[i]) for i in nu]
        yield
        vnb = [(uu[i] - pred[i]).astype(BF16) for i in nu]
        upd = [_bd_mask(_dot(kdt[i], vnb[i])) for i in nu]
        if with_out:
            o2 = [_dot(qk[i], _bd(vnb[i])) for i in nu]
        yield
        for i, (d, p) in enumerate(units):
            st_s[d, p] = st[i] * eg[i] + upd[i]
        if with_out:
            for d in range(2):
                o_all = jnp.concatenate([o1[i] + o2[i] for i in nu if units[i][0] == d], axis=1)
                rows = pl.ds(pl.multiple_of(r0[d] - off, CHUNK), CHUNK)
                o_s[rows, :] = o_s[rows, :] + o_all

    def put(refs, vals):
        for ref, group in zip(refs, vals):
            for i, x in enumerate(group):
                ref[i] = x if x.shape[0] != 1 else jnp.broadcast_to(x, ref.shape[1:])

    def get(refs, n_rows):
        return tuple([ref[i] if nr is None else ref[i, 0:nr, :] for i in range(ref.shape[0])]
                     for ref, nr in zip(refs, n_rows))

    a_rows = (None, None, None)
    b_rows = (None, None, None, 1, None)

    def pipelined(s_lo, s_hi, with_out):
        def step(s, carry):
            pa, pb = get(pa_s, a_rows), get(pb_s, b_rows)
            _, rb, ra = _interleave(stage_c(s, pb, with_out), stage_b(s + 1, pa), stage_a(s + 2))
            put(pb_s, rb)
            put(pa_s, ra)
            return carry

        lax.fori_loop(s_lo, s_hi, step, 0)

    (ra,) = _interleave(stage_a(0))
    rb, ra = _interleave(stage_b(0, ra), stage_a(1))
    put(pb_s, rb)
    put(pa_s, ra)
    if need_ctx:
        pipelined(0, n_chunks - 2, True)
    else:
        pipelined(0, n_ctx, False)
        pipelined(n_ctx, n_chunks - 2, True)
    _, rb = _interleave(stage_c(n_chunks - 2, get(pb_s, b_rows), True),
                        stage_b(n_chunks - 1, get(pa_s, a_rows)))
    _interleave(stage_c(n_chunks - 1, rb, True))

    li = lax.broadcasted_iota(jnp.int32, (PAIR_W, PAIR_W), 0) // DN_DK
    lj = lax.broadcasted_iota(jnp.int32, (PAIR_W, PAIR_W), 1) // DN_DK
    ones_bd = (li == lj).astype(BF16)

    def finalize(t, carry):
        ro = pl.multiple_of(t * TM, TM)
        o_all = o_s[pl.ds(ro, TM), :]
        z = z_ref[0, pl.ds(pl.multiple_of(ro + off, TM), TM), :].astype(F32)
        ys = [_head_rms(pair(o_all, p), og_ref[...], ones_bd) for p in range(n_pairs)]
        y_ref[0, pl.ds(ro, TM), :] = (jnp.concatenate(ys, axis=1) * _silu(z)).astype(BF16)
        return carry

    lax.fori_loop(0, lo // TM, finalize, 0)


def _gdn_scan_call(qkvn, u, us, alog_pad, dtb_pad, o_gain, need_ctx):
    bsz, lt, _ = qkvn.shape
    lo = lt if need_ctx else lt - CTX_LEN
    return pl.pallas_call(
        functools.partial(_gdn_scan_kernel, need_ctx=need_ctx),
        out_shape=jax.ShapeDtypeStruct((bsz, lo, DN_VW), BF16),
        grid=(bsz,),
        in_specs=[pl.BlockSpec((1, lt, 3 * DN_KW), lambda b: (b, 0, 0)),
                  pl.BlockSpec((1, lt, DN_VW), lambda b: (b, 0, U_DNZ // DN_VW)),
                  pl.BlockSpec((1, lt, SMALL_W), lambda b: (b, 0, 0)),
                  pl.BlockSpec((1, SMALL_W), lambda b: (0, 0)),
                  pl.BlockSpec((1, SMALL_W), lambda b: (0, 0)),
                  pl.BlockSpec((1, PAIR_W), lambda b: (0, 0))],
        out_specs=pl.BlockSpec((1, lo, DN_VW), lambda b: (b, 0, 0)),
        scratch_shapes=[pltpu.VMEM((lt, SMALL_W), F32),
                        pltpu.VMEM((2, lt, SMALL_W), F32),
                        pltpu.VMEM((lo, DN_VW), F32),
                        pltpu.VMEM((2, DN_HEADS // 2, PAIR_W, PAIR_W), F32),
                        pltpu.VMEM((DN_HEADS, PAIR_W, PAIR_W), BF16),
                        pltpu.VMEM((DN_HEADS, CHUNK, PAIR_W), F32),
                        pltpu.VMEM((DN_HEADS, CHUNK, PAIR_W), BF16),
                        pltpu.VMEM((DN_HEADS, CHUNK, PAIR_W), F32),
                        pltpu.VMEM((DN_HEADS, PAIR_W, PAIR_W), BF16),
                        pltpu.VMEM((DN_HEADS, PAIR_W, CHUNK), BF16),
                        pltpu.VMEM((DN_HEADS, 8, PAIR_W), F32),
                        pltpu.VMEM((DN_HEADS, CHUNK, PAIR_W), BF16)],
        compiler_params=_cparams(1),
        name="gdn_scan",
    )(qkvn, u, us, alog_pad, dtb_pad, jnp.tile(o_gain.astype(F32), 2).reshape(1, PAIR_W))


def _ssd_scan_kernel(xbc_ref, z_ref, us_ref, alog_ref, dtb_ref, dskip_ref, og_ref, y_ref,
                     dt_s, la_s, lam_s, y_s, st_s, *pa_s, need_ctx):
    lt = xbc_ref.shape[1]
    n_chunks = lt // CHUNK
    n_ctx = CTX_LEN // CHUNK
    off = 0 if need_ctx else CTX_LEN
    hpg = SSD_HEADS // SSD_GROUPS
    gw = SSD_DI // SSD_GROUPS

    lo = lt - off
    ppg = hpg // 2

    dt = _softplus(us_ref[0] + dtb_ref[...])
    dt_s[...] = dt
    la_s[...] = dt * (-jnp.exp(alog_ref[...]))
    _fill_chunk_cumsums(la_s, lam_s)
    st_s[...] = jnp.zeros(st_s.shape, F32)
    y_s[...] = jnp.zeros(y_s.shape, F32)

    masks = [_tri_masks(False), _tri_masks(True)]
    eye = masks[0][2]
    last = [CHUNK - 1, 0]
    groups = [(d, gi) for d in range(2) for gi in range(SSD_GROUPS)]

    units = [(d, gi, gi * ppg + r) for d, gi in groups for r in range(ppg)]
    nu = range(len(units))
    ng = range(len(groups))
    grp_of = [groups.index((d, gi)) for d, gi, _ in units]

    def rows_of(s):
        cb = jnp.where(s < n_ctx, n_ctx - 1 - s, n_chunks - 1 + n_ctx - s)
        return [pl.multiple_of(s * CHUNK, CHUNK), pl.multiple_of(cb * CHUNK, CHUNK)]

    def load_c(r0):
        return [xbc_ref[0, pl.ds(r0[d], CHUNK), SSD_DI + SSD_BC + gi * SSD_N:SSD_DI + SSD_BC + (gi + 1) * SSD_N]
                for d, gi in groups]

    def stage_a(s):
        r0 = rows_of(s)
        lam = [lam_s[d, pl.ds(r0[d], CHUNK), :] for d in range(2)]
        dtc = [dt_s[pl.ds(r, CHUNK), :] for r in r0]
        bg = [xbc_ref[0, pl.ds(r0[d], CHUNK), SSD_DI + gi * SSD_N:SSD_DI + (gi + 1) * SSD_N] for d, gi in groups]
        cg = load_c(r0)
        bgt = [_transpose_bf16(b) for b in bg]
        cb2 = [_dot_nt(cg[g], jnp.concatenate([bg[g], bg[g]], axis=0)) for g in ng]
        yield
        x = [xbc_ref[0, pl.ds(r0[d], CHUNK), pp * PAIR_W:(pp + 1) * PAIR_W].astype(F32) for d, _, pp in units]
        ln = [S_DT + d * SSD_HEADS + 2 * pp for d, _, pp in units]
        gc = [_col_pair(lam[units[i][0]], ln[i]) for i in nu]
        xdt = [x[i] * _col_pair(dtc[units[i][0]], ln[i]) for i in nu]
        glast = [gc[i][last[units[i][0]]:last[units[i][0]] + 1, :] for i in nu]
        xdec = [(xdt[i] * jnp.exp(glast[i] - gc[i])).astype(BF16) for i in nu]
        gr = [_row_form(g_, eye) for g_ in gc]
        dec = [jnp.exp(jnp.where(masks[units[i][0]][0], gc[i] - gr[i], NEG)) for i in nu]
        y1 = [_dot((cb2[grp_of[i]] * dec[i]).astype(BF16), _bd(xdt[i].astype(BF16))) for i in nu]
        yield
        return bgt, xdec, y1, [jnp.exp(g_) for g_ in gc], [jnp.exp(g_) for g_ in glast]

    def stage_c(s, pa, with_out):
        bgt, xdec, y1, egc, eg = pa
        r0 = rows_of(s)
        ht = [st_s[d, pp] for d, _, pp in units]
        upd = [_dot(bgt[grp_of[i]], xdec[i]) for i in nu]
        if with_out:
            cg = load_c(r0)
            y2 = [_dot(cg[grp_of[i]], ht[i].astype(BF16)) for i in nu]
        yield
        for i, (d, _, pp) in enumerate(units):
            st_s[d, pp] = ht[i] * eg[i] + upd[i]
        if with_out:
            for g, (d, gi) in enumerate(groups):
                y_grp = jnp.concatenate([y1[i] + y2[i] * egc[i] for i in nu if grp_of[i] == g], axis=1)
                rows = pl.ds(pl.multiple_of(r0[d] - off, CHUNK), CHUNK)
                y_s[rows, gi * gw:(gi + 1) * gw] = y_s[rows, gi * gw:(gi + 1) * gw] + y_grp

    def put(vals):
        for ref, group in zip(pa_s, vals):
            for i, v in enumerate(group):
                ref[i] = v if v.shape[0] != 1 else jnp.broadcast_to(v, ref.shape[1:])

    def get():
        return tuple([ref[i] if nr is None else ref[i, 0:nr, :] for i in range(ref.shape[0])]
                     for ref, nr in zip(pa_s, (None, None, None, None, 1)))

    def pipelined(s_lo, s_hi, with_out):
        def step(s, carry):
            _, ra = _interleave(stage_c(s, get(), with_out), stage_a(s + 1))
            put(ra)
            return carry

        lax.fori_loop(s_lo, s_hi, step, 0)

    put(_interleave(stage_a(0))[0])
    if need_ctx:
        pipelined(0, n_chunks - 1, True)
    else:
        pipelined(0, n_ctx, False)
        pipelined(n_ctx, n_chunks - 1, True)
    _interleave(stage_c(n_chunks - 1, get(), True))

    fin = 2 * CHUNK

    def finalize(t, carry):
        ro = pl.multiple_of(t * fin, fin)
        ri = pl.multiple_of(ro + off, fin)
        xs = xbc_ref[0, pl.ds(ri, fin), 0:SSD_DI].astype(F32)
        y_all = (y_s[pl.ds(ro, fin), :] + xs * dskip_ref[...]) * _silu(z_ref[0, pl.ds(ri, fin), :].astype(F32))
        ys = []
        for gi in range(SSD_GROUPS):
            yg = y_all[:, gi * gw:(gi + 1) * gw]
            ms = jnp.mean(yg * yg, axis=-1, keepdims=True)
            ys.append(yg * lax.rsqrt(ms + EPS) * og_ref[:, gi * gw:(gi + 1) * gw])
        y_ref[0, pl.ds(ro, fin), :] = jnp.concatenate(ys, axis=1).astype(BF16)
        return carry

    lax.fori_loop(0, lo // fin, finalize, 0)


def _ssd_scan_call(xbcn, u, us, alog_pad, dtb_pad, d_skip, o_gain, need_ctx):
    bsz, lt, _ = xbcn.shape
    lo = lt if need_ctx else lt - CTX_LEN
    dskip = jnp.repeat(d_skip.astype(F32), SSD_P).reshape(1, SSD_DI)
    return pl.pallas_call(
        functools.partial(_ssd_scan_kernel, need_ctx=need_ctx),
        out_shape=jax.ShapeDtypeStruct((bsz, lo, SSD_DI), BF16),
        grid=(bsz,),
        in_specs=[pl.BlockSpec((1, lt, SSD_XBC), lambda b: (b, 0, 0)),
                  pl.BlockSpec((1, lt, SSD_DI), lambda b: (b, 0, U_SSDZ // SSD_DI)),
                  pl.BlockSpec((1, lt, SMALL_W), lambda b: (b, 0, 0)),
                  pl.BlockSpec((1, SMALL_W), lambda b: (0, 0)),
                  pl.BlockSpec((1, SMALL_W), lambda b: (0, 0)),
                  pl.BlockSpec((1, SSD_DI), lambda b: (0, 0)),
                  pl.BlockSpec((1, SSD_DI), lambda b: (0, 0))],
        out_specs=pl.BlockSpec((1, lo, SSD_DI), lambda b: (b, 0, 0)),
        scratch_shapes=[pltpu.VMEM((lt, SMALL_W), F32),
                        pltpu.VMEM((lt, SMALL_W), F32),
                        pltpu.VMEM((2, lt, SMALL_W), F32),
                        pltpu.VMEM((lo, SSD_DI), F32),
                        pltpu.VMEM((2, SSD_HEADS // 2, SSD_N, PAIR_W), F32),
                        pltpu.VMEM((2 * SSD_GROUPS, SSD_N, CHUNK), BF16),
                        pltpu.VMEM((SSD_HEADS, CHUNK, PAIR_W), BF16),
                        pltpu.VMEM((SSD_HEADS, CHUNK, PAIR_W), F32),
                        pltpu.VMEM((SSD_HEADS, CHUNK, PAIR_W), F32),
                        pltpu.VMEM((SSD_HEADS, 8, PAIR_W), F32)],
        compiler_params=_cparams(1),
        name="ssd_scan",
    )(xbcn, u, us, alog_pad, dtb_pad, dskip, o_gain.astype(F32).reshape(1, SSD_DI))


def _post_kernel(ya_ref, yb_ref, yc_ref, gt_ref, x_ref, ctx_ref, gate1_ref, shift_ref, scale_ref, gate2_ref,
                 g_ref, wpa_ref, wpb_ref, wpc_ref, wout_ref, w1_ref, w2_ref, o_ref, *, t0):
    x = _token_tile(x_ref, ctx_ref, pl.program_id(1) + t0)
    g = _sigmoid(gt_ref[0].astype(F32))
    m = (g[:, 0:D_MODEL] * _dot(ya_ref[0], wpa_ref[...])
         + g[:, D_MODEL:2 * D_MODEL] * _dot(yb_ref[0], wpb_ref[...])
         + g[:, 2 * D_MODEL:3 * D_MODEL] * _dot(yc_ref[0], wpc_ref[...]))
    x1 = x + gate1_ref[0] * _dot(m.astype(BF16), wout_ref[...])
    h = _norm_mod(x1, g_ref[...], shift_ref[0], scale_ref[0]).astype(BF16)
    fc = 1024
    acc = None
    for c in range(D_FF // fc):
        a = jnp.maximum(_dot(h, w1_ref[:, c * fc:(c + 1) * fc]), 0.0)
        part = _dot((a * a).astype(BF16), w2_ref[c * fc:(c + 1) * fc, :])
        acc = part if acc is None else acc + part
    o_ref[0] = x1 + gate2_ref[0] * acc


def _post_call(ya, yb, yc, u, stream, mod3, g, wpa, wpb, wpc, wout, w1, w2, need_ctx):
    bsz, lt, _ = u.shape
    nt = lt // TM
    t0 = 0 if need_ctx else 1
    x_specs, x_args = _stream_specs(stream, t0)

    def mod_spec(k):
        return pl.BlockSpec((1, 1, D_MODEL), lambda b, t: (_mod_row(b, t + t0, bsz), 0, k))

    return pl.pallas_call(
        functools.partial(_post_kernel, t0=t0),
        out_shape=jax.ShapeDtypeStruct((bsz, (nt - t0) * TM, D_MODEL), F32),
        grid=(bsz, nt - t0),
        in_specs=[pl.BlockSpec((1, TM, NA_W), lambda b, t: (b, t, 0)),
                  pl.BlockSpec((1, TM, DN_VW), lambda b, t: (b, t, 0)),
                  pl.BlockSpec((1, TM, SSD_DI), lambda b, t: (b, t, 0)),
                  pl.BlockSpec((1, TM, 3 * D_MODEL), lambda b, t: (b, t + t0, U_GATE // (3 * D_MODEL)))]
        + x_specs + [mod_spec(2), mod_spec(3), mod_spec(4), mod_spec(5),
                     _resident((1, D_MODEL)), _resident((NA_W, D_MODEL)), _resident((DN_VW, D_MODEL)),
                     _resident((SSD_DI, D_MODEL)), _resident((D_MODEL, D_MODEL)),
                     _resident((D_MODEL, D_FF)), _resident((D_FF, D_MODEL))],
        out_specs=pl.BlockSpec((1, TM, D_MODEL), lambda b, t: (b, t, 0)),
        compiler_params=_cparams(2),
        name="post",
    )(ya, yb, yc, u, *x_args, mod3, mod3, mod3, mod3, g, wpa, wpb, wpc, wout, w1, w2)


def _pad_lanes(off, v):
    flat = v.astype(F32).reshape(v.shape[0], 1, -1)
    return jnp.pad(flat, ((0, 0), (0, 0), (off, SMALL_W - off - flat.shape[-1])))


def _split_w_in(w):
    big = jnp.concatenate([w[:, _O_NA:_O_DNQKV], w[:, _O_DNQKV:_O_DNZ], w[:, _O_XBC:_O_SSDDT],
                           w[:, _O_DNZ:_O_DNB], w[:, _O_SSDZ:_O_XBC], w[:, _O_GATE:_O_END]], axis=1)
    small = jnp.concatenate([w[:, _O_DNB:_O_SSDZ], w[:, _O_SSDDT:_O_GATE]], axis=1)
    small = jnp.pad(small, ((0, 0), (0, SMALL_W - small.shape[1])))
    return big.astype(BF16), small.astype(BF16)


def kernel(x, c, ctx, c_ctx, w_ada, b_ada, norm1_g, norm2_g, w_in, na_q_gain, na_k_gain, na_rpb,
           dn_conv_w, dn_a_log, dn_dt_bias, dn_o_gain, ssd_conv_w, ssd_conv_b, ssd_a_log,
           ssd_dt_bias, ssd_d, ssd_o_gain, w_pa, w_pb, w_pc, w_out, w_ff1, w_ff2):
    bsz, seq, _ = x.shape
    assert bsz < MOD_ROWS and seq % TM == 0 and ctx.shape[1] == CTX_LEN
    n_l = w_ada.shape[0]
    cs = jnp.concatenate([c, c_ctx[None, :], jnp.zeros((MOD_ROWS - bsz - 1, D_MODEL), F32)], axis=0)
    mod = _ada_call(cs, w_ada, b_ada)
    stream = (x, ctx)
    lt = CTX_LEN + seq
    rope = _rope_tables(seq)
    zeros_w = jnp.zeros((1, 1536), F32)

    mod4 = mod.reshape(n_l, MOD_ROWS, 1, 6 * D_MODEL)
    alog_dn, dtb_dn = _pad_lanes(S_DECAY, dn_a_log), _pad_lanes(S_DECAY, dn_dt_bias)
    alog_ssd, dtb_ssd = _pad_lanes(S_DT, ssd_a_log), _pad_lanes(S_DT, ssd_dt_bias)
    g1, g2 = norm1_g.reshape(n_l, 1, D_MODEL), norm2_g.reshape(n_l, 1, D_MODEL)

    for l in range(n_l):
        need_ctx = l < n_l - 1
        wb, ws = _split_w_in(w_in[l])
        u, us = _inproj_call(stream, bsz, lt, mod4[l], g1[l], wb, ws)
        ya = _na_call(u, na_q_gain[l], na_k_gain[l], _na_bias_table(na_rpb[l:l + 1])[0], need_ctx)
        qkvn = _conv_call(u, U_DN, dn_conv_w[l], zeros_w, rope, True)
        yb = _gdn_scan_call(qkvn, u, us, alog_dn[l], dtb_dn[l], dn_o_gain[l], need_ctx)
        xbcn = _conv_call(u, U_XBC, ssd_conv_w[l], ssd_conv_b[l], rope, False)
        yc = _ssd_scan_call(xbcn, u, us, alog_ssd[l], dtb_ssd[l], ssd_d[l], ssd_o_gain[l], need_ctx)
        post_w = [w[l].astype(BF16) for w in (w_pa, w_pb, w_pc, w_out, w_ff1, w_ff2)]
        stream = _post_call(ya, yb, yc, u, stream, mod4[l], g2[l], *post_w, need_ctx)
    return stream
```
